```python
import math
import jax, jax.numpy as jnp
from jax import lax
import numpy as np

D_MODEL = 1024
BATCH = 16
SEQ = 256
DEPTH = 2
DEC_BATCH = 8
DEC_SEQ = 4096
PAST_LEN = 512

GRID_W = 64
N_MIXERS = 2
N_HYENA = (DEPTH + 1) // 2
N_GLA = DEPTH // 2
N_DENSE = (DEPTH + 1) // 2
N_MOE = DEPTH // 2
RMS_EPS = 1e-6
HY_SHORT = 3
HY_EMB = 33
HY_BANDS = (HY_EMB - 1) // 2
HY_FFN = 64
HY_FAST_PCT = 0.3
HY_SLOW_PCT = 1.5
HY_TARGET = 1e-2
HY_MAX_DECAY = math.log(HY_TARGET) / HY_FAST_PCT
HY_MIN_DECAY = math.log(HY_TARGET) / HY_SLOW_PCT
GLA_HEADS = 4
GLA_KD = D_MODEL // 2
GLA_VD = D_MODEL
GLA_DK = GLA_KD // GLA_HEADS
GLA_DV = GLA_VD // GLA_HEADS
GLA_RANK = 16
GLA_GATE_NORM = 16.0
GLA_CHUNK = 64
D_FF = 11 * D_MODEL // 4
N_EXPERTS = 8
TOP_K = 2
D_FF_EXPERT = 7 * D_MODEL // 2
MOE_BLOCK = 256

kernel_name = 'hybrid_hyena_gla_diffusion_step'


def rmsnorm(x, g):
    xf = x.astype(jnp.float32)
    y = xf * lax.rsqrt(jnp.mean(xf * xf, axis=-1, keepdims=True) + RMS_EPS)
    return (y * g.astype(jnp.float32)).astype(x.dtype)


def swiglu(x, w_gu, w_d):
    g, u = jnp.split(x @ w_gu, 2, axis=-1)
    return (jax.nn.silu(g) * u) @ w_d


def short_conv(u, w, b, grid_rows):
    bsz, L, C = u.shape
    n = L // grid_rows
    pad = HY_SHORT // 2
    up = jnp.pad(u.reshape(bsz, grid_rows, n, C), ((0, 0), (0, 0), (pad, pad), (0, 0)))
    y = b
    for j in range(HY_SHORT):
        y = y + up[:, :, j:j + n] * w[j]
    return y.reshape(bsz, L, C)


def hyena_filters(L, w1, b1, w2, b2, freq, w3):
    t = jnp.linspace(0.0, 1.0, L, dtype=jnp.float32)[:, None]
    w = 2.0 * math.pi * jnp.arange(L, dtype=jnp.float32)[:, None] / L
    f = jnp.linspace(1e-4, HY_BANDS - 1, HY_BANDS, dtype=jnp.float32)[None, :]
    z = jnp.concatenate([t, jnp.cos(f * w), -jnp.sin(f * w)], axis=-1)
    h = jnp.sin(freq[0] * (z @ w1 + b1))
    h = jnp.sin(freq[1] * (h @ w2 + b2))
    h = (h @ w3).astype(jnp.float32).reshape(L, 2, D_MODEL)
    deltas = jnp.abs(jnp.linspace(HY_MIN_DECAY, HY_MAX_DECAY, D_MODEL, dtype=jnp.float32))
    h = h * jnp.exp(-t[:, :, None] * deltas)
    return h[:, 0], h[:, 1]


def bidir_fftconv(u, hf, hb):
    L = u.shape[1]
    kern = jnp.concatenate([hf, jnp.zeros((1, hf.shape[1]), hf.dtype), hb[1:][::-1]], axis=0)
    uf = jnp.fft.rfft(u.astype(jnp.float32), n=2 * L, axis=1)
    kf = jnp.fft.rfft(kern, axis=0)
    return jnp.fft.irfft(uf * kf[None], n=2 * L, axis=1)[:, :L]


def hyena_mixer(h, grid_rows, in_w, in_b, sc_w, sc_b, f_w1, f_b1, f_w2, f_b2, f_freq, f_w3, skip, out_w, out_b):
    L = h.shape[1]
    uc = short_conv(h @ in_w + in_b, sc_w, sc_b, grid_rows)
    x0, x1, v = jnp.split(uc, 3, axis=-1)
    v = v * x1
    hf, hb = hyena_filters(L, f_w1, f_b1, f_w2, f_b2, f_freq, f_w3)
    v = (bidir_fftconv(v, hf, hb) + v.astype(jnp.float32) * skip).astype(h.dtype)
    return (v * x0) @ out_w + out_b


def gla_scan(q, k, v, g, s0):
    bsz, L = q.shape[:2]
    n = L // GLA_CHUNK

    def chunks(a):
        return a.reshape(bsz, n, GLA_CHUNK, GLA_HEADS, a.shape[-1]).transpose(1, 0, 3, 2, 4)

    q, k, v, g = chunks(q), chunks(k), chunks(v), chunks(g)
    b = jnp.cumsum(g, axis=3)
    b_last = b[:, :, :, -1:, :]
    qd = q * jnp.exp(b)
    mask = jnp.tril(jnp.ones((GLA_CHUNK, GLA_CHUNK), dtype=bool))
    att = jnp.where(mask, jnp.einsum('nbhcd,nbhsd->nbhcs', qd, k * jnp.exp(-b)), 0.0)
    o_intra = jnp.einsum('nbhcs,nbhsv->nbhcv', att, v)
    k_state = k * jnp.exp(b_last - b)
    decay = jnp.exp(b_last[:, :, :, 0, :])

    def step(s, inp):
        qc, kc, vc, dc = inp
        o = jnp.einsum('bhcd,bhdv->bhcv', qc, s)
        s = s * dc[..., None] + jnp.einsum('bhcd,bhcv->bhdv', kc, vc)
        return s, o

    s_final, o_inter = lax.scan(step, s0, (qd, k_state, v, decay))
    o = (o_intra + o_inter).transpose(1, 0, 3, 2, 4).reshape(bsz, L, GLA_HEADS, GLA_DV)
    return o, s_final


def gla_mixer(h, s0_f, s0_b, qkv_w, gk_w1, gk_w2, gk_b, r_w, r_b, onorm_g, out_w):
    bsz, L, _ = h.shape
    q, k, v = jnp.split(h @ qkv_w, [GLA_KD, 2 * GLA_KD], axis=-1)

    def heads(a, d):
        return a.astype(jnp.float32).reshape(bsz, L, GLA_HEADS, d)

    q = heads(q, GLA_DK) * (GLA_DK ** -0.5)
    k = heads(k, GLA_DK)
    v = heads(v, GLA_DV)

    def log_gate(d):
        return heads(jax.nn.log_sigmoid(((h @ gk_w1[d]) @ gk_w2[d] + gk_b[d]).astype(jnp.float32)) / GLA_GATE_NORM, GLA_DK)

    def flip(a):
        return a[:, ::-1]

    o_f, s_f = gla_scan(q, k, v, log_gate(0), s0_f.astype(jnp.float32))
    o_b, s_b = gla_scan(flip(q), flip(k), flip(v), flip(log_gate(1)), s0_b.astype(jnp.float32))
    o = rmsnorm(o_f + flip(o_b), onorm_g).reshape(bsz, L, GLA_VD).astype(h.dtype)
    r = jax.nn.silu(h @ r_w + r_b)
    return (o * r) @ out_w, s_f, s_b


def moe_swiglu(x, router_w, w_gu, w_d):
    bsz, L, d = x.shape
    T = bsz * L
    xt = x.reshape(T, d)
    logits = (xt @ router_w).astype(jnp.float32)
    top_v, top_i = lax.top_k(logits, TOP_K)
    gate = jax.nn.softmax(top_v, axis=-1)
    flat_e = top_i.reshape(-1)
    n_assign = T * TOP_K
    flat_tok = jnp.arange(n_assign, dtype=jnp.int32) // TOP_K
    order = jnp.argsort(flat_e, stable=True)
    s_e = flat_e[order]
    s_tok = flat_tok[order]
    s_w = gate.reshape(-1)[order]
    counts = jnp.bincount(flat_e, length=N_EXPERTS)
    starts = jnp.cumsum(counts) - counts
    p_counts = (counts + MOE_BLOCK - 1) // MOE_BLOCK * MOE_BLOCK
    p_ends = jnp.cumsum(p_counts)
    p_starts = p_ends - p_counts
    dest = p_starts[s_e] + jnp.arange(n_assign, dtype=jnp.int32) - starts[s_e]
    n_rows = n_assign + N_EXPERTS * MOE_BLOCK
    n_blocks = n_rows // MOE_BLOCK
    row_tok = jnp.full((n_rows,), T, jnp.int32).at[dest].set(s_tok)
    x_pad = jnp.concatenate([xt, jnp.zeros((1, d), xt.dtype)], axis=0)
    xb = x_pad[row_tok].reshape(n_blocks, MOE_BLOCK, d)
    block_e = jnp.minimum(jnp.searchsorted(p_ends, jnp.arange(n_blocks, dtype=jnp.int32) * MOE_BLOCK, side='right'), N_EXPERTS - 1)
    yb = lax.map(lambda a: swiglu(a[0], w_gu[a[1]], w_d[a[1]]), (xb, block_e)).reshape(n_rows, d)
    y = jnp.zeros((T, d), x.dtype).at[s_tok].add(yb[dest] * s_w[:, None].astype(x.dtype))
    return y.reshape(bsz, L, d)


def trunk(x, cond, grid_rows, s_init, p):
    states = []
    cs = jax.nn.silu(cond)
    for l in range(DEPTH):
        mod = (cs @ p['ada_w'][l] + p['ada_b'][l])[:, None, :]
        sh1, sc1, g1, sh2, sc2, g2 = jnp.split(mod, 6, axis=-1)
        ng = p['norm_g'][l]
        i = l // N_MIXERS
        h = rmsnorm(x, ng[0]) * (1.0 + sc1) + sh1
        if l % N_MIXERS == 0:
            m = hyena_mixer(h, grid_rows, p['hy_in_w'][i], p['hy_in_b'][i], p['hy_sc_w'][i], p['hy_sc_b'][i],
                            p['hy_f_w1'][i], p['hy_f_b1'][i], p['hy_f_w2'][i], p['hy_f_b2'][i], p['hy_f_freq'][i],
                            p['hy_f_w3'][i], p['hy_skip'][i], p['hy_out_w'][i], p['hy_out_b'][i])
        else:
            m, s_f, s_b = gla_mixer(h, s_init[:, i, 0], s_init[:, i, 1], p['gla_qkv_w'][i], p['gla_gk_w1'][i],
                                    p['gla_gk_w2'][i], p['gla_gk_b'][i], p['gla_r_w'][i], p['gla_r_b'][i],
                                    p['gla_onorm_g'][i], p['gla_out_w'][i])
            states.append(jnp.stack([s_f, s_b], axis=1))
        x = x + g1 * rmsnorm(m, ng[1])
        h = rmsnorm(x, ng[2]) * (1.0 + sc2) + sh2
        j = l // 2
        if l % 2 == 0:
            f = swiglu(h, p['ffn_wgu'][j], p['ffn_wd'][j])
        else:
            f = moe_swiglu(h, p['moe_router'][j], p['moe_wgu'][j], p['moe_wd'][j])
        x = x + g2 * rmsnorm(f, ng[3])
    return x, jnp.stack(states, axis=1).astype(x.dtype)


def setup_inputs(seed: int = 0) -> dict:
    key = jax.random.key(seed)
    keys = iter(jax.random.split(key, 48))

    def nrm(shape, scale):
        return jax.random.normal(next(keys), shape, jnp.float32) * scale

    D = D_MODEL
    return {
        'x_prompt': nrm((BATCH, SEQ, D), 1.0),
        'x_sample': nrm((DEC_BATCH, DEC_SEQ, D), 1.0),
        'state_gla': nrm((DEC_BATCH, N_GLA, 2, GLA_HEADS, GLA_DK, GLA_DV), 0.5),
        'c': nrm((DEC_BATCH, D), 1.0),
        'c_ctx': nrm((D,), 1.0),
        'ada_w': nrm((DEPTH, D, 6 * D), D ** -0.5),
        'ada_b': nrm((DEPTH, 6 * D), 0.02),
        'norm_g': 1.0 + nrm((DEPTH, 4, D), 0.05),
        'hy_in_w': nrm((N_HYENA, D, 3 * D), D ** -0.5),
        'hy_in_b': nrm((N_HYENA, 3 * D), 0.02),
        'hy_sc_w': nrm((N_HYENA, HY_SHORT, 3 * D), HY_SHORT ** -0.5),
        'hy_sc_b': nrm((N_HYENA, 3 * D), 0.02),
        'hy_f_w1': nrm((N_HYENA, HY_EMB, HY_FFN), HY_EMB ** -0.5),
        'hy_f_b1': nrm((N_HYENA, HY_FFN), 0.02),
        'hy_f_w2': nrm((N_HYENA, HY_FFN, HY_FFN), HY_FFN ** -0.5),
        'hy_f_b2': nrm((N_HYENA, HY_FFN), 0.02),
        'hy_f_freq': 1.0 + nrm((N_HYENA, 2, HY_FFN), 0.05),
        'hy_f_w3': nrm((N_HYENA, HY_FFN, 2 * D), HY_FFN ** -0.5),
        'hy_skip': nrm((N_HYENA, D), 1.0),
        'hy_out_w': nrm((N_HYENA, D, D), D ** -0.5),
        'hy_out_b': nrm((N_HYENA, D), 0.02),
        'gla_qkv_w': nrm((N_GLA, D, 2 * GLA_KD + GLA_VD), D ** -0.5),
        'gla_gk_w1': nrm((N_GLA, 2, D, GLA_RANK), D ** -0.5),
        'gla_gk_w2': nrm((N_GLA, 2, GLA_RANK, GLA_KD), GLA_RANK ** -0.5),
        'gla_gk_b': nrm((N_GLA, 2, GLA_KD), 0.02),
        'gla_r_w': nrm((N_GLA, D, GLA_VD), D ** -0.5),
        'gla_r_b': nrm((N_GLA, GLA_VD), 0.02),
        'gla_onorm_g': 1.0 + nrm((N_GLA, GLA_DV), 0.05),
        'gla_out_w': nrm((N_GLA, GLA_VD, D), GLA_VD ** -0.5),
        'ffn_wgu': nrm((N_DENSE, D, 2 * D_FF), D ** -0.5),
        'ffn_wd': nrm((N_DENSE, D_FF, D), D_FF ** -0.5),
        'moe_router': nrm((N_MOE, D, N_EXPERTS), D ** -0.5),
        'moe_wgu': nrm((N_MOE, N_EXPERTS, D, 2 * D_FF_EXPERT), D ** -0.5),
        'moe_wd': nrm((N_MOE, N_EXPERTS, D_FF_EXPERT, D), D_FF_EXPERT ** -0.5),
    }


def reference(x_prompt, x_sample, state_gla, c, c_ctx, ada_w, ada_b, norm_g, hy_in_w, hy_in_b, hy_sc_w, hy_sc_b,
              hy_f_w1, hy_f_b1, hy_f_w2, hy_f_b2, hy_f_freq, hy_f_w3, hy_skip, hy_out_w, hy_out_b, gla_qkv_w,
              gla_gk_w1, gla_gk_w2, gla_gk_b, gla_r_w, gla_r_b, gla_onorm_g, gla_out_w, ffn_wgu, ffn_wd,
              moe_router, moe_wgu, moe_wd):
    p = dict(ada_w=ada_w, ada_b=ada_b, norm_g=norm_g, hy_in_w=hy_in_w, hy_in_b=hy_in_b, hy_sc_w=hy_sc_w,
             hy_sc_b=hy_sc_b, hy_f_w1=hy_f_w1, hy_f_b1=hy_f_b1, hy_f_w2=hy_f_w2, hy_f_b2=hy_f_b2,
             hy_f_freq=hy_f_freq, hy_f_w3=hy_f_w3, hy_skip=hy_skip, hy_out_w=hy_out_w, hy_out_b=hy_out_b,
             gla_qkv_w=gla_qkv_w, gla_gk_w1=gla_gk_w1, gla_gk_w2=gla_gk_w2, gla_gk_b=gla_gk_b, gla_r_w=gla_r_w,
             gla_r_b=gla_r_b, gla_onorm_g=gla_onorm_g, gla_out_w=gla_out_w, ffn_wgu=ffn_wgu, ffn_wd=ffn_wd,
             moe_router=moe_router, moe_wgu=moe_wgu, moe_wd=moe_wd)
    zero_state = jnp.zeros((x_prompt.shape[0], N_GLA, 2, GLA_HEADS, GLA_DK, GLA_DV), jnp.float32)
    y_prompt, state_gla_new = trunk(x_prompt, c_ctx[None, :], 1, zero_state, p)
    rows = x_sample.shape[1] // GRID_W
    y_sample = trunk(x_sample, c, rows, state_gla, p)[0]
    return (y_prompt, y_sample, state_gla_new)
```

```python
import functools
import math

import jax
import jax.numpy as jnp
from jax import lax
from jax.experimental import pallas as pl
from jax.experimental.pallas import tpu as pltpu

F32 = jnp.float32
BF = jnp.bfloat16

D = 1024
RMS_EPS = 1e-6
DEPTH = 2
HY_SHORT = 3
HY_EMB = 33
HY_BANDS = (HY_EMB - 1) // 2
HY_FFN = 64
HY_MAX_DECAY = math.log(1e-2) / 0.3
HY_MIN_DECAY = math.log(1e-2) / 1.5
H = 4
DK = 128
DV = 256
KD = H * DK
GLA_RANK = 16
GLA_GATE_NORM = 16.0
CHUNK = 64
D_FF = 11 * D // 4
N_EXPERTS = 8
D_FF_EXPERT = 7 * D // 2

LANES = 128
VMEM_LIMIT_BYTES = 56 * 1024 * 1024
ROW_TILE = 512
FFT_S2 = 128


def _params(n_axes):
    return pltpu.CompilerParams(dimension_semantics=("arbitrary",) * n_axes,
                                vmem_limit_bytes=VMEM_LIMIT_BYTES)


def _dot(a, b):
    return jnp.dot(a, b, preferred_element_type=F32)


def _dot_nt(a, b):
    return lax.dot_general(a, b, (((1,), (1,)), ((), ())), preferred_element_type=F32)


def _dot_tn(a, b):
    return lax.dot_general(a, b, (((0,), (0,)), ((), ())), preferred_element_type=F32)


def _rms(x, g):
    return x * lax.rsqrt(jnp.mean(x * x, axis=-1, keepdims=True) + RMS_EPS) * g


def _silu(x):
    return x * (1.0 / (1.0 + jnp.exp(-x)))


class _Rows:
    def __init__(self, B, L, tm=ROW_TILE):
        self.B, self.L = B, L
        if L >= tm:
            self.bb, self.tl = 1, tm
        else:
            self.bb, self.tl = tm // L, L
        assert L % self.tl == 0 and B % self.bb == 0
        self.nl = L // self.tl
        self.n = (B // self.bb) * self.nl
        self.tm = self.bb * self.tl

    def act(self, width):
        nl = self.nl
        return pl.BlockSpec((self.bb, self.tl, width), lambda i: (i // nl, i % nl, 0))

    def mod(self, m):
        nl = self.nl
        if m.shape[0] == 1:
            return pl.BlockSpec((1, 1, D), lambda i: (0, 0, 0))
        return pl.BlockSpec((self.bb, 1, D), lambda i: (i // nl, 0, 0))


def _const(shape):
    nd = len(shape)
    return pl.BlockSpec(shape, lambda *_: (0,) * nd)


def _ada_kernel(c_ref, w_ref, b_ref, o_ref):
    cs = _silu(c_ref[...])
    o_ref[0] = _dot(cs.astype(BF), w_ref[0].astype(BF)) + b_ref[0]


def _ada(cond, ada_w, ada_b):
    R = cond.shape[0]
    tn = 1536
    return pl.pallas_call(
        _ada_kernel,
        grid=(DEPTH, 6 * D // tn),
        in_specs=[pl.BlockSpec((R, D), lambda l, n: (0, 0)),
                  pl.BlockSpec((1, D, tn), lambda l, n: (l, 0, n)),
                  pl.BlockSpec((1, 1, tn), lambda l, n: (l, 0, n))],
        out_specs=pl.BlockSpec((1, R, tn), lambda l, n: (l, 0, n)),
        out_shape=jax.ShapeDtypeStruct((DEPTH, R, 6 * D), F32),
        compiler_params=_params(2),
    )(cond, ada_w, ada_b.reshape(DEPTH, 1, 6 * D))


def _hy_in_kernel(x_ref, ng_ref, sc_ref, sh_ref, w_ref, b_ref, cw_ref, cb_ref, v_ref, x0_ref, *, n_row):
    x = x_ref[...]
    bb, tl, _ = x.shape
    tm = bb * tl
    h = _rms(x, ng_ref[...]) * (1.0 + sc_ref[...]) + sh_ref[...]
    hb = h.reshape(tm, D).astype(BF)
    pos = lax.broadcasted_iota(jnp.int32, (tm, D), 0) & (n_row - 1)
    first = pos == 0
    last = pos == n_row - 1
    parts = []
    for j in range(3):
        cols = slice(j * D, (j + 1) * D)
        u = _dot(hb, w_ref[:, cols]) + b_ref[:, cols]
        up = jnp.where(first, 0.0, pltpu.roll(u, 1, 0))
        dn = jnp.where(last, 0.0, pltpu.roll(u, tm - 1, 0))
        parts.append(cb_ref[:, cols] + up * cw_ref[0:1, cols] + u * cw_ref[1:2, cols] + dn * cw_ref[2:3, cols])
    x0, x1, v = parts
    v_ref[...] = (v * x1).astype(BF).reshape(bb, tl, D)
    x0_ref[...] = x0.astype(BF).reshape(bb, tl, D)


def _hy_in(x, ng, sc, sh, w, b, cw, cb, n_row):
    B, L, _ = x.shape
    assert n_row & (n_row - 1) == 0
    r = _Rows(B, L)
    assert r.tl % n_row == 0
    return pl.pallas_call(
        functools.partial(_hy_in_kernel, n_row=n_row),
        grid=(r.n,),
        in_specs=[r.act(D), _const((1, 1, D)), r.mod(sc), r.mod(sh),
                  _const((D, 3 * D)), _const((1, 3 * D)), _const((HY_SHORT, 3 * D)), _const((1, 3 * D))],
        out_specs=[r.act(D), r.act(D)],
        out_shape=[jax.ShapeDtypeStruct((B, L, D), BF)] * 2,
        compiler_params=_params(1),
    )(x, ng.reshape(1, 1, D), sc, sh, w.astype(BF), b.reshape(1, 3 * D), cw, cb.reshape(1, 3 * D))


def _filter_kernel(z_ref, w1_ref, b1_ref, w2_ref, b2_ref, fr_ref, w3_ref, dl_ref, o_ref, *, L):
    tr = z_ref.shape[0]
    hp = lax.Precision.HIGHEST
    z = z_ref[...]
    h = jnp.sin(fr_ref[0:1, :] * (jnp.dot(z, w1_ref[...], precision=hp, preferred_element_type=F32) + b1_ref[...]))
    h = jnp.sin(fr_ref[1:2, :] * (jnp.dot(h, w2_ref[...], precision=hp, preferred_element_type=F32) + b2_ref[...]))
    hw = _dot(h.astype(BF), w3_ref[...])
    n = pl.program_id(0) * tr + lax.broadcasted_iota(jnp.int32, (tr, D), 0)
    taps = jnp.where(n < L, hw[:, :D], hw[:, D:]) * jnp.exp(-z[:, 0:1] * dl_ref[...])
    o_ref[...] = jnp.where(n == L, 0.0, taps)


def _hyena_kernel_taps(L, w1, b1, w2, b2, freq, w3):
    n = jnp.arange(2 * L, dtype=jnp.int32)
    pos = jnp.where(n < L, n, 2 * L - n) % L
    t = jnp.linspace(0.0, 1.0, L, dtype=F32)[pos][:, None]
    w = (2.0 * math.pi * pos.astype(F32) / L)[:, None]
    f = jnp.linspace(1e-4, HY_BANDS - 1, HY_BANDS, dtype=F32)[None, :]
    z = jnp.concatenate([t, jnp.cos(f * w), -jnp.sin(f * w)], axis=-1)
    z = jnp.pad(z, ((0, 0), (0, LANES - HY_EMB)))
    pad = LANES - HY_FFN
    w1p = jnp.pad(w1, ((0, LANES - HY_EMB), (0, pad)))
    w2p = jnp.pad(w2, ((0, pad), (0, pad)))
    w3p = jnp.pad(w3, ((0, pad), (0, 0))).astype(BF)
    b1p = jnp.pad(b1, (0, pad)).reshape(1, LANES)
    b2p = jnp.pad(b2, (0, pad)).reshape(1, LANES)
    frp = jnp.pad(freq, ((0, 0), (0, pad)))
    deltas = jnp.abs(jnp.linspace(HY_MIN_DECAY, HY_MAX_DECAY, D, dtype=F32)).reshape(1, D)
    tr = 512
    return pl.pallas_call(
        functools.partial(_filter_kernel, L=L),
        grid=(2 * L // tr,),
        in_specs=[pl.BlockSpec((tr, LANES), lambda i: (i, 0)), _const((LANES, LANES)), _const((1, LANES)),
                  _const((LANES, LANES)), _const((1, LANES)), _const((2, LANES)), _const((LANES, 2 * D)),
                  _const((1, D))],
        out_specs=pl.BlockSpec((tr, D), lambda i: (i, 0)),
        out_shape=jax.ShapeDtypeStruct((2 * L, D), F32),
        compiler_params=_params(1),
    )(z, w1p, b1p, w2p, b2p, frp, w3p, deltas)


def _cis(rows, cols, n, sign, scale=1.0):
    ph = (rows[:, None] * cols[None, :]) % n
    ang = ph.astype(F32) * (2.0 * math.pi / n)
    return jnp.cos(ang) * scale, jnp.sin(ang) * (sign * scale)


def _cplx_block(cr, ci):
    return jnp.concatenate([jnp.concatenate([cr, -ci], 1), jnp.concatenate([ci, cr], 1)], 0)


def _fft_short_kernel(v_ref, x0_ref, kern_ref, mk_ref, mf_ref, mi_ref, sk_ref, o_ref, ks_ref, *, L):
    n2 = 2 * L

    @pl.when(pl.program_id(0) == 0)
    def _():
        ks_ref[...] = _dot(mk_ref[...], kern_ref[...].astype(BF))

    z = jnp.concatenate([v_ref[0], v_ref[1]], 0)
    u = _dot(mf_ref[...], z)
    ur, ui = u[:n2], u[n2:]
    kr, ki = ks_ref[:n2, :], ks_ref[n2:, :]
    y = jnp.concatenate([ur * kr - ui * ki, ur * ki + ui * kr], 0).astype(BF)
    t = _dot(mi_ref[...], y)
    for j in range(2):
        conv = t[j * L:(j + 1) * L]
        o_ref[j] = ((conv + v_ref[j].astype(F32) * sk_ref[...]) * x0_ref[j].astype(F32)).astype(BF)


def _fftconv_short(v, x0, kern, skip):
    B, L, _ = v.shape
    n2 = 2 * L
    k = jnp.arange(n2, dtype=jnp.int32)
    s = jnp.arange(L, dtype=jnp.int32)
    fr, fi = _cis(k, s, n2, -1.0)
    mf = _cplx_block(fr, fi).astype(BF)
    kr, ki = _cis(k, k, n2, -1.0)
    mk = jnp.concatenate([kr, ki], 0).astype(BF)
    ir, ii = _cis(s, k, n2, 1.0, 1.0 / n2)
    mi = _cplx_block(ir, ii).astype(BF)
    pair = lambda p: (p, 0, 0)
    return pl.pallas_call(
        functools.partial(_fft_short_kernel, L=L),
        grid=(B // 2,),
        in_specs=[pl.BlockSpec((2, L, D), pair), pl.BlockSpec((2, L, D), pair), _const((n2, D)),
                  _const((2 * n2, n2)), _const((2 * n2, 2 * L)), _const((2 * L, 2 * n2)), _const((1, D))],
        out_specs=pl.BlockSpec((2, L, D), pair),
        out_shape=jax.ShapeDtypeStruct((B, L, D), BF),
        scratch_shapes=[pltpu.VMEM((2 * n2, D), F32)],
        compiler_params=_params(1),
    )(v, x0, kern, mk, mf, mi, skip.reshape(1, D))


def _lmul_kernel(m_ref, x_ref, o_ref):
    o_ref[0] = _dot(m_ref[...], x_ref[0].astype(BF)).astype(o_ref.dtype)


def _lmul(m, x, cb=8192):
    G, K, NC = x.shape
    R = m.shape[0]
    return pl.pallas_call(
        _lmul_kernel,
        grid=(G, NC // cb),
        in_specs=[pl.BlockSpec((R, K), lambda g, c: (0, 0)), pl.BlockSpec((1, K, cb), lambda g, c: (g, 0, c))],
        out_specs=pl.BlockSpec((1, R, cb), lambda g, c: (g, 0, c)),
        out_shape=jax.ShapeDtypeStruct((G, R, NC), BF),
        compiler_params=_params(2),
    )(m, x)


def _fft_mid_kernel(a_ref, af_ref, mf_ref, mi_ref, o_ref, ks_ref):
    s2 = FFT_S2
    mf = mf_ref[0]

    @pl.when(pl.program_id(1) == 0)
    def _():
        ks_ref[...] = _dot(mf, jnp.concatenate([af_ref[0, 0], af_ref[1, 0]], 0))

    u = _dot(mf, jnp.concatenate([a_ref[0, 0, 0], a_ref[0, 1, 0]], 0))
    ur, ui = u[:s2], u[s2:]
    kr, ki = ks_ref[:s2, :], ks_ref[s2:, :]
    y = jnp.concatenate([ur * kr - ui * ki, ur * ki + ui * kr], 0).astype(BF)
    z = _dot(mi_ref[0], y).astype(BF)
    o_ref[0, 0, 0] = z[:s2]
    o_ref[0, 1, 0] = z[s2:]


def _lmul_out_kernel(m_ref, z_ref, v_ref, x0_ref, sk_ref, o_ref):
    conv = _dot(m_ref[...], z_ref[0])
    o_ref[0] = ((conv + v_ref[0].astype(F32) * sk_ref[...]) * x0_ref[0].astype(F32)).astype(BF)


def _fftconv_long(v, x0, kern, skip, cb=8192):
    B, L, _ = v.shape
    n2 = 2 * L
    s2 = FFT_S2
    s1 = n2 // s2
    s1h = s1 // 2
    G = B // 2
    nc = s2 * D
    i1 = jnp.arange(s1, dtype=jnp.int32)
    i1h = jnp.arange(s1h, dtype=jnp.int32)
    i2 = jnp.arange(s2, dtype=jnp.int32)
    cr, ci = _cis(i1, i1h, s1, -1.0)
    m1 = _cplx_block(cr, ci).astype(BF)
    cr, ci = _cis(i1, i1, s1, -1.0)
    m1f = jnp.concatenate([cr, ci], 0).astype(BF)
    kk = (i1[:, None] + s1 * i2[None, :]).reshape(-1)
    gr, gi = _cis(kk, i2, n2, -1.0)
    mf = jax.vmap(_cplx_block)(gr.reshape(s1, s2, s2), gi.reshape(s1, s2, s2)).astype(BF)
    hr, hi = _cis(i2, kk, n2, 1.0, 1.0 / n2)
    hr = hr.reshape(s2, s1, s2).transpose(1, 0, 2)
    hi = hi.reshape(s2, s1, s2).transpose(1, 0, 2)
    mi = jax.vmap(_cplx_block)(hr, hi).astype(BF)
    er, ei = _cis(i1h, i1, s1, 1.0)
    m3 = _cplx_block(er, ei).astype(BF)

    a = _lmul(m1, v.reshape(G, 2 * s1h, nc), cb)
    af = _lmul(m1f, kern.reshape(1, s1, nc), cb)
    zz = pl.pallas_call(
        _fft_mid_kernel,
        grid=(s1, G),
        in_specs=[pl.BlockSpec((1, 2, 1, s2, D), lambda k, p: (p, 0, k, 0, 0)),
                  pl.BlockSpec((2, 1, s2, D), lambda k, p: (0, k, 0, 0)),
                  pl.BlockSpec((1, 2 * s2, 2 * s2), lambda k, p: (k, 0, 0)),
                  pl.BlockSpec((1, 2 * s2, 2 * s2), lambda k, p: (k, 0, 0))],
        out_specs=pl.BlockSpec((1, 2, 1, s2, D), lambda k, p: (p, 0, k, 0, 0)),
        out_shape=jax.ShapeDtypeStruct((G, 2, s1, s2, D), BF),
        scratch_shapes=[pltpu.VMEM((2 * s2, D), F32)],
        compiler_params=_params(2),
    )(a.reshape(G, 2, s1, s2, D), af.reshape(2, s1, s2, D), mf, mi)
    blk = lambda g, c: (g, 0, c)
    out = pl.pallas_call(
        _lmul_out_kernel,
        grid=(G, nc // cb),
        in_specs=[pl.BlockSpec((2 * s1h, 2 * s1), lambda g, c: (0, 0)), pl.BlockSpec((1, 2 * s1, cb), blk),
                  pl.BlockSpec((1, 2 * s1h, cb), blk), pl.BlockSpec((1, 2 * s1h, cb), blk),
                  pl.BlockSpec((1, cb), lambda g, c: (0, 0))],
        out_specs=pl.BlockSpec((1, 2 * s1h, cb), blk),
        out_shape=jax.ShapeDtypeStruct((G, 2 * s1h, nc), BF),
        compiler_params=_params(2),
    )(m3, zz.reshape(G, 2 * s1, nc), v.reshape(G, 2 * s1h, nc), x0.reshape(G, 2 * s1h, nc),
      jnp.tile(skip.reshape(1, D), (1, cb // D)))
    return out.reshape(B, L, D)


def _hy_out_kernel(vx_ref, w_ref, b_ref, x_ref, g1_ref, ng1_ref, ng2_ref, sc_ref, sh_ref, x1_ref, h_ref):
    bb, tl, _ = x_ref.shape
    m = _dot(vx_ref[...].reshape(bb * tl, D), w_ref[...]) + b_ref[...]
    x1 = x_ref[...] + g1_ref[...] * _rms(m, ng1_ref[0]).reshape(bb, tl, D)
    x1_ref[...] = x1
    h_ref[...] = (_rms(x1, ng2_ref[...]) * (1.0 + sc_ref[...]) + sh_ref[...]).astype(h_ref.dtype)


def _hy_out(vx, w, b, x, g1, ng1, ng2, sc, sh):
    B, L, _ = x.shape
    r = _Rows(B, L)
    return pl.pallas_call(
        _hy_out_kernel,
        grid=(r.n,),
        in_specs=[r.act(D), _const((D, D)), _const((1, D)), r.act(D), r.mod(g1), _const((1, 1, D)),
                  _const((1, 1, D)), r.mod(sc), r.mod(sh)],
        out_specs=[r.act(D), r.act(D)],
        out_shape=[jax.ShapeDtypeStruct((B, L, D), F32), jax.ShapeDtypeStruct((B, L, D), BF)],
        compiler_params=_params(1),
    )(vx, w.astype(BF), b.reshape(1, D), x, g1, ng1.reshape(1, 1, D), ng2.reshape(1, 1, D), sc, sh)


def _swiglu_kernel(be_ref, nv_ref, x_ref, wg_ref, wu_ref, wd_ref, o_ref, acc_ref):
    del be_ref
    f = pl.program_id(1)

    @pl.when(f == 0)
    def _():
        acc_ref[...] = jnp.zeros_like(acc_ref)

    @pl.when(pl.program_id(0) < nv_ref[0])
    def _():
        x = x_ref[...].astype(BF)
        g = _dot(x, wg_ref[0])
        u = _dot(x, wu_ref[0])
        acc_ref[...] += _dot((_silu(g) * u).astype(BF), wd_ref[0])

    @pl.when(f == pl.num_programs(1) - 1)
    def _():
        o_ref[...] = acc_ref[...]


def _swiglu(x, w_gu, w_d, block_e, n_valid, tm, tf):
    rows = x.shape[0]
    F = w_d.shape[1]
    nf = F // tf
    assert rows % tm == 0 and F % tf == 0
    grid_spec = pltpu.PrefetchScalarGridSpec(
        num_scalar_prefetch=2,
        grid=(rows // tm, nf),
        in_specs=[pl.BlockSpec((tm, D), lambda i, f, be, nv: (i, 0)),
                  pl.BlockSpec((1, D, tf), lambda i, f, be, nv: (be[i], 0, f)),
                  pl.BlockSpec((1, D, tf), lambda i, f, be, nv: (be[i], 0, nf + f)),
                  pl.BlockSpec((1, tf, D), lambda i, f, be, nv: (be[i], f, 0))],
        out_specs=pl.BlockSpec((tm, D), lambda i, f, be, nv: (i, 0)),
        scratch_shapes=[pltpu.VMEM((tm, D), F32)],
    )
    return pl.pallas_call(
        _swiglu_kernel,
        grid_spec=grid_spec,
        out_shape=jax.ShapeDtypeStruct((rows, D), F32),
        compiler_params=_params(2),
    )(block_e, n_valid, x, w_gu, w_gu, w_d)


def _gla_in_kernel(x1_ref, f_ref, g2_ref, ng3_ref, ng0_ref, sc_ref, sh_ref, wqkv_ref, wg1_ref, wg2_ref, bg_ref,
                   wr_ref, br_ref, x2_ref, qkv_ref, g_ref, r_ref):
    bb, tl, _ = x1_ref.shape
    tm = bb * tl
    x2 = x1_ref[...] + g2_ref[...] * _rms(f_ref[...], ng3_ref[...])
    x2_ref[...] = x2
    h = (_rms(x2, ng0_ref[...]) * (1.0 + sc_ref[...]) + sh_ref[...]).reshape(tm, D).astype(BF)
    qkv = _dot(h, wqkv_ref[...])
    scale = DK ** -0.5
    qkv_ref[:, :, :KD] = (qkv[:, :KD] * scale).astype(BF).reshape(bb, tl, KD)
    qkv_ref[:, :, KD:] = qkv[:, KD:].astype(BF).reshape(bb, tl, 3 * KD)
    low = _dot(h, wg1_ref[...]).astype(BF)
    a = _dot(low, wg2_ref[...]) + bg_ref[...]
    log_sig = jnp.minimum(a, 0.0) - jnp.log(1.0 + jnp.exp(-jnp.abs(a)))
    g_ref[...] = (log_sig / GLA_GATE_NORM).reshape(bb, tl, 2 * KD)
    r_ref[...] = _silu(_dot(h, wr_ref[...]) + br_ref[...]).astype(BF).reshape(bb, tl, D)


def _gla_in(x1, f, g2, ng3, ng0, sc, sh, qkv_w, gk_w1, gk_w2, gk_b, r_w, r_b):
    B, L, _ = x1.shape
    r = _Rows(B, L)
    w1 = jnp.zeros((D, LANES), F32).at[:, :GLA_RANK].set(gk_w1[0]).at[:, GLA_RANK:2 * GLA_RANK].set(gk_w1[1])
    w2 = jnp.zeros((LANES, 2 * KD), F32).at[:GLA_RANK, :KD].set(gk_w2[0]).at[GLA_RANK:2 * GLA_RANK, KD:].set(gk_w2[1])
    return pl.pallas_call(
        _gla_in_kernel,
        grid=(r.n,),
        in_specs=[r.act(D), r.act(D), r.mod(g2), _const((1, 1, D)), _const((1, 1, D)), r.mod(sc), r.mod(sh),
                  _const((D, 2 * KD + D)), _const((D, LANES)), _const((LANES, 2 * KD)), _const((1, 2 * KD)),
                  _const((D, D)), _const((1, D))],
        out_specs=[r.act(D), r.act(2 * KD + D), r.act(2 * KD), r.act(D)],
        out_shape=[jax.ShapeDtypeStruct((B, L, D), F32), jax.ShapeDtypeStruct((B, L, 2 * KD + D), BF),
                   jax.ShapeDtypeStruct((B, L, 2 * KD), F32), jax.ShapeDtypeStruct((B, L, D), BF)],
        compiler_params=_params(1),
    )(x1, f.reshape(B, L, D), g2, ng3.reshape(1, 1, D), ng0.reshape(1, 1, D), sc, sh, qkv_w.astype(BF),
      w1.astype(BF), w2.astype(BF), gk_b.reshape(1, 2 * KD), r_w.astype(BF), r_b.reshape(1, D))


def _gla_scan_kernel(*refs, zero_init):
    if zero_init:
        qf_ref, qb_ref, gf_ref, gb_ref, of_ref, ob_ref, sn_ref, st_ref = refs
    else:
        qf_ref, qb_ref, gf_ref, gb_ref, s0_ref, of_ref, ob_ref, sn_ref, st_ref = refs
    j = pl.program_id(1)
    tl = qf_ref.shape[1]
    n_chunk = tl // CHUNK

    @pl.when(j == 0)
    def _():
        for d in range(2):
            for h in range(H):
                if zero_init:
                    st_ref[d, h] = jnp.zeros((DV, DK), F32)
                else:
                    st_ref[d, h] = s0_ref[0, 0, d, h].T

    ri = lax.broadcasted_iota(jnp.int32, (CHUNK, CHUNK), 0)
    ci = lax.broadcasted_iota(jnp.int32, (CHUNK, CHUNK), 1)
    masks = (ci <= ri, ci >= ri)

    def chunk(d, q_ref, g_ref, o_ref, row0):
        rows = pl.ds(row0, CHUNK)
        mask = masks[d]
        tri = jnp.where(mask, 1.0, 0.0).astype(BF)
        g = g_ref[0, rows, :]
        g_hi = g.astype(BF)
        g_lo = (g - g_hi.astype(F32)).astype(BF)
        b = _dot(tri, g_hi) + _dot(tri, g_lo)
        b_last = b[CHUNK - 1:CHUNK, :] if d == 0 else b[0:1, :]
        e_pos = jnp.exp(b)
        e_neg = jnp.exp(-b)
        e_rem = jnp.exp(b_last - b)
        decay = jnp.exp(b_last)
        for h in range(H):
            ks = slice(h * DK, (h + 1) * DK)
            q = q_ref[0, rows, h * DK:(h + 1) * DK].astype(F32)
            k = q_ref[0, rows, KD + h * DK:KD + (h + 1) * DK].astype(F32)
            v = q_ref[0, rows, 2 * KD + h * DV:2 * KD + (h + 1) * DV]
            qd = (q * e_pos[:, ks]).astype(BF)
            kd = (k * e_neg[:, ks]).astype(BF)
            k_state = (k * e_rem[:, ks]).astype(BF)
            att = jnp.where(mask, _dot_nt(qd, kd), 0.0).astype(BF)
            s_t = st_ref[d, h]
            o = _dot(att, v) + _dot_nt(qd, s_t.astype(BF))
            o_ref[0, rows, h * DV:(h + 1) * DV] = o.astype(o_ref.dtype)
            st_ref[d, h] = s_t * decay[:, ks] + _dot_tn(v, k_state)

    def body(c, carry):
        chunk(0, qf_ref, gf_ref, of_ref, pl.multiple_of(c * CHUNK, CHUNK))
        chunk(1, qb_ref, gb_ref, ob_ref, pl.multiple_of((n_chunk - 1 - c) * CHUNK, CHUNK))
        return carry

    lax.fori_loop(0, n_chunk, body, 0)

    @pl.when(j == pl.num_programs(1) - 1)
    def _():
        for d in range(2):
            for h in range(H):
                sn_ref[0, 0, d, h] = st_ref[d, h].T


def _gla_scan(qkv, g, s0):
    B, L, _ = qkv.shape
    tl = min(L, ROW_TILE)
    nl = L // tl
    zero_init = s0 is None
    wq = 2 * KD + D
    in_specs = [pl.BlockSpec((1, tl, wq), lambda b, j: (b, j, 0)),
                pl.BlockSpec((1, tl, wq), lambda b, j: (b, nl - 1 - j, 0)),
                pl.BlockSpec((1, tl, KD), lambda b, j: (b, j, 0)),
                pl.BlockSpec((1, tl, KD), lambda b, j: (b, nl - 1 - j, 1))]
    args = [qkv, qkv, g, g]
    st_spec = pl.BlockSpec((1, 1, 2, H, DK, DV), lambda b, j: (b, 0, 0, 0, 0, 0))
    if not zero_init:
        in_specs.append(st_spec)
        args.append(s0)
    return pl.pallas_call(
        functools.partial(_gla_scan_kernel, zero_init=zero_init),
        grid=(B, nl),
        in_specs=in_specs,
        out_specs=[pl.BlockSpec((1, tl, D), lambda b, j: (b, j, 0)),
                   pl.BlockSpec((1, tl, D), lambda b, j: (b, nl - 1 - j, 0)), st_spec],
        out_shape=[jax.ShapeDtypeStruct((B, L, D), BF), jax.ShapeDtypeStruct((B, L, D), BF),
                   jax.ShapeDtypeStruct((B, 1, 2, H, DK, DV), F32)],
        scratch_shapes=[pltpu.VMEM((2, H, DV, DK), F32)],
        compiler_params=_params(2),
    )(*args)


def _gla_out_kernel(of_ref, ob_ref, r_ref, on_ref, w_ref, x_ref, g1_ref, ng1_ref, ng2_ref, sc_ref, sh_ref, rw_ref,
                    x3_ref, h_ref, route_ref, cnt_ref, carry_ref):
    bb, tl, _ = x_ref.shape
    tm = bb * tl

    @pl.when(pl.program_id(0) == 0)
    def _():
        carry_ref[...] = jnp.zeros_like(carry_ref)

    o = of_ref[...].astype(F32) + ob_ref[...].astype(F32)
    heads = []
    for h in range(H):
        oh = o[:, :, h * DV:(h + 1) * DV]
        heads.append(oh * lax.rsqrt(jnp.mean(oh * oh, axis=-1, keepdims=True) + RMS_EPS))
    o = jnp.concatenate(heads, -1) * on_ref[...] * r_ref[...].astype(F32)
    m = _dot(o.reshape(tm, D).astype(BF), w_ref[...])
    x3 = x_ref[...] + g1_ref[...] * _rms(m, ng1_ref[0]).reshape(bb, tl, D)
    x3_ref[...] = x3
    h2 = _rms(x3, ng2_ref[...]) * (1.0 + sc_ref[...]) + sh_ref[...]
    h_ref[...] = h2
    logits = _dot(h2.reshape(tm, D).astype(BF), rw_ref[...])
    lane_i = lax.broadcasted_iota(jnp.int32, (tm, LANES), 1)
    lane = lane_i.astype(F32)
    logits = jnp.where(lane_i < N_EXPERTS, logits, -jnp.inf)
    m0 = jnp.max(logits, axis=-1, keepdims=True)
    i0 = jnp.min(jnp.where(logits == m0, lane, float(LANES)), axis=-1, keepdims=True)
    rest = jnp.where(lane == i0, -jnp.inf, logits)
    m1 = jnp.max(rest, axis=-1, keepdims=True)
    i1 = jnp.min(jnp.where(rest == m1, lane, float(LANES)), axis=-1, keepdims=True)
    e = jnp.exp(m1 - m0)
    w0 = 1.0 / (1.0 + e)
    w1 = e * w0
    onehot = jnp.where((lane == i0) | (lane == i1), 1.0, 0.0)
    ri = lax.broadcasted_iota(jnp.int32, (tm, tm), 0)
    ci = lax.broadcasted_iota(jnp.int32, (tm, tm), 1)
    before = jnp.where(ci < ri, 1.0, 0.0).astype(BF)
    carry = carry_ref[0:1, :]
    prior = _dot(before, onehot.astype(BF)) + carry
    rank0 = jnp.sum(jnp.where(lane == i0, prior, 0.0), axis=-1, keepdims=True)
    rank1 = jnp.sum(jnp.where(lane == i1, prior, 0.0), axis=-1, keepdims=True)
    total = carry + jnp.sum(onehot, axis=0, keepdims=True)
    carry_ref[...] = jnp.broadcast_to(total, carry_ref.shape)
    cnt_ref[...] = jnp.broadcast_to(total, cnt_ref.shape)
    fields = (w0, w1, i0, i1, rank0, rank1)
    route = jnp.zeros((tm, LANES), F32)
    for n, val in enumerate(fields):
        route = jnp.where(lane_i == n, val, route)
    route_ref[...] = route


def _gla_out(o_f, o_b, r_gate, onorm_g, w, x, g1, ng1, ng2, sc, sh, router_w):
    B, L, _ = x.shape
    r = _Rows(B, L)
    rw = jnp.pad(router_w, ((0, 0), (0, LANES - N_EXPERTS))).astype(BF)
    tm = r.tm
    return pl.pallas_call(
        _gla_out_kernel,
        grid=(r.n,),
        in_specs=[r.act(D), r.act(D), r.act(D), _const((1, 1, D)), _const((D, D)), r.act(D), r.mod(g1),
                  _const((1, 1, D)), _const((1, 1, D)), r.mod(sc), r.mod(sh), _const((D, LANES))],
        out_specs=[r.act(D), r.act(D), pl.BlockSpec((tm, LANES), lambda i: (i, 0)), _const((8, LANES))],
        out_shape=[jax.ShapeDtypeStruct((B, L, D), F32), jax.ShapeDtypeStruct((B, L, D), F32),
                   jax.ShapeDtypeStruct((B * L, LANES), F32), jax.ShapeDtypeStruct((8, LANES), F32)],
        scratch_shapes=[pltpu.VMEM((8, LANES), F32)],
        compiler_params=_params(1),
    )(o_f, o_b, r_gate, jnp.tile(onorm_g, H).reshape(1, 1, D), w.astype(BF), x, g1, ng1.reshape(1, 1, D),
      ng2.reshape(1, 1, D), sc, sh, rw)


def _dispatch_kernel(dest_ref, h_hbm, xs_in_hbm, xs_hbm, sem, *, tm):
    del xs_in_hbm
    i = pl.program_id(0)

    def issue(r, c):
        t = i * tm + r
        for j in range(2):
            pltpu.make_async_copy(h_hbm.at[pl.ds(t, 1)], xs_hbm.at[pl.ds(dest_ref[2 * r + j], 1)], sem).start()
        return c

    lax.fori_loop(0, tm, issue, 0)

    def drain(r, c):
        pltpu.make_async_copy(h_hbm.at[pl.ds(0, 1)], xs_hbm.at[pl.ds(0, 1)], sem).wait()
        return c

    lax.fori_loop(0, 2 * tm, drain, 0)


def _dispatch(h2, dest, n_rows, tm=ROW_TILE):
    T = h2.shape[0]
    return pl.pallas_call(
        functools.partial(_dispatch_kernel, tm=tm),
        grid=(T // tm,),
        in_specs=[pl.BlockSpec((2 * tm,), lambda i: (i,), memory_space=pltpu.SMEM),
                  pl.BlockSpec(memory_space=pl.ANY), pl.BlockSpec(memory_space=pl.ANY)],
        out_specs=pl.BlockSpec(memory_space=pl.ANY),
        out_shape=jax.ShapeDtypeStruct((n_rows, D), F32),
        scratch_shapes=[pltpu.SemaphoreType.DMA(())],
        input_output_aliases={2: 0},
        compiler_params=_params(1),
    )(dest, h2, jnp.zeros((n_rows, D), F32))


def _combine_kernel(dest_ref, ys_hbm, route_ref, x_ref, g2_ref, ng_ref, o_ref, buf, sem):
    bb, tl, _ = x_ref.shape
    tm = bb * tl

    def issue(r, c):
        for j in range(2):
            pltpu.make_async_copy(ys_hbm.at[pl.ds(dest_ref[2 * r + j], 1)], buf.at[j, pl.ds(r, 1)], sem).start()
        return c

    lax.fori_loop(0, tm, issue, 0)

    def drain(r, c):
        pltpu.make_async_copy(ys_hbm.at[pl.ds(0, 1)], buf.at[0, pl.ds(0, 1)], sem).wait()
        return c

    lax.fori_loop(0, 2 * tm, drain, 0)
    route = route_ref[...]
    f = route[:, 0:1] * buf[0] + route[:, 1:2] * buf[1]
    o_ref[...] = x_ref[...] + g2_ref[...] * _rms(f, ng_ref[0]).reshape(bb, tl, D)


def _combine(ys, dest, route, x, g2, ng):
    B, L, _ = x.shape
    r = _Rows(B, L)
    tm = r.tm
    return pl.pallas_call(
        _combine_kernel,
        grid=(r.n,),
        in_specs=[pl.BlockSpec((2 * tm,), lambda i: (i,), memory_space=pltpu.SMEM),
                  pl.BlockSpec(memory_space=pl.ANY), pl.BlockSpec((tm, LANES), lambda i: (i, 0)),
                  r.act(D), r.mod(g2), _const((1, 1, D))],
        out_specs=r.act(D),
        out_shape=jax.ShapeDtypeStruct((B, L, D), F32),
        scratch_shapes=[pltpu.VMEM((2, tm, D), F32), pltpu.SemaphoreType.DMA(())],
        compiler_params=_params(1),
    )(dest, ys, route, x, g2, ng.reshape(1, 1, D))


def _moe(h2, route, counts, x3, g2, ng, w_gu, w_d, tmoe):
    B, L, _ = h2.shape
    T = B * L
    idx = route[:, 2:4].astype(jnp.int32)
    rank = route[:, 4:6].astype(jnp.int32)
    cnt = counts[0, :N_EXPERTS].astype(jnp.int32)
    p_cnt = (cnt + tmoe - 1) // tmoe * tmoe
    p_end = jnp.cumsum(p_cnt)
    p_start = p_end - p_cnt
    sel = idx[:, :, None] == jnp.arange(N_EXPERTS, dtype=jnp.int32)[None, None, :]
    dest = (jnp.sum(jnp.where(sel, p_start[None, None, :], 0), axis=-1) + rank).reshape(-1)
    n_rows = 2 * T + N_EXPERTS * tmoe
    n_blocks = n_rows // tmoe
    starts = jnp.arange(n_blocks, dtype=jnp.int32) * tmoe
    block_e = jnp.minimum(jnp.sum(starts[:, None] >= p_end[None, :], axis=-1), N_EXPERTS - 1).astype(jnp.int32)
    n_valid = (p_end[-1:] // tmoe).astype(jnp.int32)
    xs = _dispatch(h2.reshape(T, D), dest, n_rows)
    ys = _swiglu(xs, w_gu, w_d, block_e, n_valid, tmoe, 512)
    return _combine(ys, dest, route, x3, g2, ng)


def _trunk(x, mods, n_row, s0, p):
    B, L, _ = x.shape
    T = B * L
    ng = p['norm_g']
    sh1, sc1, g1, sh2, sc2, g2 = mods[0]
    v, x0 = _hy_in(x, ng[0, 0], sc1, sh1, p['hy_in_w'][0], p['hy_in_b'][0], p['hy_sc_w'][0], p['hy_sc_b'][0], n_row)
    kern = _hyena_kernel_taps(L, p['hy_f_w1'][0], p['hy_f_b1'][0], p['hy_f_w2'][0], p['hy_f_b2'][0],
                              p['hy_f_freq'][0], p['hy_f_w3'][0])
    if L <= 512:
        vx = _fftconv_short(v, x0, kern, p['hy_skip'][0])
    else:
        vx = _fftconv_long(v, x0, kern, p['hy_skip'][0])
    x1, h2 = _hy_out(vx, p['hy_out_w'][0], p['hy_out_b'][0], x, g1, ng[0, 1], ng[0, 2], sc2, sh2)
    tm = 1024
    ones = jnp.zeros((T // tm,), jnp.int32)
    f = _swiglu(h2.reshape(T, D), p['ffn_wgu'], p['ffn_wd'], ones, jnp.full((1,), T // tm, jnp.int32), tm, 256)
    g2_0, ng3_0 = g2, ng[0, 3]
    sh1, sc1, g1, sh2, sc2, g2 = mods[1]
    x2, qkv, gate, r_gate = _gla_in(x1, f, g2_0, ng3_0, ng[1, 0], sc1, sh1, p['gla_qkv_w'][0], p['gla_gk_w1'][0],
                                    p['gla_gk_w2'][0], p['gla_gk_b'][0], p['gla_r_w'][0], p['gla_r_b'][0])
    o_f, o_b, s_new = _gla_scan(qkv, gate, s0)
    x3, h2, route, counts = _gla_out(o_f, o_b, r_gate, p['gla_onorm_g'][0], p['gla_out_w'][0], x2, g1, ng[1, 1],
                                     ng[1, 2], sc2, sh2, p['moe_router'][0])
    x4 = _moe(h2, route, counts, x3, g2, ng[1, 3], p['moe_wgu'], p['moe_wd'], 1024 if T >= 16384 else 512)
    return x4, s_new


def kernel(x_prompt, x_sample, state_gla, c, c_ctx, ada_w, ada_b, norm_g, hy_in_w, hy_in_b, hy_sc_w, hy_sc_b, hy_f_w1, hy_f_b1, hy_f_w2, hy_f_b2, hy_f_freq, hy_f_w3, hy_skip, hy_out_w, hy_out_b, gla_qkv_w, gla_gk_w1, gla_gk_w2, gla_gk_b, gla_r_w, gla_r_b, gla_onorm_g, gla_out_w, ffn_wgu, ffn_wd, moe_router, moe_wgu, moe_wd):
    p = dict(norm_g=norm_g, hy_in_w=hy_in_w, hy_in_b=hy_in_b, hy_sc_w=hy_sc_w, hy_sc_b=hy_sc_b, hy_f_w1=hy_f_w1,
             hy_f_b1=hy_f_b1, hy_f_w2=hy_f_w2, hy_f_b2=hy_f_b2, hy_f_freq=hy_f_freq, hy_f_w3=hy_f_w3,
             hy_skip=hy_skip, hy_out_w=hy_out_w, hy_out_b=hy_out_b, gla_qkv_w=gla_qkv_w, gla_gk_w1=gla_gk_w1,
             gla_gk_w2=gla_gk_w2, gla_gk_b=gla_gk_b, gla_r_w=gla_r_w, gla_r_b=gla_r_b, gla_onorm_g=gla_onorm_g,
             gla_out_w=gla_out_w, ffn_wgu=ffn_wgu.astype(BF), ffn_wd=ffn_wd.astype(BF),
             moe_router=moe_router, moe_wgu=moe_wgu[0].astype(BF), moe_wd=moe_wd[0].astype(BF))
    n_dec = c.shape[0]
    cond = jnp.concatenate([c_ctx[None, :], c, jnp.zeros((16 - 1 - n_dec, D), F32)], axis=0)
    mod = _ada(cond, ada_w, ada_b)
    mods_ctx = [[m[:, None, :] for m in jnp.split(mod[l, 0:1], 6, axis=-1)] for l in range(DEPTH)]
    mods_dec = [[m[:, None, :] for m in jnp.split(mod[l, 1:1 + n_dec], 6, axis=-1)] for l in range(DEPTH)]
    y_prompt, state_new = _trunk(x_prompt, mods_ctx, x_prompt.shape[1], None, p)
    grid_w = 64
    y_sample, _ = _trunk(x_sample, mods_dec, grid_w, state_gla, p)
    return y_prompt, y_sample, state_new
```

```python
import functools
import math

import jax
import jax.numpy as jnp
from jax import lax
from jax.experimental import pallas as pl
from jax.experimental.pallas import tpu as pltpu

F32 = jnp.float32
BF = jnp.bfloat16

D = 1024
RMS_EPS = 1e-6
DEPTH = 2
HY_SHORT = 3
HY_EMB = 33
HY_BANDS = (HY_EMB - 1) // 2
HY_FFN = 64
HY_MAX_DECAY = math.log(1e-2) / 0.3
HY_MIN_DECAY = math.log(1e-2) / 1.5
H = 4
DK = 128
DV = 256
KD = H * DK
GLA_RANK = 16
GLA_GATE_NORM = 16.0
CHUNK = 64
D_FF = 11 * D // 4
N_EXPERTS = 8
D_FF_EXPERT = 7 * D // 2

LANES = 128
SUBLANES = 8
VMEM_LIMIT_BYTES = 56 * 1024 * 1024
ROW_TILE = 512
FFT_S2 = 128


def _params(n_axes):
    return pltpu.CompilerParams(dimension_semantics=("arbitrary",) * n_axes,
                                vmem_limit_bytes=VMEM_LIMIT_BYTES)


def _dot(a, b):
    return jnp.dot(a, b, preferred_element_type=F32)


def _dot_nt(a, b):
    return lax.dot_general(a, b, (((1,), (1,)), ((), ())), preferred_element_type=F32)


def _dot_tn(a, b):
    return lax.dot_general(a, b, (((0,), (0,)), ((), ())), preferred_element_type=F32)


def _rms(x, g):
    return x * lax.rsqrt(jnp.mean(x * x, axis=-1, keepdims=True) + RMS_EPS) * g


def _silu(x):
    return x * (1.0 / (1.0 + jnp.exp(-x)))


class _Rows:
    def __init__(self, B, L, tm=ROW_TILE):
        self.B, self.L = B, L
        if L >= tm:
            self.bb, self.tl = 1, tm
        else:
            self.bb, self.tl = tm // L, L
        assert L % self.tl == 0 and B % self.bb == 0
        self.nl = L // self.tl
        self.n = (B // self.bb) * self.nl
        self.tm = self.bb * self.tl

    def act(self, width):
        nl = self.nl
        return pl.BlockSpec((self.bb, self.tl, width), lambda i: (i // nl, i % nl, 0))

    def mod(self, m):
        nl = self.nl
        if m.shape[0] == 1:
            return pl.BlockSpec((1, 1, D), lambda i: (0, 0, 0))
        return pl.BlockSpec((self.bb, 1, D), lambda i: (i // nl, 0, 0))


def _const(shape):
    nd = len(shape)
    return pl.BlockSpec(shape, lambda *_: (0,) * nd)


def _ada_kernel(c_ref, w_ref, b_ref, o_ref):
    cs = _silu(c_ref[...])
    o_ref[0] = _dot(cs.astype(BF), w_ref[0].astype(BF)) + b_ref[0]


def _ada(cond, ada_w, ada_b):
    R = cond.shape[0]
    tn = 1536
    return pl.pallas_call(
        _ada_kernel,
        grid=(DEPTH, 6 * D // tn),
        in_specs=[pl.BlockSpec((R, D), lambda l, n: (0, 0)),
                  pl.BlockSpec((1, D, tn), lambda l, n: (l, 0, n)),
                  pl.BlockSpec((1, 1, tn), lambda l, n: (l, 0, n))],
        out_specs=pl.BlockSpec((1, R, tn), lambda l, n: (l, 0, n)),
        out_shape=jax.ShapeDtypeStruct((DEPTH, R, 6 * D), F32),
        compiler_params=_params(2),
    )(cond, ada_w, ada_b.reshape(DEPTH, 1, 6 * D))


def _hy_in_kernel(x_ref, ng_ref, sc_ref, sh_ref, w_ref, b_ref, cw_ref, cb_ref, v_ref, x0_ref, *, n_row):
    x = x_ref[...]
    bb, tl, _ = x.shape
    tm = bb * tl
    h = _rms(x, ng_ref[...]) * (1.0 + sc_ref[...]) + sh_ref[...]
    hb = h.reshape(tm, D).astype(BF)
    pos = lax.broadcasted_iota(jnp.int32, (tm, D), 0) & (n_row - 1)
    first = pos == 0
    last = pos == n_row - 1
    parts = []
    for j in range(3):
        cols = slice(j * D, (j + 1) * D)
        u = _dot(hb, w_ref[:, cols]) + b_ref[:, cols]
        up = jnp.where(first, 0.0, pltpu.roll(u, 1, 0))
        dn = jnp.where(last, 0.0, pltpu.roll(u, tm - 1, 0))
        parts.append(cb_ref[:, cols] + up * cw_ref[0:1, cols] + u * cw_ref[1:2, cols] + dn * cw_ref[2:3, cols])
    x0, x1, v = parts
    v_ref[...] = (v * x1).astype(BF).reshape(bb, tl, D)
    x0_ref[...] = x0.astype(BF).reshape(bb, tl, D)


def _hy_in(x, ng, sc, sh, w, b, cw, cb, n_row):
    B, L, _ = x.shape
    assert n_row & (n_row - 1) == 0
    r = _Rows(B, L)
    assert r.tl % n_row == 0
    return pl.pallas_call(
        functools.partial(_hy_in_kernel, n_row=n_row),
        grid=(r.n,),
        in_specs=[r.act(D), _const((1, 1, D)), r.mod(sc), r.mod(sh),
                  _const((D, 3 * D)), _const((1, 3 * D)), _const((HY_SHORT, 3 * D)), _const((1, 3 * D))],
        out_specs=[r.act(D), r.act(D)],
        out_shape=[jax.ShapeDtypeStruct((B, L, D), BF)] * 2,
        compiler_params=_params(1),
    )(x, ng.reshape(1, 1, D), sc, sh, w.astype(BF), b.reshape(1, 3 * D), cw, cb.reshape(1, 3 * D))


def _filter_kernel(z_ref, w1_ref, b1_ref, w2_ref, b2_ref, fr_ref, w3_ref, dl_ref, o_ref, *, L):
    tr = z_ref.shape[0]
    hp = lax.Precision.HIGHEST
    z = z_ref[...]
    h = jnp.sin(fr_ref[0:1, :] * (jnp.dot(z, w1_ref[...], precision=hp, preferred_element_type=F32) + b1_ref[...]))
    h = jnp.sin(fr_ref[1:2, :] * (jnp.dot(h, w2_ref[...], precision=hp, preferred_element_type=F32) + b2_ref[...]))
    hw = _dot(h.astype(BF), w3_ref[...])
    n = pl.program_id(0) * tr + lax.broadcasted_iota(jnp.int32, (tr, D), 0)
    taps = jnp.where(n < L, hw[:, :D], hw[:, D:]) * jnp.exp(-z[:, 0:1] * dl_ref[...])
    o_ref[...] = jnp.where(n == L, 0.0, taps)


def _hyena_kernel_taps(L, w1, b1, w2, b2, freq, w3):
    n = jnp.arange(2 * L, dtype=jnp.int32)
    pos = jnp.where(n < L, n, 2 * L - n) % L
    t = jnp.linspace(0.0, 1.0, L, dtype=F32)[pos][:, None]
    w = (2.0 * math.pi * pos.astype(F32) / L)[:, None]
    f = jnp.linspace(1e-4, HY_BANDS - 1, HY_BANDS, dtype=F32)[None, :]
    z = jnp.concatenate([t, jnp.cos(f * w), -jnp.sin(f * w)], axis=-1)
    z = jnp.pad(z, ((0, 0), (0, LANES - HY_EMB)))
    pad = LANES - HY_FFN
    w1p = jnp.pad(w1, ((0, LANES - HY_EMB), (0, pad)))
    w2p = jnp.pad(w2, ((0, pad), (0, pad)))
    w3p = jnp.pad(w3, ((0, pad), (0, 0))).astype(BF)
    b1p = jnp.pad(b1, (0, pad)).reshape(1, LANES)
    b2p = jnp.pad(b2, (0, pad)).reshape(1, LANES)
    frp = jnp.pad(freq, ((0, 0), (0, pad)))
    deltas = jnp.abs(jnp.linspace(HY_MIN_DECAY, HY_MAX_DECAY, D, dtype=F32)).reshape(1, D)
    tr = 512
    return pl.pallas_call(
        functools.partial(_filter_kernel, L=L),
        grid=(2 * L // tr,),
        in_specs=[pl.BlockSpec((tr, LANES), lambda i: (i, 0)), _const((LANES, LANES)), _const((1, LANES)),
                  _const((LANES, LANES)), _const((1, LANES)), _const((2, LANES)), _const((LANES, 2 * D)),
                  _const((1, D))],
        out_specs=pl.BlockSpec((tr, D), lambda i: (i, 0)),
        out_shape=jax.ShapeDtypeStruct((2 * L, D), F32),
        compiler_params=_params(1),
    )(z, w1p, b1p, w2p, b2p, frp, w3p, deltas)


def _cis(rows, cols, n, sign, scale=1.0):
    ph = (rows[:, None] * cols[None, :]) % n
    ang = ph.astype(F32) * (2.0 * math.pi / n)
    return jnp.cos(ang) * scale, jnp.sin(ang) * (sign * scale)


def _cplx_block(cr, ci):
    return jnp.concatenate([jnp.concatenate([cr, -ci], 1), jnp.concatenate([ci, cr], 1)], 0)


def _fft_short_kernel(v_ref, x0_ref, kern_ref, mk_ref, mf_ref, mi_ref, sk_ref, o_ref, ks_ref, *, L):
    n2 = 2 * L

    @pl.when(pl.program_id(0) == 0)
    def _():
        ks_ref[...] = _dot(mk_ref[...], kern_ref[...].astype(BF))

    z = jnp.concatenate([v_ref[0], v_ref[1]], 0)
    u = _dot(mf_ref[...], z)
    ur, ui = u[:n2], u[n2:]
    kr, ki = ks_ref[:n2, :], ks_ref[n2:, :]
    y = jnp.concatenate([ur * kr - ui * ki, ur * ki + ui * kr], 0).astype(BF)
    t = _dot(mi_ref[...], y)
    for j in range(2):
        conv = t[j * L:(j + 1) * L]
        o_ref[j] = ((conv + v_ref[j].astype(F32) * sk_ref[...]) * x0_ref[j].astype(F32)).astype(BF)


def _fftconv_short(v, x0, kern, skip):
    B, L, _ = v.shape
    n2 = 2 * L
    k = jnp.arange(n2, dtype=jnp.int32)
    s = jnp.arange(L, dtype=jnp.int32)
    fr, fi = _cis(k, s, n2, -1.0)
    mf = _cplx_block(fr, fi).astype(BF)
    kr, ki = _cis(k, k, n2, -1.0)
    mk = jnp.concatenate([kr, ki], 0).astype(BF)
    ir, ii = _cis(s, k, n2, 1.0, 1.0 / n2)
    mi = _cplx_block(ir, ii).astype(BF)
    pair = lambda p: (p, 0, 0)
    return pl.pallas_call(
        functools.partial(_fft_short_kernel, L=L),
        grid=(B // 2,),
        in_specs=[pl.BlockSpec((2, L, D), pair), pl.BlockSpec((2, L, D), pair), _const((n2, D)),
                  _const((2 * n2, n2)), _const((2 * n2, 2 * L)), _const((2 * L, 2 * n2)), _const((1, D))],
        out_specs=pl.BlockSpec((2, L, D), pair),
        out_shape=jax.ShapeDtypeStruct((B, L, D), BF),
        scratch_shapes=[pltpu.VMEM((2 * n2, D), F32)],
        compiler_params=_params(1),
    )(v, x0, kern, mk, mf, mi, skip.reshape(1, D))


def _lmul_kernel(m_ref, x_ref, o_ref):
    o_ref[0] = _dot(m_ref[...], x_ref[0].astype(BF)).astype(o_ref.dtype)


def _lmul(m, x, cb=8192):
    G, K, NC = x.shape
    R = m.shape[0]
    return pl.pallas_call(
        _lmul_kernel,
        grid=(G, NC // cb),
        in_specs=[pl.BlockSpec((R, K), lambda g, c: (0, 0)), pl.BlockSpec((1, K, cb), lambda g, c: (g, 0, c))],
        out_specs=pl.BlockSpec((1, R, cb), lambda g, c: (g, 0, c)),
        out_shape=jax.ShapeDtypeStruct((G, R, NC), BF),
        compiler_params=_params(2),
    )(m, x)


def _fft_mid_kernel(a_ref, af_ref, mf_ref, mi_ref, o_ref, ks_ref):
    s2 = FFT_S2
    mf = mf_ref[0]

    @pl.when(pl.program_id(1) == 0)
    def _():
        ks_ref[...] = _dot(mf, jnp.concatenate([af_ref[0, 0], af_ref[1, 0]], 0))

    u = _dot(mf, jnp.concatenate([a_ref[0, 0, 0], a_ref[0, 1, 0]], 0))
    ur, ui = u[:s2], u[s2:]
    kr, ki = ks_ref[:s2, :], ks_ref[s2:, :]
    y = jnp.concatenate([ur * kr - ui * ki, ur * ki + ui * kr], 0).astype(BF)
    z = _dot(mi_ref[0], y).astype(BF)
    o_ref[0, 0, 0] = z[:s2]
    o_ref[0, 1, 0] = z[s2:]


def _lmul_out_kernel(m_ref, z_ref, v_ref, x0_ref, sk_ref, o_ref):
    conv = _dot(m_ref[...], z_ref[0])
    o_ref[0] = ((conv + v_ref[0].astype(F32) * sk_ref[...]) * x0_ref[0].astype(F32)).astype(BF)


def _fftconv_long(v, x0, kern, skip, cb=8192):
    B, L, _ = v.shape
    n2 = 2 * L
    s2 = FFT_S2
    s1 = n2 // s2
    s1h = s1 // 2
    G = B // 2
    nc = s2 * D
    i1 = jnp.arange(s1, dtype=jnp.int32)
    i1h = jnp.arange(s1h, dtype=jnp.int32)
    i2 = jnp.arange(s2, dtype=jnp.int32)
    cr, ci = _cis(i1, i1h, s1, -1.0)
    m1 = _cplx_block(cr, ci).astype(BF)
    cr, ci = _cis(i1, i1, s1, -1.0)
    m1f = jnp.concatenate([cr, ci], 0).astype(BF)
    kk = (i1[:, None] + s1 * i2[None, :]).reshape(-1)
    gr, gi = _cis(kk, i2, n2, -1.0)
    mf = jax.vmap(_cplx_block)(gr.reshape(s1, s2, s2), gi.reshape(s1, s2, s2)).astype(BF)
    hr, hi = _cis(i2, kk, n2, 1.0, 1.0 / n2)
    hr = hr.reshape(s2, s1, s2).transpose(1, 0, 2)
    hi = hi.reshape(s2, s1, s2).transpose(1, 0, 2)
    mi = jax.vmap(_cplx_block)(hr, hi).astype(BF)
    er, ei = _cis(i1h, i1, s1, 1.0)
    m3 = _cplx_block(er, ei).astype(BF)

    a = _lmul(m1, v.reshape(G, 2 * s1h, nc), cb)
    af = _lmul(m1f, kern.reshape(1, s1, nc), cb)
    zz = pl.pallas_call(
        _fft_mid_kernel,
        grid=(s1, G),
        in_specs=[pl.BlockSpec((1, 2, 1, s2, D), lambda k, p: (p, 0, k, 0, 0)),
                  pl.BlockSpec((2, 1, s2, D), lambda k, p: (0, k, 0, 0)),
                  pl.BlockSpec((1, 2 * s2, 2 * s2), lambda k, p: (k, 0, 0)),
                  pl.BlockSpec((1, 2 * s2, 2 * s2), lambda k, p: (k, 0, 0))],
        out_specs=pl.BlockSpec((1, 2, 1, s2, D), lambda k, p: (p, 0, k, 0, 0)),
        out_shape=jax.ShapeDtypeStruct((G, 2, s1, s2, D), BF),
        scratch_shapes=[pltpu.VMEM((2 * s2, D), F32)],
        compiler_params=_params(2),
    )(a.reshape(G, 2, s1, s2, D), af.reshape(2, s1, s2, D), mf, mi)
    blk = lambda g, c: (g, 0, c)
    out = pl.pallas_call(
        _lmul_out_kernel,
        grid=(G, nc // cb),
        in_specs=[pl.BlockSpec((2 * s1h, 2 * s1), lambda g, c: (0, 0)), pl.BlockSpec((1, 2 * s1, cb), blk),
                  pl.BlockSpec((1, 2 * s1h, cb), blk), pl.BlockSpec((1, 2 * s1h, cb), blk),
                  pl.BlockSpec((1, cb), lambda g, c: (0, 0))],
        out_specs=pl.BlockSpec((1, 2 * s1h, cb), blk),
        out_shape=jax.ShapeDtypeStruct((G, 2 * s1h, nc), BF),
        compiler_params=_params(2),
    )(m3, zz.reshape(G, 2 * s1, nc), v.reshape(G, 2 * s1h, nc), x0.reshape(G, 2 * s1h, nc),
      jnp.tile(skip.reshape(1, D), (1, cb // D)))
    return out.reshape(B, L, D)


def _hy_out_kernel(vx_ref, w_ref, b_ref, x_ref, g1_ref, ng1_ref, ng2_ref, sc_ref, sh_ref, x1_ref, h_ref):
    bb, tl, _ = x_ref.shape
    m = _dot(vx_ref[...].reshape(bb * tl, D), w_ref[...]) + b_ref[...]
    x1 = x_ref[...] + g1_ref[...] * _rms(m, ng1_ref[0]).reshape(bb, tl, D)
    x1_ref[...] = x1
    h_ref[...] = (_rms(x1, ng2_ref[...]) * (1.0 + sc_ref[...]) + sh_ref[...]).astype(h_ref.dtype)


def _hy_out(vx, w, b, x, g1, ng1, ng2, sc, sh):
    B, L, _ = x.shape
    r = _Rows(B, L)
    return pl.pallas_call(
        _hy_out_kernel,
        grid=(r.n,),
        in_specs=[r.act(D), _const((D, D)), _const((1, D)), r.act(D), r.mod(g1), _const((1, 1, D)),
                  _const((1, 1, D)), r.mod(sc), r.mod(sh)],
        out_specs=[r.act(D), r.act(D)],
        out_shape=[jax.ShapeDtypeStruct((B, L, D), F32), jax.ShapeDtypeStruct((B, L, D), BF)],
        compiler_params=_params(1),
    )(vx, w.astype(BF), b.reshape(1, D), x, g1, ng1.reshape(1, 1, D), ng2.reshape(1, 1, D), sc, sh)


def _swiglu_kernel(be_ref, nv_ref, x_ref, wg_ref, wu_ref, wd_ref, o_ref, acc_ref):
    del be_ref
    f = pl.program_id(1)

    @pl.when(f == 0)
    def _():
        acc_ref[...] = jnp.zeros_like(acc_ref)

    @pl.when(pl.program_id(0) < nv_ref[0])
    def _():
        x = x_ref[...].astype(BF)
        g = _dot(x, wg_ref[0])
        u = _dot(x, wu_ref[0])
        acc_ref[...] += _dot((_silu(g) * u).astype(BF), wd_ref[0])

    @pl.when(f == pl.num_programs(1) - 1)
    def _():
        o_ref[...] = acc_ref[...]


def _swiglu(x, w_gu, w_d, block_e, n_valid, tm, tf):
    rows = x.shape[0]
    F = w_d.shape[1]
    nf = F // tf
    assert rows % tm == 0 and F % tf == 0
    grid_spec = pltpu.PrefetchScalarGridSpec(
        num_scalar_prefetch=2,
        grid=(rows // tm, nf),
        in_specs=[pl.BlockSpec((tm, D), lambda i, f, be, nv: (i, 0)),
                  pl.BlockSpec((1, D, tf), lambda i, f, be, nv: (be[i], 0, f)),
                  pl.BlockSpec((1, D, tf), lambda i, f, be, nv: (be[i], 0, nf + f)),
                  pl.BlockSpec((1, tf, D), lambda i, f, be, nv: (be[i], f, 0))],
        out_specs=pl.BlockSpec((tm, D), lambda i, f, be, nv: (i, 0)),
        scratch_shapes=[pltpu.VMEM((tm, D), F32)],
    )
    return pl.pallas_call(
        _swiglu_kernel,
        grid_spec=grid_spec,
        out_shape=jax.ShapeDtypeStruct((rows, D), F32),
        compiler_params=_params(2),
    )(block_e, n_valid, x, w_gu, w_gu, w_d)


def _gla_in_kernel(x1_ref, f_ref, g2_ref, ng3_ref, ng0_ref, sc_ref, sh_ref, wqkv_ref, wg1_ref, wg2_ref, bg_ref,
                   wr_ref, br_ref, x2_ref, qkv_ref, g_ref, r_ref):
    bb, tl, _ = x1_ref.shape
    tm = bb * tl
    x2 = x1_ref[...] + g2_ref[...] * _rms(f_ref[...], ng3_ref[...])
    x2_ref[...] = x2
    h = (_rms(x2, ng0_ref[...]) * (1.0 + sc_ref[...]) + sh_ref[...]).reshape(tm, D).astype(BF)
    qkv = _dot(h, wqkv_ref[...])
    scale = DK ** -0.5
    qkv_ref[:, :, :KD] = (qkv[:, :KD] * scale).astype(BF).reshape(bb, tl, KD)
    qkv_ref[:, :, KD:] = qkv[:, KD:].astype(BF).reshape(bb, tl, 3 * KD)
    low = _dot(h, wg1_ref[...]).astype(BF)
    a = _dot(low, wg2_ref[...]) + bg_ref[...]
    log_sig = jnp.minimum(a, 0.0) - jnp.log(1.0 + jnp.exp(-jnp.abs(a)))
    g_ref[...] = (log_sig / GLA_GATE_NORM).reshape(bb, tl, 2 * KD)
    r_ref[...] = _silu(_dot(h, wr_ref[...]) + br_ref[...]).astype(BF).reshape(bb, tl, D)


def _gla_in(x1, f, g2, ng3, ng0, sc, sh, qkv_w, gk_w1, gk_w2, gk_b, r_w, r_b):
    B, L, _ = x1.shape
    r = _Rows(B, L)
    w1 = jnp.zeros((D, LANES), F32).at[:, :GLA_RANK].set(gk_w1[0]).at[:, GLA_RANK:2 * GLA_RANK].set(gk_w1[1])
    w2 = jnp.zeros((LANES, 2 * KD), F32).at[:GLA_RANK, :KD].set(gk_w2[0]).at[GLA_RANK:2 * GLA_RANK, KD:].set(gk_w2[1])
    return pl.pallas_call(
        _gla_in_kernel,
        grid=(r.n,),
        in_specs=[r.act(D), r.act(D), r.mod(g2), _const((1, 1, D)), _const((1, 1, D)), r.mod(sc), r.mod(sh),
                  _const((D, 2 * KD + D)), _const((D, LANES)), _const((LANES, 2 * KD)), _const((1, 2 * KD)),
                  _const((D, D)), _const((1, D))],
        out_specs=[r.act(D), r.act(2 * KD + D), r.act(2 * KD), r.act(D)],
        out_shape=[jax.ShapeDtypeStruct((B, L, D), F32), jax.ShapeDtypeStruct((B, L, 2 * KD + D), BF),
                   jax.ShapeDtypeStruct((B, L, 2 * KD), F32), jax.ShapeDtypeStruct((B, L, D), BF)],
        compiler_params=_params(1),
    )(x1, f.reshape(B, L, D), g2, ng3.reshape(1, 1, D), ng0.reshape(1, 1, D), sc, sh, qkv_w.astype(BF),
      w1.astype(BF), w2.astype(BF), gk_b.reshape(1, 2 * KD), r_w.astype(BF), r_b.reshape(1, D))


def _gla_scan_kernel(*refs, zero_init):
    if zero_init:
        qf_ref, qb_ref, gf_ref, gb_ref, of_ref, ob_ref, sn_ref, st_ref = refs
    else:
        qf_ref, qb_ref, gf_ref, gb_ref, s0_ref, of_ref, ob_ref, sn_ref, st_ref = refs
    j = pl.program_id(1)
    tl = qf_ref.shape[1]
    n_chunk = tl // CHUNK

    @pl.when(j == 0)
    def _():
        for d in range(2):
            for h in range(H):
                if zero_init:
                    st_ref[d, h] = jnp.zeros((DV, DK), F32)
                else:
                    st_ref[d, h] = s0_ref[0, 0, d, h].T

    ri = lax.broadcasted_iota(jnp.int32, (CHUNK, CHUNK), 0)
    ci = lax.broadcasted_iota(jnp.int32, (CHUNK, CHUNK), 1)
    masks = (ci <= ri, ci >= ri)

    def chunk(d, q_ref, g_ref, o_ref, row0):
        rows = pl.ds(row0, CHUNK)
        mask = masks[d]
        tri = jnp.where(mask, 1.0, 0.0).astype(BF)
        g = g_ref[0, rows, :]
        g_hi = g.astype(BF)
        g_lo = (g - g_hi.astype(F32)).astype(BF)
        b = _dot(tri, g_hi) + _dot(tri, g_lo)
        b_last = b[CHUNK - 1:CHUNK, :] if d == 0 else b[0:1, :]
        e_pos = jnp.exp(b)
        e_neg = jnp.exp(-b)
        e_rem = jnp.exp(b_last - b)
        decay = jnp.exp(b_last)
        for h in range(H):
            ks = slice(h * DK, (h + 1) * DK)
            q = q_ref[0, rows, h * DK:(h + 1) * DK].astype(F32)
            k = q_ref[0, rows, KD + h * DK:KD + (h + 1) * DK].astype(F32)
            v = q_ref[0, rows, 2 * KD + h * DV:2 * KD + (h + 1) * DV]
            qd = (q * e_pos[:, ks]).astype(BF)
            kd = (k * e_neg[:, ks]).astype(BF)
            k_state = (k * e_rem[:, ks]).astype(BF)
            att = jnp.where(mask, _dot_nt(qd, kd), 0.0).astype(BF)
            s_t = st_ref[d, h]
            o = _dot(att, v) + _dot_nt(qd, s_t.astype(BF))
            o_ref[0, rows, h * DV:(h + 1) * DV] = o.astype(o_ref.dtype)
            st_ref[d, h] = s_t * decay[:, ks] + _dot_tn(v, k_state)

    def body(c, carry):
        chunk(0, qf_ref, gf_ref, of_ref, pl.multiple_of(c * CHUNK, CHUNK))
        chunk(1, qb_ref, gb_ref, ob_ref, pl.multiple_of((n_chunk - 1 - c) * CHUNK, CHUNK))
        return carry

    lax.fori_loop(0, n_chunk, body, 0)

    @pl.when(j == pl.num_programs(1) - 1)
    def _():
        for d in range(2):
            for h in range(H):
                sn_ref[0, 0, d, h] = st_ref[d, h].T


def _gla_scan(qkv, g, s0):
    B, L, _ = qkv.shape
    tl = min(L, ROW_TILE)
    nl = L // tl
    zero_init = s0 is None
    wq = 2 * KD + D
    in_specs = [pl.BlockSpec((1, tl, wq), lambda b, j: (b, j, 0)),
                pl.BlockSpec((1, tl, wq), lambda b, j: (b, nl - 1 - j, 0)),
                pl.BlockSpec((1, tl, KD), lambda b, j: (b, j, 0)),
                pl.BlockSpec((1, tl, KD), lambda b, j: (b, nl - 1 - j, 1))]
    args = [qkv, qkv, g, g]
    st_spec = pl.BlockSpec((1, 1, 2, H, DK, DV), lambda b, j: (b, 0, 0, 0, 0, 0))
    if not zero_init:
        in_specs.append(st_spec)
        args.append(s0)
    return pl.pallas_call(
        functools.partial(_gla_scan_kernel, zero_init=zero_init),
        grid=(B, nl),
        in_specs=in_specs,
        out_specs=[pl.BlockSpec((1, tl, D), lambda b, j: (b, j, 0)),
                   pl.BlockSpec((1, tl, D), lambda b, j: (b, nl - 1 - j, 0)), st_spec],
        out_shape=[jax.ShapeDtypeStruct((B, L, D), BF), jax.ShapeDtypeStruct((B, L, D), BF),
                   jax.ShapeDtypeStruct((B, 1, 2, H, DK, DV), F32)],
        scratch_shapes=[pltpu.VMEM((2, H, DV, DK), F32)],
        compiler_params=_params(2),
    )(*args)


def _gla_out_kernel(of_ref, ob_ref, r_ref, on_ref, w_ref, x_ref, g1_ref, ng1_ref, ng2_ref, sc_ref, sh_ref, rw_ref,
                    x3_ref, h_ref, route_ref, cnt_ref):
    bb, tl, _ = x_ref.shape
    tm = bb * tl
    o = of_ref[...].astype(F32) + ob_ref[...].astype(F32)
    heads = []
    for h in range(H):
        oh = o[:, :, h * DV:(h + 1) * DV]
        heads.append(oh * lax.rsqrt(jnp.mean(oh * oh, axis=-1, keepdims=True) + RMS_EPS))
    o = jnp.concatenate(heads, -1) * on_ref[...] * r_ref[...].astype(F32)
    m = _dot(o.reshape(tm, D).astype(BF), w_ref[...])
    x3 = x_ref[...] + g1_ref[...] * _rms(m, ng1_ref[0]).reshape(bb, tl, D)
    x3_ref[...] = x3
    h2 = _rms(x3, ng2_ref[...]) * (1.0 + sc_ref[...]) + sh_ref[...]
    h_ref[...] = h2.astype(h_ref.dtype)
    logits = _dot(h2.reshape(tm, D).astype(BF), rw_ref[...])
    lane_i = lax.broadcasted_iota(jnp.int32, (tm, LANES), 1)
    lane = lane_i.astype(F32)
    logits = jnp.where(lane_i < N_EXPERTS, logits, -jnp.inf)
    m0 = jnp.max(logits, axis=-1, keepdims=True)
    i0 = jnp.min(jnp.where(logits == m0, lane, float(LANES)), axis=-1, keepdims=True)
    rest = jnp.where(lane == i0, -jnp.inf, logits)
    m1 = jnp.max(rest, axis=-1, keepdims=True)
    i1 = jnp.min(jnp.where(rest == m1, lane, float(LANES)), axis=-1, keepdims=True)
    e = jnp.exp(m1 - m0)
    w0 = 1.0 / (1.0 + e)
    w1 = e * w0
    onehot = jnp.where((lane == i0) | (lane == i1), 1.0, 0.0)
    ri = lax.broadcasted_iota(jnp.int32, (tm, tm), 0)
    ci = lax.broadcasted_iota(jnp.int32, (tm, tm), 1)
    before = jnp.where(ci < ri, 1.0, 0.0).astype(BF)
    prior = _dot(before, onehot.astype(BF))
    rank0 = jnp.sum(jnp.where(lane == i0, prior, 0.0), axis=-1, keepdims=True)
    rank1 = jnp.sum(jnp.where(lane == i1, prior, 0.0), axis=-1, keepdims=True)
    cnt_ref[0] = jnp.broadcast_to(jnp.sum(onehot, axis=0, keepdims=True), cnt_ref.shape[1:])
    fields = (w0, w1, i0, i1, rank0, rank1)
    route = jnp.zeros((tm, LANES), F32)
    for n, val in enumerate(fields):
        route = jnp.where(lane_i == n, val, route)
    route_ref[...] = route


def _gla_out(o_f, o_b, r_gate, onorm_g, w, x, g1, ng1, ng2, sc, sh, router_w):
    B, L, _ = x.shape
    r = _Rows(B, L)
    rw = jnp.pad(router_w, ((0, 0), (0, LANES - N_EXPERTS))).astype(BF)
    tm = r.tm
    return pl.pallas_call(
        _gla_out_kernel,
        grid=(r.n,),
        in_specs=[r.act(D), r.act(D), r.act(D), _const((1, 1, D)), _const((D, D)), r.act(D), r.mod(g1),
                  _const((1, 1, D)), _const((1, 1, D)), r.mod(sc), r.mod(sh), _const((D, LANES))],
        out_specs=[r.act(D), r.act(D), pl.BlockSpec((tm, LANES), lambda i: (i, 0)),
                   pl.BlockSpec((1, 8, LANES), lambda i: (i, 0, 0))],
        out_shape=[jax.ShapeDtypeStruct((B, L, D), F32), jax.ShapeDtypeStruct((B, L, D), BF),
                   jax.ShapeDtypeStruct((B * L, LANES), F32), jax.ShapeDtypeStruct((r.n, 8, LANES), F32)],
        compiler_params=_params(1),
    )(o_f, o_b, r_gate, jnp.tile(onorm_g, H).reshape(1, 1, D), w.astype(BF), x, g1, ng1.reshape(1, 1, D),
      ng2.reshape(1, 1, D), sc, sh, rw)


SEG = 128


def _segment_pieces(seg_ref, cnt_ref, tm):
    i = pl.program_id(0)
    n_piece = tm // SEG
    for e in range(N_EXPERTS):
        n_e = cnt_ref[i * N_EXPERTS + e]
        seg = pl.multiple_of(seg_ref[i * N_EXPERTS + e], SUBLANES)
        for k in range(n_piece):
            yield e * n_piece + k, e, k * SEG, seg + k * SEG, k * SEG < n_e


def _local_rank(route, e):
    ef = float(e)
    return jnp.where(route[:, 2:3] == ef, route[:, 4:5], jnp.where(route[:, 3:4] == ef, route[:, 5:6], -1.0))


def _dispatch_kernel(seg_ref, cnt_ref, h_ref, route_ref, xs_in_hbm, xs_hbm, stage, sem):
    del xs_in_hbm
    tm = h_ref.shape[0]
    h = h_ref[...]
    route = route_ref[...]
    lane = lax.broadcasted_iota(jnp.int32, (tm, SEG), 1).astype(F32)
    ranks = [_local_rank(route, e) for e in range(N_EXPERTS)]

    def copy(slot, row0):
        return pltpu.make_async_copy(stage.at[slot], xs_hbm.at[pl.ds(row0, SEG)], sem.at[slot])

    for slot, e, r0, row0, live in _segment_pieces(seg_ref, cnt_ref, tm):
        @pl.when(live)
        def _(slot=slot, e=e, r0=r0, row0=row0):
            pick = jnp.where(ranks[e] - float(r0) == lane, 1.0, 0.0).astype(BF)
            stage[slot] = _dot_tn(pick, h)
            copy(slot, row0).start()

    for slot, e, r0, row0, live in _segment_pieces(seg_ref, cnt_ref, tm):
        @pl.when(live)
        def _(slot=slot, row0=row0):
            copy(slot, row0).wait()


def _dispatch(h2, route, seg, cnt, n_rows, tm=ROW_TILE):
    T = h2.shape[0]
    grid_spec = pltpu.PrefetchScalarGridSpec(
        num_scalar_prefetch=2,
        grid=(T // tm,),
        in_specs=[pl.BlockSpec((tm, D), lambda i, s, c: (i, 0)), pl.BlockSpec((tm, LANES), lambda i, s, c: (i, 0)),
                  pl.BlockSpec(memory_space=pl.ANY)],
        out_specs=pl.BlockSpec(memory_space=pl.ANY),
        scratch_shapes=[pltpu.VMEM((N_EXPERTS * tm // SEG, SEG, D), F32),
                        pltpu.SemaphoreType.DMA((N_EXPERTS * tm // SEG,))],
    )
    return pl.pallas_call(
        _dispatch_kernel,
        grid_spec=grid_spec,
        out_shape=jax.ShapeDtypeStruct((n_rows, D), F32),
        input_output_aliases={4: 0},
        compiler_params=_params(1),
    )(seg, cnt, h2, route, jnp.zeros((n_rows, D), F32))


def _combine_kernel(seg_ref, cnt_ref, ys_hbm, route_ref, x_ref, g2_ref, ng_ref, o_ref, stage, acc_ref, sem):
    bb, tl, _ = x_ref.shape
    tm = bb * tl

    def copy(slot, row0):
        return pltpu.make_async_copy(ys_hbm.at[pl.ds(row0, SEG)], stage.at[slot], sem.at[slot])

    for slot, e, r0, row0, live in _segment_pieces(seg_ref, cnt_ref, tm):
        @pl.when(live)
        def _(slot=slot, row0=row0):
            copy(slot, row0).start()

    route = route_ref[...]
    lane = lax.broadcasted_iota(jnp.int32, (tm, SEG), 1).astype(F32)
    acc_ref[...] = jnp.zeros_like(acc_ref)
    for slot, e, r0, row0, live in _segment_pieces(seg_ref, cnt_ref, tm):
        @pl.when(live)
        def _(slot=slot, e=e, r0=r0, row0=row0):
            copy(slot, row0).wait()
            ef = float(e)
            gate = jnp.where(route[:, 2:3] == ef, route[:, 0:1], jnp.where(route[:, 3:4] == ef, route[:, 1:2], 0.0))
            pick = jnp.where(_local_rank(route, e) - float(r0) == lane, 1.0, 0.0).astype(BF)
            acc_ref[...] += gate * _dot(pick, stage[slot].astype(BF))

    o_ref[...] = x_ref[...] + g2_ref[...] * _rms(acc_ref[...], ng_ref[0]).reshape(bb, tl, D)


def _combine(ys, route, seg, cnt, x, g2, ng):
    B, L, _ = x.shape
    r = _Rows(B, L)
    tm = r.tm
    nl = r.nl
    act = pl.BlockSpec((r.bb, r.tl, D), lambda i, s, c: (i // nl, i % nl, 0))
    g2_spec = (pl.BlockSpec((1, 1, D), lambda i, s, c: (0, 0, 0)) if g2.shape[0] == 1 else
               pl.BlockSpec((r.bb, 1, D), lambda i, s, c: (i // nl, 0, 0)))
    grid_spec = pltpu.PrefetchScalarGridSpec(
        num_scalar_prefetch=2,
        grid=(r.n,),
        in_specs=[pl.BlockSpec(memory_space=pl.ANY), pl.BlockSpec((tm, LANES), lambda i, s, c: (i, 0)), act, g2_spec,
                  pl.BlockSpec((1, 1, D), lambda i, s, c: (0, 0, 0))],
        out_specs=act,
        scratch_shapes=[pltpu.VMEM((N_EXPERTS * tm // SEG, SEG, D), F32), pltpu.VMEM((tm, D), F32),
                        pltpu.SemaphoreType.DMA((N_EXPERTS * tm // SEG,))],
    )
    return pl.pallas_call(
        _combine_kernel,
        grid_spec=grid_spec,
        out_shape=jax.ShapeDtypeStruct((B, L, D), F32),
        compiler_params=_params(1),
    )(seg, cnt, ys, route, x, g2, ng.reshape(1, 1, D))


def _moe(h2, route, blk_cnt, x3, g2, ng, w_gu, w_d, tmoe):
    B, L, _ = h2.shape
    T = B * L
    cntb = blk_cnt[:, 0, :N_EXPERTS].astype(jnp.int32)
    n_tok_blocks = cntb.shape[0]
    held = (cntb + SUBLANES - 1) // SUBLANES * SUBLANES
    before = jnp.cumsum(held, axis=0) - held
    cnt = jnp.sum(held, axis=0)
    p_cnt = (cnt + SEG + tmoe - 1) // tmoe * tmoe
    p_end = jnp.cumsum(p_cnt)
    p_start = p_end - p_cnt
    seg = (p_start[None, :] + before).reshape(-1)
    n_rows = 2 * T + N_EXPERTS * (n_tok_blocks * (SUBLANES - 1) + tmoe + SEG)
    n_rows = (n_rows + tmoe - 1) // tmoe * tmoe
    n_blocks = n_rows // tmoe
    starts = jnp.arange(n_blocks, dtype=jnp.int32) * tmoe
    block_e = jnp.minimum(jnp.sum(starts[:, None] >= p_end[None, :], axis=-1), N_EXPERTS - 1).astype(jnp.int32)
    n_valid = (p_end[-1:] // tmoe).astype(jnp.int32)
    xs = _dispatch(h2.reshape(T, D), route, seg, cntb.reshape(-1), n_rows)
    ys = _swiglu(xs, w_gu, w_d, block_e, n_valid, tmoe, 512)
    return _combine(ys, route, seg, cntb.reshape(-1), x3, g2, ng)


def _trunk(x, mods, n_row, s0, p):
    B, L, _ = x.shape
    T = B * L
    ng = p['norm_g']
    sh1, sc1, g1, sh2, sc2, g2 = mods[0]
    v, x0 = _hy_in(x, ng[0, 0], sc1, sh1, p['hy_in_w'][0], p['hy_in_b'][0], p['hy_sc_w'][0], p['hy_sc_b'][0], n_row)
    kern = _hyena_kernel_taps(L, p['hy_f_w1'][0], p['hy_f_b1'][0], p['hy_f_w2'][0], p['hy_f_b2'][0],
                              p['hy_f_freq'][0], p['hy_f_w3'][0])
    if L <= 512:
        vx = _fftconv_short(v, x0, kern, p['hy_skip'][0])
    else:
        vx = _fftconv_long(v, x0, kern, p['hy_skip'][0])
    x1, h2 = _hy_out(vx, p['hy_out_w'][0], p['hy_out_b'][0], x, g1, ng[0, 1], ng[0, 2], sc2, sh2)
    tm = 1024
    ones = jnp.zeros((T // tm,), jnp.int32)
    f = _swiglu(h2.reshape(T, D), p['ffn_wgu'], p['ffn_wd'], ones, jnp.full((1,), T // tm, jnp.int32), tm, 256)
    g2_0, ng3_0 = g2, ng[0, 3]
    sh1, sc1, g1, sh2, sc2, g2 = mods[1]
    x2, qkv, gate, r_gate = _gla_in(x1, f, g2_0, ng3_0, ng[1, 0], sc1, sh1, p['gla_qkv_w'][0], p['gla_gk_w1'][0],
                                    p['gla_gk_w2'][0], p['gla_gk_b'][0], p['gla_r_w'][0], p['gla_r_b'][0])
    o_f, o_b, s_new = _gla_scan(qkv, gate, s0)
    x3, h2, route, counts = _gla_out(o_f, o_b, r_gate, p['gla_onorm_g'][0], p['gla_out_w'][0], x2, g1, ng[1, 1],
                                     ng[1, 2], sc2, sh2, p['moe_router'][0])
    x4 = _moe(h2, route, counts, x3, g2, ng[1, 3], p['moe_wgu'], p['moe_wd'], 1024 if T >= 16384 else 512)
    return x4, s_new


def kernel(x_prompt, x_sample, state_gla, c, c_ctx, ada_w, ada_b, norm_g, hy_in_w, hy_in_b, hy_sc_w, hy_sc_b, hy_f_w1, hy_f_b1, hy_f_w2, hy_f_b2, hy_f_freq, hy_f_w3, hy_skip, hy_out_w, hy_out_b, gla_qkv_w, gla_gk_w1, gla_gk_w2, gla_gk_b, gla_r_w, gla_r_b, gla_onorm_g, gla_out_w, ffn_wgu, ffn_wd, moe_router, moe_wgu, moe_wd):
    p = dict(norm_g=norm_g, hy_in_w=hy_in_w, hy_in_b=hy_in_b, hy_sc_w=hy_sc_w, hy_sc_b=hy_sc_b, hy_f_w1=hy_f_w1,
             hy_f_b1=hy_f_b1, hy_f_w2=hy_f_w2, hy_f_b2=hy_f_b2, hy_f_freq=hy_f_freq, hy_f_w3=hy_f_w3,
             hy_skip=hy_skip, hy_out_w=hy_out_w, hy_out_b=hy_out_b, gla_qkv_w=gla_qkv_w, gla_gk_w1=gla_gk_w1,
             gla_gk_w2=gla_gk_w2, gla_gk_b=gla_gk_b, gla_r_w=gla_r_w, gla_r_b=gla_r_b, gla_onorm_g=gla_onorm_g,
             gla_out_w=gla_out_w, ffn_wgu=ffn_wgu.astype(BF), ffn_wd=ffn_wd.astype(BF),
             moe_router=moe_router, moe_wgu=moe_wgu[0].astype(BF), moe_wd=moe_wd[0].astype(BF))
    n_dec = c.shape[0]
    cond = jnp.concatenate([c_ctx[None, :], c, jnp.zeros((16 - 1 - n_dec, D), F32)], axis=0)
    mod = _ada(cond, ada_w, ada_b)
    mods_ctx = [[m[:, None, :] for m in jnp.split(mod[l, 0:1], 6, axis=-1)] for l in range(DEPTH)]
    mods_dec = [[m[:, None, :] for m in jnp.split(mod[l, 1:1 + n_dec], 6, axis=-1)] for l in range(DEPTH)]
    y_prompt, state_new = _trunk(x_prompt, mods_ctx, x_prompt.shape[1], None, p)
    grid_w = 64
    y_sample, _ = _trunk(x_sample, mods_dec, grid_w, state_gla, p)
    return y_prompt, y_sample, state_new
```

```python
import functools
import math

import jax
import jax.numpy as jnp
from jax import lax
from jax.experimental import pallas as pl
from jax.experimental.pallas import tpu as pltpu

F32 = jnp.float32
BF = jnp.bfloat16

D = 1024
RMS_EPS = 1e-6
DEPTH = 2
HY_SHORT = 3
HY_EMB = 33
HY_BANDS = (HY_EMB - 1) // 2
HY_FFN = 64
HY_MAX_DECAY = math.log(1e-2) / 0.3
HY_MIN_DECAY = math.log(1e-2) / 1.5
H = 4
DK = 128
DV = 256
KD = H * DK
GLA_RANK = 16
GLA_GATE_NORM = 16.0
CHUNK = 64
D_FF = 11 * D // 4
N_EXPERTS = 8
D_FF_EXPERT = 7 * D // 2

LANES = 128
SUBLANES = 8
VMEM_LIMIT_BYTES = 56 * 1024 * 1024
ROW_TILE = 512
FFT_S2 = 128


def _params(n_axes):
    return pltpu.CompilerParams(dimension_semantics=("arbitrary",) * n_axes,
                                vmem_limit_bytes=VMEM_LIMIT_BYTES)


def _dot(a, b):
    return jnp.dot(a, b, preferred_element_type=F32)


def _dot_nt(a, b):
    return lax.dot_general(a, b, (((1,), (1,)), ((), ())), preferred_element_type=F32)


def _dot_tn(a, b):
    return lax.dot_general(a, b, (((0,), (0,)), ((), ())), preferred_element_type=F32)


def _rms(x, g):
    return x * lax.rsqrt(jnp.mean(x * x, axis=-1, keepdims=True) + RMS_EPS) * g


def _silu(x):
    return x * (1.0 / (1.0 + jnp.exp(-x)))


class _Rows:
    def __init__(self, B, L, tm=ROW_TILE):
        self.B, self.L = B, L
        if L >= tm:
            self.bb, self.tl = 1, tm
        else:
            self.bb, self.tl = tm // L, L
        assert L % self.tl == 0 and B % self.bb == 0
        self.nl = L // self.tl
        self.n = (B // self.bb) * self.nl
        self.tm = self.bb * self.tl

    def act(self, width):
        nl = self.nl
        return pl.BlockSpec((self.bb, self.tl, width), lambda i: (i // nl, i % nl, 0))

    def mod(self, m):
        nl = self.nl
        if m.shape[0] == 1:
            return pl.BlockSpec((1, 1, D), lambda i: (0, 0, 0))
        return pl.BlockSpec((self.bb, 1, D), lambda i: (i // nl, 0, 0))


def _const(shape):
    nd = len(shape)
    return pl.BlockSpec(shape, lambda *_: (0,) * nd)


def _ada_kernel(c_ref, w_ref, b_ref, o_ref):
    cs = _silu(c_ref[...])
    o_ref[0] = _dot(cs.astype(BF), w_ref[0].astype(BF)) + b_ref[0]


def _ada(cond, ada_w, ada_b):
    R = cond.shape[0]
    tn = 1536
    return pl.pallas_call(
        _ada_kernel,
        grid=(DEPTH, 6 * D // tn),
        in_specs=[pl.BlockSpec((R, D), lambda l, n: (0, 0)),
                  pl.BlockSpec((1, D, tn), lambda l, n: (l, 0, n)),
                  pl.BlockSpec((1, 1, tn), lambda l, n: (l, 0, n))],
        out_specs=pl.BlockSpec((1, R, tn), lambda l, n: (l, 0, n)),
        out_shape=jax.ShapeDtypeStruct((DEPTH, R, 6 * D), F32),
        compiler_params=_params(2),
    )(cond, ada_w, ada_b.reshape(DEPTH, 1, 6 * D))


def _hy_in_kernel(x_ref, ng_ref, sc_ref, sh_ref, w_ref, b_ref, cw_ref, cb_ref, v_ref, x0_ref, *, n_row):
    x = x_ref[...]
    bb, tl, _ = x.shape
    tm = bb * tl
    h = _rms(x, ng_ref[...]) * (1.0 + sc_ref[...]) + sh_ref[...]
    hb = h.reshape(tm, D).astype(BF)
    pos = lax.broadcasted_iota(jnp.int32, (tm, D), 0) & (n_row - 1)
    first = pos == 0
    last = pos == n_row - 1
    parts = []
    for j in range(3):
        cols = slice(j * D, (j + 1) * D)
        u = _dot(hb, w_ref[:, cols]) + b_ref[:, cols]
        up = jnp.where(first, 0.0, pltpu.roll(u, 1, 0))
        dn = jnp.where(last, 0.0, pltpu.roll(u, tm - 1, 0))
        parts.append(cb_ref[:, cols] + up * cw_ref[0:1, cols] + u * cw_ref[1:2, cols] + dn * cw_ref[2:3, cols])
    x0, x1, v = parts
    v_ref[...] = (v * x1).astype(BF).reshape(bb, tl, D)
    x0_ref[...] = x0.astype(BF).reshape(bb, tl, D)


def _hy_in(x, ng, sc, sh, w, b, cw, cb, n_row):
    B, L, _ = x.shape
    assert n_row & (n_row - 1) == 0
    r = _Rows(B, L)
    assert r.tl % n_row == 0
    return pl.pallas_call(
        functools.partial(_hy_in_kernel, n_row=n_row),
        grid=(r.n,),
        in_specs=[r.act(D), _const((1, 1, D)), r.mod(sc), r.mod(sh),
                  _const((D, 3 * D)), _const((1, 3 * D)), _const((HY_SHORT, 3 * D)), _const((1, 3 * D))],
        out_specs=[r.act(D), r.act(D)],
        out_shape=[jax.ShapeDtypeStruct((B, L, D), BF)] * 2,
        compiler_params=_params(1),
    )(x, ng.reshape(1, 1, D), sc, sh, w.astype(BF), b.reshape(1, 3 * D), cw, cb.reshape(1, 3 * D))


def _filter_kernel(z_ref, w1_ref, b1_ref, w2_ref, b2_ref, fr_ref, w3_ref, dl_ref, o_ref, *, L):
    tr = z_ref.shape[0]
    hp = lax.Precision.HIGHEST
    z = z_ref[...]
    h = jnp.sin(fr_ref[0:1, :] * (jnp.dot(z, w1_ref[...], precision=hp, preferred_element_type=F32) + b1_ref[...]))
    h = jnp.sin(fr_ref[1:2, :] * (jnp.dot(h, w2_ref[...], precision=hp, preferred_element_type=F32) + b2_ref[...]))
    hw = _dot(h.astype(BF), w3_ref[...])
    n = pl.program_id(0) * tr + lax.broadcasted_iota(jnp.int32, (tr, D), 0)
    taps = jnp.where(n < L, hw[:, :D], hw[:, D:]) * jnp.exp(-z[:, 0:1] * dl_ref[...])
    o_ref[...] = jnp.where(n == L, 0.0, taps)


def _hyena_kernel_taps(L, w1, b1, w2, b2, freq, w3):
    n = jnp.arange(2 * L, dtype=jnp.int32)
    pos = jnp.where(n < L, n, 2 * L - n) % L
    t = jnp.linspace(0.0, 1.0, L, dtype=F32)[pos][:, None]
    w = (2.0 * math.pi * pos.astype(F32) / L)[:, None]
    f = jnp.linspace(1e-4, HY_BANDS - 1, HY_BANDS, dtype=F32)[None, :]
    z = jnp.concatenate([t, jnp.cos(f * w), -jnp.sin(f * w)], axis=-1)
    z = jnp.pad(z, ((0, 0), (0, LANES - HY_EMB)))
    pad = LANES - HY_FFN
    w1p = jnp.pad(w1, ((0, LANES - HY_EMB), (0, pad)))
    w2p = jnp.pad(w2, ((0, pad), (0, pad)))
    w3p = jnp.pad(w3, ((0, pad), (0, 0))).astype(BF)
    b1p = jnp.pad(b1, (0, pad)).reshape(1, LANES)
    b2p = jnp.pad(b2, (0, pad)).reshape(1, LANES)
    frp = jnp.pad(freq, ((0, 0), (0, pad)))
    deltas = jnp.abs(jnp.linspace(HY_MIN_DECAY, HY_MAX_DECAY, D, dtype=F32)).reshape(1, D)
    tr = 512
    return pl.pallas_call(
        functools.partial(_filter_kernel, L=L),
        grid=(2 * L // tr,),
        in_specs=[pl.BlockSpec((tr, LANES), lambda i: (i, 0)), _const((LANES, LANES)), _const((1, LANES)),
                  _const((LANES, LANES)), _const((1, LANES)), _const((2, LANES)), _const((LANES, 2 * D)),
                  _const((1, D))],
        out_specs=pl.BlockSpec((tr, D), lambda i: (i, 0)),
        out_shape=jax.ShapeDtypeStruct((2 * L, D), F32),
        compiler_params=_params(1),
    )(z, w1p, b1p, w2p, b2p, frp, w3p, deltas)


def _cis(rows, cols, n, sign, scale=1.0):
    ph = (rows[:, None] * cols[None, :]) % n
    ang = ph.astype(F32) * (2.0 * math.pi / n)
    return jnp.cos(ang) * scale, jnp.sin(ang) * (sign * scale)


def _cplx_block(cr, ci):
    return jnp.concatenate([jnp.concatenate([cr, -ci], 1), jnp.concatenate([ci, cr], 1)], 0)


def _fft_short_kernel(v_ref, x0_ref, kern_ref, mk_ref, mf_ref, mi_ref, sk_ref, o_ref, ks_ref, *, L):
    n2 = 2 * L

    @pl.when(pl.program_id(0) == 0)
    def _():
        ks_ref[...] = _dot(mk_ref[...], kern_ref[...].astype(BF))

    z = jnp.concatenate([v_ref[0], v_ref[1]], 0)
    u = _dot(mf_ref[...], z)
    ur, ui = u[:n2], u[n2:]
    kr, ki = ks_ref[:n2, :], ks_ref[n2:, :]
    y = jnp.concatenate([ur * kr - ui * ki, ur * ki + ui * kr], 0).astype(BF)
    t = _dot(mi_ref[...], y)
    for j in range(2):
        conv = t[j * L:(j + 1) * L]
        o_ref[j] = ((conv + v_ref[j].astype(F32) * sk_ref[...]) * x0_ref[j].astype(F32)).astype(BF)


def _fftconv_short(v, x0, kern, skip):
    B, L, _ = v.shape
    n2 = 2 * L
    k = jnp.arange(n2, dtype=jnp.int32)
    s = jnp.arange(L, dtype=jnp.int32)
    fr, fi = _cis(k, s, n2, -1.0)
    mf = _cplx_block(fr, fi).astype(BF)
    kr, ki = _cis(k, k, n2, -1.0)
    mk = jnp.concatenate([kr, ki], 0).astype(BF)
    ir, ii = _cis(s, k, n2, 1.0, 1.0 / n2)
    mi = _cplx_block(ir, ii).astype(BF)
    pair = lambda p: (p, 0, 0)
    return pl.pallas_call(
        functools.partial(_fft_short_kernel, L=L),
        grid=(B // 2,),
        in_specs=[pl.BlockSpec((2, L, D), pair), pl.BlockSpec((2, L, D), pair), _const((n2, D)),
                  _const((2 * n2, n2)), _const((2 * n2, 2 * L)), _const((2 * L, 2 * n2)), _const((1, D))],
        out_specs=pl.BlockSpec((2, L, D), pair),
        out_shape=jax.ShapeDtypeStruct((B, L, D), BF),
        scratch_shapes=[pltpu.VMEM((2 * n2, D), F32)],
        compiler_params=_params(1),
    )(v, x0, kern, mk, mf, mi, skip.reshape(1, D))


def _lmul_kernel(m_ref, x_ref, o_ref):
    o_ref[0] = _dot(m_ref[...], x_ref[0].astype(BF)).astype(o_ref.dtype)


def _lmul(m, x, cb=8192):
    G, K, NC = x.shape
    R = m.shape[0]
    return pl.pallas_call(
        _lmul_kernel,
        grid=(G, NC // cb),
        in_specs=[pl.BlockSpec((R, K), lambda g, c: (0, 0)), pl.BlockSpec((1, K, cb), lambda g, c: (g, 0, c))],
        out_specs=pl.BlockSpec((1, R, cb), lambda g, c: (g, 0, c)),
        out_shape=jax.ShapeDtypeStruct((G, R, NC), BF),
        compiler_params=_params(2),
    )(m, x)


def _fft_mid_kernel(a_ref, af_ref, mf_ref, mi_ref, o_ref, ks_ref):
    s2 = FFT_S2
    mf = mf_ref[0]

    @pl.when(pl.program_id(1) == 0)
    def _():
        ks_ref[...] = _dot(mf, jnp.concatenate([af_ref[0, 0], af_ref[1, 0]], 0))

    u = _dot(mf, jnp.concatenate([a_ref[0, 0, 0], a_ref[0, 1, 0]], 0))
    ur, ui = u[:s2], u[s2:]
    kr, ki = ks_ref[:s2, :], ks_ref[s2:, :]
    y = jnp.concatenate([ur * kr - ui * ki, ur * ki + ui * kr], 0).astype(BF)
    z = _dot(mi_ref[0], y).astype(BF)
    o_ref[0, 0, 0] = z[:s2]
    o_ref[0, 1, 0] = z[s2:]


def _lmul_out_kernel(m_ref, z_ref, v_ref, x0_ref, sk_ref, o_ref):
    conv = _dot(m_ref[...], z_ref[0])
    o_ref[0] = ((conv + v_ref[0].astype(F32) * sk_ref[...]) * x0_ref[0].astype(F32)).astype(BF)


def _fftconv_long(v, x0, kern, skip, cb=8192):
    B, L, _ = v.shape
    n2 = 2 * L
    s2 = FFT_S2
    s1 = n2 // s2
    s1h = s1 // 2
    G = B // 2
    nc = s2 * D
    i1 = jnp.arange(s1, dtype=jnp.int32)
    i1h = jnp.arange(s1h, dtype=jnp.int32)
    i2 = jnp.arange(s2, dtype=jnp.int32)
    cr, ci = _cis(i1, i1h, s1, -1.0)
    m1 = _cplx_block(cr, ci).astype(BF)
    cr, ci = _cis(i1, i1, s1, -1.0)
    m1f = jnp.concatenate([cr, ci], 0).astype(BF)
    kk = (i1[:, None] + s1 * i2[None, :]).reshape(-1)
    gr, gi = _cis(kk, i2, n2, -1.0)
    mf = jax.vmap(_cplx_block)(gr.reshape(s1, s2, s2), gi.reshape(s1, s2, s2)).astype(BF)
    hr, hi = _cis(i2, kk, n2, 1.0, 1.0 / n2)
    hr = hr.reshape(s2, s1, s2).transpose(1, 0, 2)
    hi = hi.reshape(s2, s1, s2).transpose(1, 0, 2)
    mi = jax.vmap(_cplx_block)(hr, hi).astype(BF)
    er, ei = _cis(i1h, i1, s1, 1.0)
    m3 = _cplx_block(er, ei).astype(BF)

    a = _lmul(m1, v.reshape(G, 2 * s1h, nc), cb)
    af = _lmul(m1f, kern.reshape(1, s1, nc), cb)
    zz = pl.pallas_call(
        _fft_mid_kernel,
        grid=(s1, G),
        in_specs=[pl.BlockSpec((1, 2, 1, s2, D), lambda k, p: (p, 0, k, 0, 0)),
                  pl.BlockSpec((2, 1, s2, D), lambda k, p: (0, k, 0, 0)),
                  pl.BlockSpec((1, 2 * s2, 2 * s2), lambda k, p: (k, 0, 0)),
                  pl.BlockSpec((1, 2 * s2, 2 * s2), lambda k, p: (k, 0, 0))],
        out_specs=pl.BlockSpec((1, 2, 1, s2, D), lambda k, p: (p, 0, k, 0, 0)),
        out_shape=jax.ShapeDtypeStruct((G, 2, s1, s2, D), BF),
        scratch_shapes=[pltpu.VMEM((2 * s2, D), F32)],
        compiler_params=_params(2),
    )(a.reshape(G, 2, s1, s2, D), af.reshape(2, s1, s2, D), mf, mi)
    blk = lambda g, c: (g, 0, c)
    out = pl.pallas_call(
        _lmul_out_kernel,
        grid=(G, nc // cb),
        in_specs=[pl.BlockSpec((2 * s1h, 2 * s1), lambda g, c: (0, 0)), pl.BlockSpec((1, 2 * s1, cb), blk),
                  pl.BlockSpec((1, 2 * s1h, cb), blk), pl.BlockSpec((1, 2 * s1h, cb), blk),
                  pl.BlockSpec((1, cb), lambda g, c: (0, 0))],
        out_specs=pl.BlockSpec((1, 2 * s1h, cb), blk),
        out_shape=jax.ShapeDtypeStruct((G, 2 * s1h, nc), BF),
        compiler_params=_params(2),
    )(m3, zz.reshape(G, 2 * s1, nc), v.reshape(G, 2 * s1h, nc), x0.reshape(G, 2 * s1h, nc),
      jnp.tile(skip.reshape(1, D), (1, cb // D)))
    return out.reshape(B, L, D)


def _hy_out_kernel(vx_ref, w_ref, b_ref, x_ref, g1_ref, ng1_ref, ng2_ref, sc_ref, sh_ref, x1_ref, h_ref):
    bb, tl, _ = x_ref.shape
    m = _dot(vx_ref[...].reshape(bb * tl, D), w_ref[...]) + b_ref[...]
    x1 = x_ref[...] + g1_ref[...] * _rms(m, ng1_ref[0]).reshape(bb, tl, D)
    x1_ref[...] = x1
    h_ref[...] = (_rms(x1, ng2_ref[...]) * (1.0 + sc_ref[...]) + sh_ref[...]).astype(h_ref.dtype)


def _hy_out(vx, w, b, x, g1, ng1, ng2, sc, sh):
    B, L, _ = x.shape
    r = _Rows(B, L)
    return pl.pallas_call(
        _hy_out_kernel,
        grid=(r.n,),
        in_specs=[r.act(D), _const((D, D)), _const((1, D)), r.act(D), r.mod(g1), _const((1, 1, D)),
                  _const((1, 1, D)), r.mod(sc), r.mod(sh)],
        out_specs=[r.act(D), r.act(D)],
        out_shape=[jax.ShapeDtypeStruct((B, L, D), F32), jax.ShapeDtypeStruct((B, L, D), BF)],
        compiler_params=_params(1),
    )(vx, w.astype(BF), b.reshape(1, D), x, g1, ng1.reshape(1, 1, D), ng2.reshape(1, 1, D), sc, sh)


def _swiglu_kernel(be_ref, nv_ref, x_ref, wg_ref, wu_ref, wd_ref, o_ref, acc_ref):
    del be_ref
    f = pl.program_id(1)

    @pl.when(f == 0)
    def _():
        acc_ref[...] = jnp.zeros_like(acc_ref)

    @pl.when(pl.program_id(0) < nv_ref[0])
    def _():
        x = x_ref[...].astype(BF)
        g = _dot(x, wg_ref[0].astype(BF))
        u = _dot(x, wu_ref[0].astype(BF))
        acc_ref[...] += _dot((_silu(g) * u).astype(BF), wd_ref[0].astype(BF))

    @pl.when(f == pl.num_programs(1) - 1)
    def _():
        o_ref[...] = acc_ref[...]


def _swiglu(x, w_gu, w_d, block_e, n_valid, tm, tf):
    rows = x.shape[0]
    F = w_d.shape[1]
    nf = F // tf
    assert rows % tm == 0 and F % tf == 0
    grid_spec = pltpu.PrefetchScalarGridSpec(
        num_scalar_prefetch=2,
        grid=(rows // tm, nf),
        in_specs=[pl.BlockSpec((tm, D), lambda i, f, be, nv: (i, 0)),
                  pl.BlockSpec((1, D, tf), lambda i, f, be, nv: (be[i], 0, f)),
                  pl.BlockSpec((1, D, tf), lambda i, f, be, nv: (be[i], 0, nf + f)),
                  pl.BlockSpec((1, tf, D), lambda i, f, be, nv: (be[i], f, 0))],
        out_specs=pl.BlockSpec((tm, D), lambda i, f, be, nv: (i, 0)),
        scratch_shapes=[pltpu.VMEM((tm, D), F32)],
    )
    return pl.pallas_call(
        _swiglu_kernel,
        grid_spec=grid_spec,
        out_shape=jax.ShapeDtypeStruct((rows, D), F32),
        compiler_params=_params(2),
    )(block_e, n_valid, x, w_gu, w_gu, w_d)


def _gla_in_kernel(x1_ref, f_ref, g2_ref, ng3_ref, ng0_ref, sc_ref, sh_ref, wqkv_ref, wg1_ref, wg2_ref, bg_ref,
                   wr_ref, br_ref, x2_ref, qkv_ref, g_ref, r_ref):
    bb, tl, _ = x1_ref.shape
    tm = bb * tl
    x2 = x1_ref[...] + g2_ref[...] * _rms(f_ref[...], ng3_ref[...])
    x2_ref[...] = x2
    h = (_rms(x2, ng0_ref[...]) * (1.0 + sc_ref[...]) + sh_ref[...]).reshape(tm, D).astype(BF)
    qkv = _dot(h, wqkv_ref[...])
    scale = DK ** -0.5
    qkv_ref[:, :, :KD] = (qkv[:, :KD] * scale).astype(BF).reshape(bb, tl, KD)
    qkv_ref[:, :, KD:] = qkv[:, KD:].astype(BF).reshape(bb, tl, 3 * KD)
    low = _dot(h, wg1_ref[...]).astype(BF)
    a = _dot(low, wg2_ref[...]) + bg_ref[...]
    log_sig = jnp.minimum(a, 0.0) - jnp.log(1.0 + jnp.exp(-jnp.abs(a)))
    g_ref[...] = (log_sig / GLA_GATE_NORM).reshape(bb, tl, 2 * KD)
    r_ref[...] = _silu(_dot(h, wr_ref[...]) + br_ref[...]).astype(BF).reshape(bb, tl, D)


def _gla_in(x1, f, g2, ng3, ng0, sc, sh, qkv_w, gk_w1, gk_w2, gk_b, r_w, r_b):
    B, L, _ = x1.shape
    r = _Rows(B, L)
    w1 = jnp.zeros((D, LANES), F32).at[:, :GLA_RANK].set(gk_w1[0]).at[:, GLA_RANK:2 * GLA_RANK].set(gk_w1[1])
    w2 = jnp.zeros((LANES, 2 * KD), F32).at[:GLA_RANK, :KD].set(gk_w2[0]).at[GLA_RANK:2 * GLA_RANK, KD:].set(gk_w2[1])
    return pl.pallas_call(
        _gla_in_kernel,
        grid=(r.n,),
        in_specs=[r.act(D), r.act(D), r.mod(g2), _const((1, 1, D)), _const((1, 1, D)), r.mod(sc), r.mod(sh),
                  _const((D, 2 * KD + D)), _const((D, LANES)), _const((LANES, 2 * KD)), _const((1, 2 * KD)),
                  _const((D, D)), _const((1, D))],
        out_specs=[r.act(D), r.act(2 * KD + D), r.act(2 * KD), r.act(D)],
        out_shape=[jax.ShapeDtypeStruct((B, L, D), F32), jax.ShapeDtypeStruct((B, L, 2 * KD + D), BF),
                   jax.ShapeDtypeStruct((B, L, 2 * KD), F32), jax.ShapeDtypeStruct((B, L, D), BF)],
        compiler_params=_params(1),
    )(x1, f.reshape(B, L, D), g2, ng3.reshape(1, 1, D), ng0.reshape(1, 1, D), sc, sh, qkv_w.astype(BF),
      w1.astype(BF), w2.astype(BF), gk_b.reshape(1, 2 * KD), r_w.astype(BF), r_b.reshape(1, D))


def _gla_scan_kernel(*refs, zero_init):
    if zero_init:
        qf_ref, qb_ref, gf_ref, gb_ref, of_ref, ob_ref, sn_ref, st_ref = refs
    else:
        qf_ref, qb_ref, gf_ref, gb_ref, s0_ref, of_ref, ob_ref, sn_ref, st_ref = refs
    j = pl.program_id(1)
    tl = qf_ref.shape[1]
    n_chunk = tl // CHUNK

    @pl.when(j == 0)
    def _():
        for d in range(2):
            for h in range(H):
                if zero_init:
                    st_ref[d, h] = jnp.zeros((DV, DK), F32)
                else:
                    st_ref[d, h] = s0_ref[0, 0, d, h].T

    ri = lax.broadcasted_iota(jnp.int32, (CHUNK, CHUNK), 0)
    ci = lax.broadcasted_iota(jnp.int32, (CHUNK, CHUNK), 1)
    masks = (ci <= ri, ci >= ri)

    def chunk(d, q_ref, g_ref, o_ref, row0):
        rows = pl.ds(row0, CHUNK)
        mask = masks[d]
        tri = jnp.where(mask, 1.0, 0.0).astype(BF)
        g = g_ref[0, rows, :]
        g_hi = g.astype(BF)
        g_lo = (g - g_hi.astype(F32)).astype(BF)
        b = _dot(tri, g_hi) + _dot(tri, g_lo)
        b_last = b[CHUNK - 1:CHUNK, :] if d == 0 else b[0:1, :]
        e_pos = jnp.exp(b)
        e_neg = jnp.exp(-b)
        e_rem = jnp.exp(b_last - b)
        decay = jnp.exp(b_last)
        for h in range(H):
            ks = slice(h * DK, (h + 1) * DK)
            q = q_ref[0, rows, h * DK:(h + 1) * DK].astype(F32)
            k = q_ref[0, rows, KD + h * DK:KD + (h + 1) * DK].astype(F32)
            v = q_ref[0, rows, 2 * KD + h * DV:2 * KD + (h + 1) * DV]
            qd = (q * e_pos[:, ks]).astype(BF)
            kd = (k * e_neg[:, ks]).astype(BF)
            k_state = (k * e_rem[:, ks]).astype(BF)
            att = jnp.where(mask, _dot_nt(qd, kd), 0.0).astype(BF)
            s_t = st_ref[d, h]
            o = _dot(att, v) + _dot_nt(qd, s_t.astype(BF))
            o_ref[0, rows, h * DV:(h + 1) * DV] = o.astype(o_ref.dtype)
            st_ref[d, h] = s_t * decay[:, ks] + _dot_tn(v, k_state)

    def body(c, carry):
        chunk(0, qf_ref, gf_ref, of_ref, pl.multiple_of(c * CHUNK, CHUNK))
        chunk(1, qb_ref, gb_ref, ob_ref, pl.multiple_of((n_chunk - 1 - c) * CHUNK, CHUNK))
        return carry

    lax.fori_loop(0, n_chunk, body, 0)

    @pl.when(j == pl.num_programs(1) - 1)
    def _():
        for d in range(2):
            for h in range(H):
                sn_ref[0, 0, d, h] = st_ref[d, h].T


def _gla_scan(qkv, g, s0):
    B, L, _ = qkv.shape
    tl = min(L, ROW_TILE)
    nl = L // tl
    zero_init = s0 is None
    wq = 2 * KD + D
    in_specs = [pl.BlockSpec((1, tl, wq), lambda b, j: (b, j, 0)),
                pl.BlockSpec((1, tl, wq), lambda b, j: (b, nl - 1 - j, 0)),
                pl.BlockSpec((1, tl, KD), lambda b, j: (b, j, 0)),
                pl.BlockSpec((1, tl, KD), lambda b, j: (b, nl - 1 - j, 1))]
    args = [qkv, qkv, g, g]
    st_spec = pl.BlockSpec((1, 1, 2, H, DK, DV), lambda b, j: (b, 0, 0, 0, 0, 0))
    if not zero_init:
        in_specs.append(st_spec)
        args.append(s0)
    return pl.pallas_call(
        functools.partial(_gla_scan_kernel, zero_init=zero_init),
        grid=(B, nl),
        in_specs=in_specs,
        out_specs=[pl.BlockSpec((1, tl, D), lambda b, j: (b, j, 0)),
                   pl.BlockSpec((1, tl, D), lambda b, j: (b, nl - 1 - j, 0)), st_spec],
        out_shape=[jax.ShapeDtypeStruct((B, L, D), BF), jax.ShapeDtypeStruct((B, L, D), BF),
                   jax.ShapeDtypeStruct((B, 1, 2, H, DK, DV), F32)],
        scratch_shapes=[pltpu.VMEM((2, H, DV, DK), F32)],
        compiler_params=_params(2),
    )(*args)


def _gla_out_kernel(of_ref, ob_ref, r_ref, on_ref, w_ref, x_ref, g1_ref, ng1_ref, ng2_ref, sc_ref, sh_ref, rw_ref,
                    x3_ref, h_ref, route_ref, rt_ref, cnt_ref):
    bb, tl, _ = x_ref.shape
    tm = bb * tl
    o = of_ref[...].astype(F32) + ob_ref[...].astype(F32)
    heads = []
    for h in range(H):
        oh = o[:, :, h * DV:(h + 1) * DV]
        heads.append(oh * lax.rsqrt(jnp.mean(oh * oh, axis=-1, keepdims=True) + RMS_EPS))
    o = jnp.concatenate(heads, -1) * on_ref[...] * r_ref[...].astype(F32)
    m = _dot(o.reshape(tm, D).astype(BF), w_ref[...])
    x3 = x_ref[...] + g1_ref[...] * _rms(m, ng1_ref[0]).reshape(bb, tl, D)
    x3_ref[...] = x3
    h2 = _rms(x3, ng2_ref[...]) * (1.0 + sc_ref[...]) + sh_ref[...]
    h_ref[...] = h2.astype(h_ref.dtype)
    logits = jnp.dot(h2.reshape(tm, D), rw_ref[...], precision=lax.Precision.HIGHEST, preferred_element_type=F32)
    lane_i = lax.broadcasted_iota(jnp.int32, (tm, LANES), 1)
    lane = lane_i.astype(F32)
    logits = jnp.where(lane_i < N_EXPERTS, logits, -jnp.inf)
    m0 = jnp.max(logits, axis=-1, keepdims=True)
    i0 = jnp.min(jnp.where(logits == m0, lane, float(LANES)), axis=-1, keepdims=True)
    rest = jnp.where(lane == i0, -jnp.inf, logits)
    m1 = jnp.max(rest, axis=-1, keepdims=True)
    i1 = jnp.min(jnp.where(rest == m1, lane, float(LANES)), axis=-1, keepdims=True)
    e = jnp.exp(m1 - m0)
    w0 = 1.0 / (1.0 + e)
    w1 = e * w0
    onehot = jnp.where((lane == i0) | (lane == i1), 1.0, 0.0)
    ri = lax.broadcasted_iota(jnp.int32, (tm, tm), 0)
    ci = lax.broadcasted_iota(jnp.int32, (tm, tm), 1)
    before = jnp.where(ci < ri, 1.0, 0.0).astype(BF)
    prior = _dot(before, onehot.astype(BF))
    rank0 = jnp.sum(jnp.where(lane == i0, prior, 0.0), axis=-1, keepdims=True)
    rank1 = jnp.sum(jnp.where(lane == i1, prior, 0.0), axis=-1, keepdims=True)
    cnt_ref[0] = jnp.broadcast_to(jnp.sum(onehot, axis=0, keepdims=True), cnt_ref.shape[1:])
    fields = (w0, w1, i0, i1, rank0, rank1)
    route = jnp.zeros((tm, LANES), F32)
    for n, val in enumerate(fields):
        route = jnp.where(lane_i == n, val, route)
    route_ref[...] = route
    rt_ref[0] = route.T[0:SUBLANES, :]


def _gla_out(o_f, o_b, r_gate, onorm_g, w, x, g1, ng1, ng2, sc, sh, router_w):
    B, L, _ = x.shape
    r = _Rows(B, L)
    rw = jnp.pad(router_w, ((0, 0), (0, LANES - N_EXPERTS)))
    tm = r.tm
    return pl.pallas_call(
        _gla_out_kernel,
        grid=(r.n,),
        in_specs=[r.act(D), r.act(D), r.act(D), _const((1, 1, D)), _const((D, D)), r.act(D), r.mod(g1),
                  _const((1, 1, D)), _const((1, 1, D)), r.mod(sc), r.mod(sh), _const((D, LANES))],
        out_specs=[r.act(D), r.act(D), pl.BlockSpec((tm, LANES), lambda i: (i, 0)),
                   pl.BlockSpec((1, SUBLANES, tm), lambda i: (i, 0, 0)),
                   pl.BlockSpec((1, SUBLANES, LANES), lambda i: (i, 0, 0))],
        out_shape=[jax.ShapeDtypeStruct((B, L, D), F32), jax.ShapeDtypeStruct((B, L, D), BF),
                   jax.ShapeDtypeStruct((B * L, LANES), F32), jax.ShapeDtypeStruct((r.n, SUBLANES, tm), F32),
                   jax.ShapeDtypeStruct((r.n, SUBLANES, LANES), F32)],
        compiler_params=_params(1),
    )(o_f, o_b, r_gate, jnp.tile(onorm_g, H).reshape(1, 1, D), w.astype(BF), x, g1, ng1.reshape(1, 1, D),
      ng2.reshape(1, 1, D), sc, sh, rw)


SEG = 128
MAIN = 2


def _seg_start(seg_ref, step, e):
    return pl.multiple_of(seg_ref[step * N_EXPERTS + e], SUBLANES)


def _dispatch_kernel(seg_ref, cnt_ref, h_ref, rt_ref, xs_in_hbm, xs_hbm, stage, sem):
    del xs_in_hbm
    i = pl.program_id(0)
    tm = h_ref.shape[0]
    n_piece = tm // SEG
    h = h_ref[...]
    assert n_piece == 2 * MAIN
    ex0, ex1, lr0, lr1 = (jnp.broadcast_to(rt_ref[0, n:n + 1, :], (MAIN * SEG, tm)) for n in range(2, 6))
    rank_row = lax.broadcasted_iota(jnp.int32, (MAIN * SEG, tm), 0).astype(F32)

    def copy(e, k):
        dst = xs_hbm.at[pl.ds(_seg_start(seg_ref, i, e) + k * SEG, SEG)]
        return pltpu.make_async_copy(stage.at[e, pl.ds(k * SEG, SEG)], dst, sem.at[e * n_piece + k])

    def live(e, k):
        return k * SEG < cnt_ref[i * N_EXPERTS + e]

    for e in range(N_EXPERTS):
        ef = float(e)
        lr = jnp.where(ex0 == ef, lr0, jnp.where(ex1 == ef, lr1, -1.0))
        pick = jnp.where(lr == rank_row, 1.0, 0.0).astype(BF)
        stage[e, 0:MAIN * SEG] = _dot(pick, h)

        @pl.when(live(e, MAIN))
        def _(e=e, lr=lr):
            more = jnp.where(lr - float(MAIN * SEG) == rank_row, 1.0, 0.0).astype(BF)
            stage[e, MAIN * SEG:2 * MAIN * SEG] = _dot(more, h)

        for k in range(n_piece):
            @pl.when(live(e, k))
            def _(e=e, k=k):
                copy(e, k).start()

    for e in range(N_EXPERTS):
        for k in range(n_piece):
            @pl.when(live(e, k))
            def _(e=e, k=k):
                copy(e, k).wait()


def _dispatch(h2, route_t, seg, cnt, n_rows, tm=ROW_TILE):
    T = h2.shape[0]
    grid_spec = pltpu.PrefetchScalarGridSpec(
        num_scalar_prefetch=2,
        grid=(T // tm,),
        in_specs=[pl.BlockSpec((tm, D), lambda i, s, c: (i, 0)),
                  pl.BlockSpec((1, SUBLANES, tm), lambda i, s, c: (i, 0, 0)),
                  pl.BlockSpec(memory_space=pl.ANY)],
        out_specs=pl.BlockSpec(memory_space=pl.ANY),
        scratch_shapes=[pltpu.VMEM((N_EXPERTS, tm, D), F32),
                        pltpu.SemaphoreType.DMA((N_EXPERTS * tm // SEG,))],
    )
    return pl.pallas_call(
        _dispatch_kernel,
        grid_spec=grid_spec,
        out_shape=jax.ShapeDtypeStruct((n_rows, D), F32),
        input_output_aliases={4: 0},
        compiler_params=_params(1),
    )(seg, cnt, h2, route_t, jnp.zeros((n_rows, D), F32))


def _combine_kernel(seg_ref, cnt_ref, ys_hbm, route_ref, x_ref, g2_ref, ng_ref, o_ref, gbuf, extra, acc_ref, sem,
                    esem):
    i = pl.program_id(0)
    bb, tl, _ = x_ref.shape
    tm = bb * tl
    n_piece = tm // SEG

    def main_copy(step, slot, e, k):
        src = ys_hbm.at[pl.ds(_seg_start(seg_ref, step, e) + k * SEG, SEG)]
        return pltpu.make_async_copy(src, gbuf.at[slot, pl.ds((e * MAIN + k) * SEG, SEG)], sem.at[slot, e * MAIN + k])

    def fetch(step, slot):
        for e in range(N_EXPERTS):
            for k in range(MAIN):
                main_copy(step, slot, e, k).start()

    @pl.when(i == 0)
    def _():
        fetch(0, 0)

    @pl.when(i + 1 < pl.num_programs(0))
    def _():
        fetch(i + 1, (i + 1) % 2)

    slot = i % 2
    route = route_ref[...]

    def spread_cols(e, r0, width):
        ef = float(e)
        gate = jnp.where(route[:, 2:3] == ef, route[:, 0:1], jnp.where(route[:, 3:4] == ef, route[:, 1:2], 0.0))
        rank = jnp.where(route[:, 2:3] == ef, route[:, 4:5], jnp.where(route[:, 3:4] == ef, route[:, 5:6], -1.0))
        col = lax.broadcasted_iota(jnp.int32, (tm, width), 1).astype(F32) + float(r0)
        return (jnp.where(rank == col, 1.0, 0.0) * gate).astype(BF)

    spread = jnp.concatenate([spread_cols(e, 0, MAIN * SEG) for e in range(N_EXPERTS)], axis=1)
    for e in range(N_EXPERTS):
        for k in range(MAIN):
            main_copy(i, slot, e, k).wait()
    acc_ref[...] = _dot(spread, gbuf[slot].astype(BF))
    for e in range(N_EXPERTS):
        for k in range(MAIN, n_piece):
            @pl.when(k * SEG < cnt_ref[i * N_EXPERTS + e])
            def _(e=e, k=k):
                src = ys_hbm.at[pl.ds(_seg_start(seg_ref, i, e) + k * SEG, SEG)]
                cp = pltpu.make_async_copy(src, extra, esem)
                cp.start()
                more = spread_cols(e, k * SEG, SEG)
                cp.wait()
                acc_ref[...] += _dot(more, extra[...].astype(BF))

    o_ref[...] = x_ref[...] + g2_ref[...] * _rms(acc_ref[...], ng_ref[0]).reshape(bb, tl, D)


def _combine(ys, route, seg, cnt, x, g2, ng):
    B, L, _ = x.shape
    r = _Rows(B, L)
    tm = r.tm
    nl = r.nl
    act = pl.BlockSpec((r.bb, r.tl, D), lambda i, s, c: (i // nl, i % nl, 0))
    g2_spec = (pl.BlockSpec((1, 1, D), lambda i, s, c: (0, 0, 0)) if g2.shape[0] == 1 else
               pl.BlockSpec((r.bb, 1, D), lambda i, s, c: (i // nl, 0, 0)))
    grid_spec = pltpu.PrefetchScalarGridSpec(
        num_scalar_prefetch=2,
        grid=(r.n,),
        in_specs=[pl.BlockSpec(memory_space=pl.ANY), pl.BlockSpec((tm, LANES), lambda i, s, c: (i, 0)), act, g2_spec,
                  pl.BlockSpec((1, 1, D), lambda i, s, c: (0, 0, 0))],
        out_specs=act,
        scratch_shapes=[pltpu.VMEM((2, N_EXPERTS * MAIN * SEG, D), F32), pltpu.VMEM((SEG, D), F32),
                        pltpu.VMEM((tm, D), F32), pltpu.SemaphoreType.DMA((2, N_EXPERTS * MAIN)),
                        pltpu.SemaphoreType.DMA(())],
    )
    return pl.pallas_call(
        _combine_kernel,
        grid_spec=grid_spec,
        out_shape=jax.ShapeDtypeStruct((B, L, D), F32),
        compiler_params=_params(1),
    )(seg, cnt, ys, route, x, g2, ng.reshape(1, 1, D))


def _moe(h2, route, route_t, blk_cnt, x3, g2, ng, w_gu, w_d, tmoe):
    B, L, _ = h2.shape
    T = B * L
    cntb = blk_cnt[:, 0, :N_EXPERTS].astype(jnp.int32)
    n_tok_blocks = cntb.shape[0]
    held = (cntb + SUBLANES - 1) // SUBLANES * SUBLANES
    before = jnp.cumsum(held, axis=0) - held
    cnt = jnp.sum(held, axis=0)
    p_cnt = (cnt + MAIN * SEG + tmoe - 1) // tmoe * tmoe
    p_end = jnp.cumsum(p_cnt)
    p_start = p_end - p_cnt
    seg = (p_start[None, :] + before).reshape(-1)
    n_rows = 2 * T + N_EXPERTS * (n_tok_blocks * (SUBLANES - 1) + tmoe + MAIN * SEG)
    n_rows = (n_rows + tmoe - 1) // tmoe * tmoe
    n_blocks = n_rows // tmoe
    starts = jnp.arange(n_blocks, dtype=jnp.int32) * tmoe
    block_e = jnp.minimum(jnp.sum(starts[:, None] >= p_end[None, :], axis=-1), N_EXPERTS - 1).astype(jnp.int32)
    n_valid = (p_end[-1:] // tmoe).astype(jnp.int32)
    xs = _dispatch(h2.reshape(T, D), route_t, seg, cntb.reshape(-1), n_rows)
    ys = _swiglu(xs, w_gu, w_d, block_e, n_valid, tmoe, 512)
    return _combine(ys, route, seg, cntb.reshape(-1), x3, g2, ng)


def _trunk(x, mods, n_row, s0, p):
    B, L, _ = x.shape
    T = B * L
    ng = p['norm_g']
    sh1, sc1, g1, sh2, sc2, g2 = mods[0]
    v, x0 = _hy_in(x, ng[0, 0], sc1, sh1, p['hy_in_w'][0], p['hy_in_b'][0], p['hy_sc_w'][0], p['hy_sc_b'][0], n_row)
    kern = _hyena_kernel_taps(L, p['hy_f_w1'][0], p['hy_f_b1'][0], p['hy_f_w2'][0], p['hy_f_b2'][0],
                              p['hy_f_freq'][0], p['hy_f_w3'][0])
    if L <= 512:
        vx = _fftconv_short(v, x0, kern, p['hy_skip'][0])
    else:
        vx = _fftconv_long(v, x0, kern, p['hy_skip'][0])
    x1, h2 = _hy_out(vx, p['hy_out_w'][0], p['hy_out_b'][0], x, g1, ng[0, 1], ng[0, 2], sc2, sh2)
    tm = 1024
    ones = jnp.zeros((T // tm,), jnp.int32)
    f = _swiglu(h2.reshape(T, D), p['ffn_wgu'], p['ffn_wd'], ones, jnp.full((1,), T // tm, jnp.int32), tm, 256)
    g2_0, ng3_0 = g2, ng[0, 3]
    sh1, sc1, g1, sh2, sc2, g2 = mods[1]
    x2, qkv, gate, r_gate = _gla_in(x1, f, g2_0, ng3_0, ng[1, 0], sc1, sh1, p['gla_qkv_w'][0], p['gla_gk_w1'][0],
                                    p['gla_gk_w2'][0], p['gla_gk_b'][0], p['gla_r_w'][0], p['gla_r_b'][0])
    o_f, o_b, s_new = _gla_scan(qkv, gate, s0)
    x3, h2, route, route_t, counts = _gla_out(o_f, o_b, r_gate, p['gla_onorm_g'][0], p['gla_out_w'][0], x2, g1, ng[1, 1],
                                     ng[1, 2], sc2, sh2, p['moe_router'][0])
    x4 = _moe(h2, route, route_t, counts, x3, g2, ng[1, 3], p['moe_wgu'], p['moe_wd'], 1024 if T >= 16384 else 512)
    return x4, s_new


def kernel(x_prompt, x_sample, state_gla, c, c_ctx, ada_w, ada_b, norm_g, hy_in_w, hy_in_b, hy_sc_w, hy_sc_b, hy_f_w1, hy_f_b1, hy_f_w2, hy_f_b2, hy_f_freq, hy_f_w3, hy_skip, hy_out_w, hy_out_b, gla_qkv_w, gla_gk_w1, gla_gk_w2, gla_gk_b, gla_r_w, gla_r_b, gla_onorm_g, gla_out_w, ffn_wgu, ffn_wd, moe_router, moe_wgu, moe_wd):
    p = dict(norm_g=norm_g, hy_in_w=hy_in_w, hy_in_b=hy_in_b, hy_sc_w=hy_sc_w, hy_sc_b=hy_sc_b, hy_f_w1=hy_f_w1,
             hy_f_b1=hy_f_b1, hy_f_w2=hy_f_w2, hy_f_b2=hy_f_b2, hy_f_freq=hy_f_freq, hy_f_w3=hy_f_w3,
             hy_skip=hy_skip, hy_out_w=hy_out_w, hy_out_b=hy_out_b, gla_qkv_w=gla_qkv_w, gla_gk_w1=gla_gk_w1,
             gla_gk_w2=gla_gk_w2, gla_gk_b=gla_gk_b, gla_r_w=gla_r_w, gla_r_b=gla_r_b, gla_onorm_g=gla_onorm_g,
             gla_out_w=gla_out_w, ffn_wgu=ffn_wgu.astype(BF), ffn_wd=ffn_wd.astype(BF),
             moe_router=moe_router, moe_wgu=moe_wgu[0], moe_wd=moe_wd[0])
    n_dec = c.shape[0]
    cond = jnp.concatenate([c_ctx[None, :], c, jnp.zeros((16 - 1 - n_dec, D), F32)], axis=0)
    mod = _ada(cond, ada_w, ada_b)
    mods_ctx = [[m[:, None, :] for m in jnp.split(mod[l, 0:1], 6, axis=-1)] for l in range(DEPTH)]
    mods_dec = [[m[:, None, :] for m in jnp.split(mod[l, 1:1 + n_dec], 6, axis=-1)] for l in range(DEPTH)]
    y_prompt, state_new = _trunk(x_prompt, mods_ctx, x_prompt.shape[1], None, p)
    grid_w = 64
    y_sample, _ = _trunk(x_sample, mods_dec, grid_w, state_gla, p)
    return y_prompt, y_sample, state_new
```

```python
import functools
import math

import jax
import jax.numpy as jnp
from jax import lax
from jax.experimental import pallas as pl
from jax.experimental.pallas import tpu as pltpu

F32 = jnp.float32
BF = jnp.bfloat16

D = 1024
RMS_EPS = 1e-6
DEPTH = 2
HY_SHORT = 3
HY_EMB = 33
HY_BANDS = (HY_EMB - 1) // 2
HY_FFN = 64
HY_MAX_DECAY = math.log(1e-2) / 0.3
HY_MIN_DECAY = math.log(1e-2) / 1.5
H = 4
DK = 128
DV = 256
KD = H * DK
GLA_RANK = 16
GLA_GATE_NORM = 16.0
CHUNK = 64
D_FF = 11 * D // 4
N_EXPERTS = 8
D_FF_EXPERT = 7 * D // 2

LANES = 128
SUBLANES = 8
VMEM_LIMIT_BYTES = 56 * 1024 * 1024
ROW_TILE = 512
FFT_S2 = 128


def _params(n_axes):
    return pltpu.CompilerParams(dimension_semantics=("arbitrary",) * n_axes,
                                vmem_limit_bytes=VMEM_LIMIT_BYTES)


def _dot(a, b):
    return jnp.dot(a, b, preferred_element_type=F32)


def _dot_nt(a, b):
    return lax.dot_general(a, b, (((1,), (1,)), ((), ())), preferred_element_type=F32)


def _dot_tn(a, b):
    return lax.dot_general(a, b, (((0,), (0,)), ((), ())), preferred_element_type=F32)


def _rms(x, g):
    return x * lax.rsqrt(jnp.mean(x * x, axis=-1, keepdims=True) + RMS_EPS) * g


def _silu(x):
    return x * (1.0 / (1.0 + jnp.exp(-x)))


class _Rows:
    def __init__(self, B, L, tm=ROW_TILE):
        self.B, self.L = B, L
        if L >= tm:
            self.bb, self.tl = 1, tm
        else:
            self.bb, self.tl = tm // L, L
        assert L % self.tl == 0 and B % self.bb == 0
        self.nl = L // self.tl
        self.n = (B // self.bb) * self.nl
        self.tm = self.bb * self.tl

    def act(self, width):
        nl = self.nl
        return pl.BlockSpec((self.bb, self.tl, width), lambda i: (i // nl, i % nl, 0))

    def mod(self, m):
        nl = self.nl
        if m.shape[0] == 1:
            return pl.BlockSpec((1, 1, D), lambda i: (0, 0, 0))
        return pl.BlockSpec((self.bb, 1, D), lambda i: (i // nl, 0, 0))


def _const(shape):
    nd = len(shape)
    return pl.BlockSpec(shape, lambda *_: (0,) * nd)


def _ada_kernel(c_ref, w_ref, b_ref, o_ref):
    cs = _silu(c_ref[...])
    o_ref[0] = _dot(cs.astype(BF), w_ref[0].astype(BF)) + b_ref[0]


def _ada(cond, ada_w, ada_b):
    R = cond.shape[0]
    tn = 1536
    return pl.pallas_call(
        _ada_kernel,
        grid=(DEPTH, 6 * D // tn),
        in_specs=[pl.BlockSpec((R, D), lambda l, n: (0, 0)),
                  pl.BlockSpec((1, D, tn), lambda l, n: (l, 0, n)),
                  pl.BlockSpec((1, 1, tn), lambda l, n: (l, 0, n))],
        out_specs=pl.BlockSpec((1, R, tn), lambda l, n: (l, 0, n)),
        out_shape=jax.ShapeDtypeStruct((DEPTH, R, 6 * D), F32),
        compiler_params=_params(2),
    )(cond, ada_w, ada_b.reshape(DEPTH, 1, 6 * D))


def _hy_in_kernel(x_ref, ng_ref, sc_ref, sh_ref, w_ref, b_ref, cw_ref, cb_ref, v_ref, x0_ref, *, n_row):
    x = x_ref[...]
    bb, tl, _ = x.shape
    tm = bb * tl
    h = _rms(x, ng_ref[...]) * (1.0 + sc_ref[...]) + sh_ref[...]
    hb = h.reshape(tm, D).astype(BF)
    pos = lax.broadcasted_iota(jnp.int32, (tm, D), 0) & (n_row - 1)
    first = pos == 0
    last = pos == n_row - 1
    parts = []
    for j in range(3):
        cols = slice(j * D, (j + 1) * D)
        u = _dot(hb, w_ref[:, cols]) + b_ref[:, cols]
        up = jnp.where(first, 0.0, pltpu.roll(u, 1, 0))
        dn = jnp.where(last, 0.0, pltpu.roll(u, tm - 1, 0))
        parts.append(cb_ref[:, cols] + up * cw_ref[0:1, cols] + u * cw_ref[1:2, cols] + dn * cw_ref[2:3, cols])
    x0, x1, v = parts
    v_ref[...] = (v * x1).astype(BF).reshape(bb, tl, D)
    x0_ref[...] = x0.astype(BF).reshape(bb, tl, D)


def _hy_in(x, ng, sc, sh, w, b, cw, cb, n_row):
    B, L, _ = x.shape
    assert n_row & (n_row - 1) == 0
    r = _Rows(B, L)
    assert r.tl % n_row == 0
    return pl.pallas_call(
        functools.partial(_hy_in_kernel, n_row=n_row),
        grid=(r.n,),
        in_specs=[r.act(D), _const((1, 1, D)), r.mod(sc), r.mod(sh),
                  _const((D, 3 * D)), _const((1, 3 * D)), _const((HY_SHORT, 3 * D)), _const((1, 3 * D))],
        out_specs=[r.act(D), r.act(D)],
        out_shape=[jax.ShapeDtypeStruct((B, L, D), BF)] * 2,
        compiler_params=_params(1),
    )(x, ng.reshape(1, 1, D), sc, sh, w.astype(BF), b.reshape(1, 3 * D), cw, cb.reshape(1, 3 * D))


def _filter_kernel(z_ref, w1_ref, b1_ref, w2_ref, b2_ref, fr_ref, w3_ref, dl_ref, o_ref, *, L):
    tr = z_ref.shape[0]
    hp = lax.Precision.HIGHEST
    z = z_ref[...]
    h = jnp.sin(fr_ref[0:1, :] * (jnp.dot(z, w1_ref[...], precision=hp, preferred_element_type=F32) + b1_ref[...]))
    h = jnp.sin(fr_ref[1:2, :] * (jnp.dot(h, w2_ref[...], precision=hp, preferred_element_type=F32) + b2_ref[...]))
    hw = _dot(h.astype(BF), w3_ref[...])
    n = pl.program_id(0) * tr + lax.broadcasted_iota(jnp.int32, (tr, D), 0)
    taps = jnp.where(n < L, hw[:, :D], hw[:, D:]) * jnp.exp(-z[:, 0:1] * dl_ref[...])
    o_ref[...] = jnp.where(n == L, 0.0, taps)


def _hyena_kernel_taps(L, w1, b1, w2, b2, freq, w3):
    n = jnp.arange(2 * L, dtype=jnp.int32)
    pos = jnp.where(n < L, n, 2 * L - n) % L
    t = jnp.linspace(0.0, 1.0, L, dtype=F32)[pos][:, None]
    w = (2.0 * math.pi * pos.astype(F32) / L)[:, None]
    f = jnp.linspace(1e-4, HY_BANDS - 1, HY_BANDS, dtype=F32)[None, :]
    z = jnp.concatenate([t, jnp.cos(f * w), -jnp.sin(f * w)], axis=-1)
    z = jnp.pad(z, ((0, 0), (0, LANES - HY_EMB)))
    pad = LANES - HY_FFN
    w1p = jnp.pad(w1, ((0, LANES - HY_EMB), (0, pad)))
    w2p = jnp.pad(w2, ((0, pad), (0, pad)))
    w3p = jnp.pad(w3, ((0, pad), (0, 0))).astype(BF)
    b1p = jnp.pad(b1, (0, pad)).reshape(1, LANES)
    b2p = jnp.pad(b2, (0, pad)).reshape(1, LANES)
    frp = jnp.pad(freq, ((0, 0), (0, pad)))
    deltas = jnp.abs(jnp.linspace(HY_MIN_DECAY, HY_MAX_DECAY, D, dtype=F32)).reshape(1, D)
    tr = 512
    return pl.pallas_call(
        functools.partial(_filter_kernel, L=L),
        grid=(2 * L // tr,),
        in_specs=[pl.BlockSpec((tr, LANES), lambda i: (i, 0)), _const((LANES, LANES)), _const((1, LANES)),
                  _const((LANES, LANES)), _const((1, LANES)), _const((2, LANES)), _const((LANES, 2 * D)),
                  _const((1, D))],
        out_specs=pl.BlockSpec((tr, D), lambda i: (i, 0)),
        out_shape=jax.ShapeDtypeStruct((2 * L, D), F32),
        compiler_params=_params(1),
    )(z, w1p, b1p, w2p, b2p, frp, w3p, deltas)


def _cis(rows, cols, n, sign, scale=1.0):
    ph = (rows[:, None] * cols[None, :]) % n
    ang = ph.astype(F32) * (2.0 * math.pi / n)
    return jnp.cos(ang) * scale, jnp.sin(ang) * (sign * scale)


def _cplx_block(cr, ci):
    return jnp.concatenate([jnp.concatenate([cr, -ci], 1), jnp.concatenate([ci, cr], 1)], 0)


def _fft_short_kernel(v_ref, x0_ref, kern_ref, mk_ref, mf_ref, mi_ref, sk_ref, o_ref, ks_ref, *, L):
    n2 = 2 * L

    @pl.when(pl.program_id(0) == 0)
    def _():
        ks_ref[...] = _dot(mk_ref[...], kern_ref[...].astype(BF))

    z = jnp.concatenate([v_ref[0], v_ref[1]], 0)
    u = _dot(mf_ref[...], z)
    ur, ui = u[:n2], u[n2:]
    kr, ki = ks_ref[:n2, :], ks_ref[n2:, :]
    y = jnp.concatenate([ur * kr - ui * ki, ur * ki + ui * kr], 0).astype(BF)
    t = _dot(mi_ref[...], y)
    for j in range(2):
        conv = t[j * L:(j + 1) * L]
        o_ref[j] = ((conv + v_ref[j].astype(F32) * sk_ref[...]) * x0_ref[j].astype(F32)).astype(BF)


def _fftconv_short(v, x0, kern, skip):
    B, L, _ = v.shape
    n2 = 2 * L
    k = jnp.arange(n2, dtype=jnp.int32)
    s = jnp.arange(L, dtype=jnp.int32)
    fr, fi = _cis(k, s, n2, -1.0)
    mf = _cplx_block(fr, fi).astype(BF)
    kr, ki = _cis(k, k, n2, -1.0)
    mk = jnp.concatenate([kr, ki], 0).astype(BF)
    ir, ii = _cis(s, k, n2, 1.0, 1.0 / n2)
    mi = _cplx_block(ir, ii).astype(BF)
    pair = lambda p: (p, 0, 0)
    return pl.pallas_call(
        functools.partial(_fft_short_kernel, L=L),
        grid=(B // 2,),
        in_specs=[pl.BlockSpec((2, L, D), pair), pl.BlockSpec((2, L, D), pair), _const((n2, D)),
                  _const((2 * n2, n2)), _const((2 * n2, 2 * L)), _const((2 * L, 2 * n2)), _const((1, D))],
        out_specs=pl.BlockSpec((2, L, D), pair),
        out_shape=jax.ShapeDtypeStruct((B, L, D), BF),
        scratch_shapes=[pltpu.VMEM((2 * n2, D), F32)],
        compiler_params=_params(1),
    )(v, x0, kern, mk, mf, mi, skip.reshape(1, D))


def _lmul_kernel(m_ref, x_ref, o_ref):
    o_ref[0] = _dot(m_ref[...], x_ref[0].astype(BF)).astype(o_ref.dtype)


def _lmul(m, x, cb=8192):
    G, K, NC = x.shape
    R = m.shape[0]
    return pl.pallas_call(
        _lmul_kernel,
        grid=(G, NC // cb),
        in_specs=[pl.BlockSpec((R, K), lambda g, c: (0, 0)), pl.BlockSpec((1, K, cb), lambda g, c: (g, 0, c))],
        out_specs=pl.BlockSpec((1, R, cb), lambda g, c: (g, 0, c)),
        out_shape=jax.ShapeDtypeStruct((G, R, NC), BF),
        compiler_params=_params(2),
    )(m, x)


def _fft_mid_kernel(a_ref, af_ref, mf_ref, mi_ref, o_ref, ks_ref):
    s2 = FFT_S2
    mf = mf_ref[0]

    @pl.when(pl.program_id(1) == 0)
    def _():
        ks_ref[...] = _dot(mf, jnp.concatenate([af_ref[0, 0], af_ref[1, 0]], 0))

    u = _dot(mf, jnp.concatenate([a_ref[0, 0, 0], a_ref[0, 1, 0]], 0))
    ur, ui = u[:s2], u[s2:]
    kr, ki = ks_ref[:s2, :], ks_ref[s2:, :]
    y = jnp.concatenate([ur * kr - ui * ki, ur * ki + ui * kr], 0).astype(BF)
    z = _dot(mi_ref[0], y).astype(BF)
    o_ref[0, 0, 0] = z[:s2]
    o_ref[0, 1, 0] = z[s2:]


def _lmul_out_kernel(m_ref, z_ref, v_ref, x0_ref, sk_ref, o_ref):
    conv = _dot(m_ref[...], z_ref[0])
    o_ref[0] = ((conv + v_ref[0].astype(F32) * sk_ref[...]) * x0_ref[0].astype(F32)).astype(BF)


def _fftconv_long(v, x0, kern, skip, cb=8192):
    B, L, _ = v.shape
    n2 = 2 * L
    s2 = FFT_S2
    s1 = n2 // s2
    s1h = s1 // 2
    G = B // 2
    nc = s2 * D
    i1 = jnp.arange(s1, dtype=jnp.int32)
    i1h = jnp.arange(s1h, dtype=jnp.int32)
    i2 = jnp.arange(s2, dtype=jnp.int32)
    cr, ci = _cis(i1, i1h, s1, -1.0)
    m1 = _cplx_block(cr, ci).astype(BF)
    cr, ci = _cis(i1, i1, s1, -1.0)
    m1f = jnp.concatenate([cr, ci], 0).astype(BF)
    kk = (i1[:, None] + s1 * i2[None, :]).reshape(-1)
    gr, gi = _cis(kk, i2, n2, -1.0)
    mf = jax.vmap(_cplx_block)(gr.reshape(s1, s2, s2), gi.reshape(s1, s2, s2)).astype(BF)
    hr, hi = _cis(i2, kk, n2, 1.0, 1.0 / n2)
    hr = hr.reshape(s2, s1, s2).transpose(1, 0, 2)
    hi = hi.reshape(s2, s1, s2).transpose(1, 0, 2)
    mi = jax.vmap(_cplx_block)(hr, hi).astype(BF)
    er, ei = _cis(i1h, i1, s1, 1.0)
    m3 = _cplx_block(er, ei).astype(BF)

    a = _lmul(m1, v.reshape(G, 2 * s1h, nc), cb)
    af = _lmul(m1f, kern.reshape(1, s1, nc), cb)
    zz = pl.pallas_call(
        _fft_mid_kernel,
        grid=(s1, G),
        in_specs=[pl.BlockSpec((1, 2, 1, s2, D), lambda k, p: (p, 0, k, 0, 0)),
                  pl.BlockSpec((2, 1, s2, D), lambda k, p: (0, k, 0, 0)),
                  pl.BlockSpec((1, 2 * s2, 2 * s2), lambda k, p: (k, 0, 0)),
                  pl.BlockSpec((1, 2 * s2, 2 * s2), lambda k, p: (k, 0, 0))],
        out_specs=pl.BlockSpec((1, 2, 1, s2, D), lambda k, p: (p, 0, k, 0, 0)),
        out_shape=jax.ShapeDtypeStruct((G, 2, s1, s2, D), BF),
        scratch_shapes=[pltpu.VMEM((2 * s2, D), F32)],
        compiler_params=_params(2),
    )(a.reshape(G, 2, s1, s2, D), af.reshape(2, s1, s2, D), mf, mi)
    blk = lambda g, c: (g, 0, c)
    out = pl.pallas_call(
        _lmul_out_kernel,
        grid=(G, nc // cb),
        in_specs=[pl.BlockSpec((2 * s1h, 2 * s1), lambda g, c: (0, 0)), pl.BlockSpec((1, 2 * s1, cb), blk),
                  pl.BlockSpec((1, 2 * s1h, cb), blk), pl.BlockSpec((1, 2 * s1h, cb), blk),
                  pl.BlockSpec((1, cb), lambda g, c: (0, 0))],
        out_specs=pl.BlockSpec((1, 2 * s1h, cb), blk),
        out_shape=jax.ShapeDtypeStruct((G, 2 * s1h, nc), BF),
        compiler_params=_params(2),
    )(m3, zz.reshape(G, 2 * s1, nc), v.reshape(G, 2 * s1h, nc), x0.reshape(G, 2 * s1h, nc),
      jnp.tile(skip.reshape(1, D), (1, cb // D)))
    return out.reshape(B, L, D)


def _hy_out_kernel(vx_ref, w_ref, b_ref, x_ref, g1_ref, ng1_ref, ng2_ref, sc_ref, sh_ref, x1_ref, h_ref):
    bb, tl, _ = x_ref.shape
    m = _dot(vx_ref[...].reshape(bb * tl, D), w_ref[...]) + b_ref[...]
    x1 = x_ref[...] + g1_ref[...] * _rms(m, ng1_ref[0]).reshape(bb, tl, D)
    x1_ref[...] = x1
    h_ref[...] = (_rms(x1, ng2_ref[...]) * (1.0 + sc_ref[...]) + sh_ref[...]).astype(h_ref.dtype)


def _hy_out(vx, w, b, x, g1, ng1, ng2, sc, sh):
    B, L, _ = x.shape
    r = _Rows(B, L)
    return pl.pallas_call(
        _hy_out_kernel,
        grid=(r.n,),
        in_specs=[r.act(D), _const((D, D)), _const((1, D)), r.act(D), r.mod(g1), _const((1, 1, D)),
                  _const((1, 1, D)), r.mod(sc), r.mod(sh)],
        out_specs=[r.act(D), r.act(D)],
        out_shape=[jax.ShapeDtypeStruct((B, L, D), F32), jax.ShapeDtypeStruct((B, L, D), BF)],
        compiler_params=_params(1),
    )(vx, w.astype(BF), b.reshape(1, D), x, g1, ng1.reshape(1, 1, D), ng2.reshape(1, 1, D), sc, sh)


def _swiglu_kernel(be_ref, nv_ref, x_ref, wg_ref, wu_ref, wd_ref, o_ref, acc_ref):
    del be_ref
    f = pl.program_id(1)

    @pl.when(f == 0)
    def _():
        acc_ref[...] = jnp.zeros_like(acc_ref)

    @pl.when(pl.program_id(0) < nv_ref[0])
    def _():
        x = x_ref[...].astype(BF)
        tf = wg_ref.shape[2]
        sub = 256 if tf % 256 == 0 else tf
        part = None
        for c in range(tf // sub):
            cols = slice(c * sub, (c + 1) * sub)
            g = _dot(x, wg_ref[0, :, cols])
            u = _dot(x, wu_ref[0, :, cols])
            y = _dot((_silu(g) * u).astype(BF), wd_ref[0, cols, :])
            part = y if part is None else part + y
        acc_ref[...] += part

    @pl.when(f == pl.num_programs(1) - 1)
    def _():
        o_ref[...] = acc_ref[...]


def _swiglu(x, w_gu, w_d, block_e, n_valid, tm, tf):
    rows = x.shape[0]
    F = w_d.shape[1]
    nf = F // tf
    assert rows % tm == 0 and F % tf == 0
    grid_spec = pltpu.PrefetchScalarGridSpec(
        num_scalar_prefetch=2,
        grid=(rows // tm, nf),
        in_specs=[pl.BlockSpec((tm, D), lambda i, f, be, nv: (i, 0)),
                  pl.BlockSpec((1, D, tf), lambda i, f, be, nv: (be[i], 0, f)),
                  pl.BlockSpec((1, D, tf), lambda i, f, be, nv: (be[i], 0, nf + f)),
                  pl.BlockSpec((1, tf, D), lambda i, f, be, nv: (be[i], f, 0))],
        out_specs=pl.BlockSpec((tm, D), lambda i, f, be, nv: (i, 0)),
        scratch_shapes=[pltpu.VMEM((tm, D), F32)],
    )
    return pl.pallas_call(
        _swiglu_kernel,
        grid_spec=grid_spec,
        out_shape=jax.ShapeDtypeStruct((rows, D), F32),
        compiler_params=_params(2),
    )(block_e, n_valid, x, w_gu, w_gu, w_d)


def _gla_in_kernel(x1_ref, f_ref, g2_ref, ng3_ref, ng0_ref, sc_ref, sh_ref, wqkv_ref, wg1_ref, wg2_ref, bg_ref,
                   wr_ref, br_ref, x2_ref, qkv_ref, g_ref, r_ref):
    bb, tl, _ = x1_ref.shape
    tm = bb * tl
    x2 = x1_ref[...] + g2_ref[...] * _rms(f_ref[...], ng3_ref[...])
    x2_ref[...] = x2
    h = (_rms(x2, ng0_ref[...]) * (1.0 + sc_ref[...]) + sh_ref[...]).reshape(tm, D).astype(BF)
    for c in range(4):
        cols = slice(c * KD, (c + 1) * KD)
        part = _dot(h, wqkv_ref[:, cols])
        if c == 0:
            part = part * (DK ** -0.5)
        qkv_ref[:, :, cols] = part.astype(BF).reshape(bb, tl, KD)
    low = _dot(h, wg1_ref[...]).astype(BF)
    a = _dot(low, wg2_ref[...]) + bg_ref[...]
    log_sig = jnp.minimum(a, 0.0) - jnp.log(1.0 + jnp.exp(-jnp.abs(a)))
    g_ref[...] = (log_sig / GLA_GATE_NORM).reshape(bb, tl, 2 * KD)
    r_ref[...] = _silu(_dot(h, wr_ref[...]) + br_ref[...]).astype(BF).reshape(bb, tl, D)


def _gla_in(x1, f, g2, ng3, ng0, sc, sh, qkv_w, gk_w1, gk_w2, gk_b, r_w, r_b):
    B, L, _ = x1.shape
    r = _Rows(B, L)
    w1 = jnp.zeros((D, LANES), F32).at[:, :GLA_RANK].set(gk_w1[0]).at[:, GLA_RANK:2 * GLA_RANK].set(gk_w1[1])
    w2 = jnp.zeros((LANES, 2 * KD), F32).at[:GLA_RANK, :KD].set(gk_w2[0]).at[GLA_RANK:2 * GLA_RANK, KD:].set(gk_w2[1])
    return pl.pallas_call(
        _gla_in_kernel,
        grid=(r.n,),
        in_specs=[r.act(D), r.act(D), r.mod(g2), _const((1, 1, D)), _const((1, 1, D)), r.mod(sc), r.mod(sh),
                  _const((D, 2 * KD + D)), _const((D, LANES)), _const((LANES, 2 * KD)), _const((1, 2 * KD)),
                  _const((D, D)), _const((1, D))],
        out_specs=[r.act(D), r.act(2 * KD + D), r.act(2 * KD), r.act(D)],
        out_shape=[jax.ShapeDtypeStruct((B, L, D), F32), jax.ShapeDtypeStruct((B, L, 2 * KD + D), BF),
                   jax.ShapeDtypeStruct((B, L, 2 * KD), F32), jax.ShapeDtypeStruct((B, L, D), BF)],
        compiler_params=_params(1),
    )(x1, f.reshape(B, L, D), g2, ng3.reshape(1, 1, D), ng0.reshape(1, 1, D), sc, sh, qkv_w.astype(BF),
      w1.astype(BF), w2.astype(BF), gk_b.reshape(1, 2 * KD), r_w.astype(BF), r_b.reshape(1, D))


def _gla_scan_kernel(*refs, zero_init):
    if zero_init:
        qf_ref, qb_ref, gf_ref, gb_ref, of_ref, ob_ref, sn_ref, st_ref = refs
    else:
        qf_ref, qb_ref, gf_ref, gb_ref, s0_ref, of_ref, ob_ref, sn_ref, st_ref = refs
    j = pl.program_id(1)
    tl = qf_ref.shape[1]
    n_chunk = tl // CHUNK

    @pl.when(j == 0)
    def _():
        for d in range(2):
            for h in range(H):
                if zero_init:
                    st_ref[d, h] = jnp.zeros((DV, DK), F32)
                else:
                    st_ref[d, h] = s0_ref[0, 0, d, h].T

    ri = lax.broadcasted_iota(jnp.int32, (CHUNK, CHUNK), 0)
    ci = lax.broadcasted_iota(jnp.int32, (CHUNK, CHUNK), 1)
    masks = (ci <= ri, ci >= ri)

    def chunk(d, q_ref, g_ref, o_ref, row0):
        rows = pl.ds(row0, CHUNK)
        mask = masks[d]
        tri = jnp.where(mask, 1.0, 0.0).astype(BF)
        g = g_ref[0, rows, :]
        g_hi = g.astype(BF)
        g_lo = (g - g_hi.astype(F32)).astype(BF)
        b = _dot(tri, g_hi) + _dot(tri, g_lo)
        b_last = b[CHUNK - 1:CHUNK, :] if d == 0 else b[0:1, :]
        e_pos = jnp.exp(b)
        e_neg = jnp.exp(-b)
        e_rem = jnp.exp(b_last - b)
        decay = jnp.exp(b_last)
        for h in range(H):
            ks = slice(h * DK, (h + 1) * DK)
            q = q_ref[0, rows, h * DK:(h + 1) * DK].astype(F32)
            k = q_ref[0, rows, KD + h * DK:KD + (h + 1) * DK].astype(F32)
            v = q_ref[0, rows, 2 * KD + h * DV:2 * KD + (h + 1) * DV]
            qd = (q * e_pos[:, ks]).astype(BF)
            kd = (k * e_neg[:, ks]).astype(BF)
            k_state = (k * e_rem[:, ks]).astype(BF)
            att = jnp.where(mask, _dot_nt(qd, kd), 0.0).astype(BF)
            s_t = st_ref[d, h]
            o = _dot(att, v) + _dot_nt(qd, s_t.astype(BF))
            o_ref[0, rows, h * DV:(h + 1) * DV] = o.astype(o_ref.dtype)
            st_ref[d, h] = s_t * decay[:, ks] + _dot_tn(v, k_state)

    def body(c, carry):
        chunk(0, qf_ref, gf_ref, of_ref, pl.multiple_of(c * CHUNK, CHUNK))
        chunk(1, qb_ref, gb_ref, ob_ref, pl.multiple_of((n_chunk - 1 - c) * CHUNK, CHUNK))
        return carry

    lax.fori_loop(0, n_chunk, body, 0)

    @pl.when(j == pl.num_programs(1) - 1)
    def _():
        for d in range(2):
            for h in range(H):
                sn_ref[0, 0, d, h] = st_ref[d, h].T


def _gla_scan(qkv, g, s0):
    B, L, _ = qkv.shape
    tl = min(L, ROW_TILE)
    nl = L // tl
    zero_init = s0 is None
    wq = 2 * KD + D
    in_specs = [pl.BlockSpec((1, tl, wq), lambda b, j: (b, j, 0)),
                pl.BlockSpec((1, tl, wq), lambda b, j: (b, nl - 1 - j, 0)),
                pl.BlockSpec((1, tl, KD), lambda b, j: (b, j, 0)),
                pl.BlockSpec((1, tl, KD), lambda b, j: (b, nl - 1 - j, 1))]
    args = [qkv, qkv, g, g]
    st_spec = pl.BlockSpec((1, 1, 2, H, DK, DV), lambda b, j: (b, 0, 0, 0, 0, 0))
    if not zero_init:
        in_specs.append(st_spec)
        args.append(s0)
    return pl.pallas_call(
        functools.partial(_gla_scan_kernel, zero_init=zero_init),
        grid=(B, nl),
        in_specs=in_specs,
        out_specs=[pl.BlockSpec((1, tl, D), lambda b, j: (b, j, 0)),
                   pl.BlockSpec((1, tl, D), lambda b, j: (b, nl - 1 - j, 0)), st_spec],
        out_shape=[jax.ShapeDtypeStruct((B, L, D), BF), jax.ShapeDtypeStruct((B, L, D), BF),
                   jax.ShapeDtypeStruct((B, 1, 2, H, DK, DV), F32)],
        scratch_shapes=[pltpu.VMEM((2, H, DV, DK), F32)],
        compiler_params=_params(2),
    )(*args)


def _gla_out_kernel(of_ref, ob_ref, r_ref, on_ref, w_ref, x_ref, g1_ref, ng1_ref, ng2_ref, sc_ref, sh_ref, rw_ref,
                    x3_ref, h_ref, route_ref, rt_ref, cnt_ref):
    bb, tl, _ = x_ref.shape
    tm = bb * tl
    o = of_ref[...].astype(F32) + ob_ref[...].astype(F32)
    heads = []
    for h in range(H):
        oh = o[:, :, h * DV:(h + 1) * DV]
        heads.append(oh * lax.rsqrt(jnp.mean(oh * oh, axis=-1, keepdims=True) + RMS_EPS))
    o = jnp.concatenate(heads, -1) * on_ref[...] * r_ref[...].astype(F32)
    m = _dot(o.reshape(tm, D).astype(BF), w_ref[...])
    x3 = x_ref[...] + g1_ref[...] * _rms(m, ng1_ref[0]).reshape(bb, tl, D)
    x3_ref[...] = x3
    h2 = _rms(x3, ng2_ref[...]) * (1.0 + sc_ref[...]) + sh_ref[...]
    h_hi = h2.astype(BF)
    h_ref[...] = h_hi
    h_lo = (h2 - h_hi.astype(F32)).reshape(tm, D).astype(BF)
    both = _dot(h_hi.reshape(tm, D), rw_ref[...])
    logits = both[:, :LANES] + both[:, LANES:] + _dot(h_lo, rw_ref[:, :LANES])
    lane_i = lax.broadcasted_iota(jnp.int32, (tm, LANES), 1)
    lane = lane_i.astype(F32)
    logits = jnp.where(lane_i < N_EXPERTS, logits, -jnp.inf)
    m0 = jnp.max(logits, axis=-1, keepdims=True)
    i0 = jnp.min(jnp.where(logits == m0, lane, float(LANES)), axis=-1, keepdims=True)
    rest = jnp.where(lane == i0, -jnp.inf, logits)
    m1 = jnp.max(rest, axis=-1, keepdims=True)
    i1 = jnp.min(jnp.where(rest == m1, lane, float(LANES)), axis=-1, keepdims=True)
    e = jnp.exp(m1 - m0)
    w0 = 1.0 / (1.0 + e)
    w1 = e * w0
    onehot = jnp.where((lane == i0) | (lane == i1), 1.0, 0.0)
    ri = lax.broadcasted_iota(jnp.int32, (tm, tm), 0)
    ci = lax.broadcasted_iota(jnp.int32, (tm, tm), 1)
    before = jnp.where(ci < ri, 1.0, 0.0).astype(BF)
    prior = _dot(before, onehot.astype(BF))
    rank0 = jnp.sum(jnp.where(lane == i0, prior, 0.0), axis=-1, keepdims=True)
    rank1 = jnp.sum(jnp.where(lane == i1, prior, 0.0), axis=-1, keepdims=True)
    cnt_ref[0] = jnp.broadcast_to(jnp.sum(onehot, axis=0, keepdims=True), cnt_ref.shape[1:])
    fields = (w0, w1, i0, i1, rank0, rank1)
    route = jnp.zeros((tm, LANES), F32)
    for n, val in enumerate(fields):
        route = jnp.where(lane_i == n, val, route)
    route_ref[...] = route
    rt_ref[0] = route.T[0:SUBLANES, :]


def _gla_out(o_f, o_b, r_gate, onorm_g, w, x, g1, ng1, ng2, sc, sh, router_w):
    B, L, _ = x.shape
    r = _Rows(B, L)
    rw = jnp.pad(router_w, ((0, 0), (0, LANES - N_EXPERTS)))
    rw_hi = rw.astype(BF)
    rw = jnp.concatenate([rw_hi, (rw - rw_hi.astype(F32)).astype(BF)], axis=1)
    tm = r.tm
    return pl.pallas_call(
        _gla_out_kernel,
        grid=(r.n,),
        in_specs=[r.act(D), r.act(D), r.act(D), _const((1, 1, D)), _const((D, D)), r.act(D), r.mod(g1),
                  _const((1, 1, D)), _const((1, 1, D)), r.mod(sc), r.mod(sh), _const((D, 2 * LANES))],
        out_specs=[r.act(D), r.act(D), pl.BlockSpec((tm, LANES), lambda i: (i, 0)),
                   pl.BlockSpec((1, SUBLANES, tm), lambda i: (i, 0, 0)),
                   pl.BlockSpec((1, SUBLANES, LANES), lambda i: (i, 0, 0))],
        out_shape=[jax.ShapeDtypeStruct((B, L, D), F32), jax.ShapeDtypeStruct((B, L, D), BF),
                   jax.ShapeDtypeStruct((B * L, LANES), F32), jax.ShapeDtypeStruct((r.n, SUBLANES, tm), F32),
                   jax.ShapeDtypeStruct((r.n, SUBLANES, LANES), F32)],
        compiler_params=_params(1),
    )(o_f, o_b, r_gate, jnp.tile(onorm_g, H).reshape(1, 1, D), w.astype(BF), x, g1, ng1.reshape(1, 1, D),
      ng2.reshape(1, 1, D), sc, sh, rw)


SEG = 128
MAIN = 2


def _seg_start(seg_ref, step, e):
    return pl.multiple_of(seg_ref[step * N_EXPERTS + e], SUBLANES)


def _dispatch_kernel(seg_ref, cnt_ref, h_ref, rt_ref, xs_in_hbm, xs_hbm, stage, sem):
    del xs_in_hbm
    i = pl.program_id(0)
    tm = h_ref.shape[0]
    n_piece = tm // SEG
    h = h_ref[...]
    assert n_piece == 2 * MAIN
    ex0, ex1, lr0, lr1 = (jnp.broadcast_to(rt_ref[0, n:n + 1, :], (MAIN * SEG, tm)) for n in range(2, 6))
    rank_row = lax.broadcasted_iota(jnp.int32, (MAIN * SEG, tm), 0).astype(F32)

    def copy(e, k):
        dst = xs_hbm.at[pl.ds(_seg_start(seg_ref, i, e) + k * SEG, SEG)]
        return pltpu.make_async_copy(stage.at[e, pl.ds(k * SEG, SEG)], dst, sem.at[e * n_piece + k])

    def live(e, k):
        return k * SEG < cnt_ref[i * N_EXPERTS + e]

    for e in range(N_EXPERTS):
        ef = float(e)
        lr = jnp.where(ex0 == ef, lr0, jnp.where(ex1 == ef, lr1, -1.0))
        pick = jnp.where(lr == rank_row, 1.0, 0.0).astype(BF)
        stage[e, 0:MAIN * SEG] = _dot(pick, h)

        @pl.when(live(e, MAIN))
        def _(e=e, lr=lr):
            more = jnp.where(lr - float(MAIN * SEG) == rank_row, 1.0, 0.0).astype(BF)
            stage[e, MAIN * SEG:2 * MAIN * SEG] = _dot(more, h)

        for k in range(n_piece):
            @pl.when(live(e, k))
            def _(e=e, k=k):
                copy(e, k).start()

    for e in range(N_EXPERTS):
        for k in range(n_piece):
            @pl.when(live(e, k))
            def _(e=e, k=k):
                copy(e, k).wait()


def _dispatch(h2, route_t, seg, cnt, n_rows, tm=ROW_TILE):
    T = h2.shape[0]
    grid_spec = pltpu.PrefetchScalarGridSpec(
        num_scalar_prefetch=2,
        grid=(T // tm,),
        in_specs=[pl.BlockSpec((tm, D), lambda i, s, c: (i, 0)),
                  pl.BlockSpec((1, SUBLANES, tm), lambda i, s, c: (i, 0, 0)),
                  pl.BlockSpec(memory_space=pl.ANY)],
        out_specs=pl.BlockSpec(memory_space=pl.ANY),
        scratch_shapes=[pltpu.VMEM((N_EXPERTS, tm, D), F32),
                        pltpu.SemaphoreType.DMA((N_EXPERTS * tm // SEG,))],
    )
    return pl.pallas_call(
        _dispatch_kernel,
        grid_spec=grid_spec,
        out_shape=jax.ShapeDtypeStruct((n_rows, D), F32),
        input_output_aliases={4: 0},
        compiler_params=_params(1),
    )(seg, cnt, h2, route_t, jnp.zeros((n_rows, D), F32))


def _combine_kernel(seg_ref, cnt_ref, ys_hbm, route_ref, x_ref, g2_ref, ng_ref, o_ref, gbuf, extra, acc_ref, sem,
                    esem):
    i = pl.program_id(0)
    bb, tl, _ = x_ref.shape
    tm = bb * tl
    n_piece = tm // SEG

    def main_copy(step, slot, e, k):
        src = ys_hbm.at[pl.ds(_seg_start(seg_ref, step, e) + k * SEG, SEG)]
        return pltpu.make_async_copy(src, gbuf.at[slot, pl.ds((e * MAIN + k) * SEG, SEG)], sem.at[slot, e * MAIN + k])

    def fetch(step, slot):
        for e in range(N_EXPERTS):
            for k in range(MAIN):
                main_copy(step, slot, e, k).start()

    @pl.when(i == 0)
    def _():
        fetch(0, 0)

    @pl.when(i + 1 < pl.num_programs(0))
    def _():
        fetch(i + 1, (i + 1) % 2)

    slot = i % 2
    route = route_ref[...]

    def spread_cols(e, r0, width):
        ef = float(e)
        gate = jnp.where(route[:, 2:3] == ef, route[:, 0:1], jnp.where(route[:, 3:4] == ef, route[:, 1:2], 0.0))
        rank = jnp.where(route[:, 2:3] == ef, route[:, 4:5], jnp.where(route[:, 3:4] == ef, route[:, 5:6], -1.0))
        col = lax.broadcasted_iota(jnp.int32, (tm, width), 1).astype(F32) + float(r0)
        return (jnp.where(rank == col, 1.0, 0.0) * gate).astype(BF)

    spread = jnp.concatenate([spread_cols(e, 0, MAIN * SEG) for e in range(N_EXPERTS)], axis=1)
    for e in range(N_EXPERTS):
        for k in range(MAIN):
            main_copy(i, slot, e, k).wait()
    acc_ref[...] = _dot(spread, gbuf[slot].astype(BF))
    for e in range(N_EXPERTS):
        for k in range(MAIN, n_piece):
            @pl.when(k * SEG < cnt_ref[i * N_EXPERTS + e])
            def _(e=e, k=k):
                src = ys_hbm.at[pl.ds(_seg_start(seg_ref, i, e) + k * SEG, SEG)]
                cp = pltpu.make_async_copy(src, extra, esem)
                cp.start()
                more = spread_cols(e, k * SEG, SEG)
                cp.wait()
                acc_ref[...] += _dot(more, extra[...].astype(BF))

    o_ref[...] = x_ref[...] + g2_ref[...] * _rms(acc_ref[...], ng_ref[0]).reshape(bb, tl, D)


def _combine(ys, route, seg, cnt, x, g2, ng):
    B, L, _ = x.shape
    r = _Rows(B, L)
    tm = r.tm
    nl = r.nl
    act = pl.BlockSpec((r.bb, r.tl, D), lambda i, s, c: (i // nl, i % nl, 0))
    g2_spec = (pl.BlockSpec((1, 1, D), lambda i, s, c: (0, 0, 0)) if g2.shape[0] == 1 else
               pl.BlockSpec((r.bb, 1, D), lambda i, s, c: (i // nl, 0, 0)))
    grid_spec = pltpu.PrefetchScalarGridSpec(
        num_scalar_prefetch=2,
        grid=(r.n,),
        in_specs=[pl.BlockSpec(memory_space=pl.ANY), pl.BlockSpec((tm, LANES), lambda i, s, c: (i, 0)), act, g2_spec,
                  pl.BlockSpec((1, 1, D), lambda i, s, c: (0, 0, 0))],
        out_specs=act,
        scratch_shapes=[pltpu.VMEM((2, N_EXPERTS * MAIN * SEG, D), F32), pltpu.VMEM((SEG, D), F32),
                        pltpu.VMEM((tm, D), F32), pltpu.SemaphoreType.DMA((2, N_EXPERTS * MAIN)),
                        pltpu.SemaphoreType.DMA(())],
    )
    return pl.pallas_call(
        _combine_kernel,
        grid_spec=grid_spec,
        out_shape=jax.ShapeDtypeStruct((B, L, D), F32),
        compiler_params=_params(1),
    )(seg, cnt, ys, route, x, g2, ng.reshape(1, 1, D))


def _moe(h2, route, route_t, blk_cnt, x3, g2, ng, w_gu, w_d, tmoe):
    B, L, _ = h2.shape
    T = B * L
    cntb = blk_cnt[:, 0, :N_EXPERTS].astype(jnp.int32)
    n_tok_blocks = cntb.shape[0]
    held = (cntb + SUBLANES - 1) // SUBLANES * SUBLANES
    before = jnp.cumsum(held, axis=0) - held
    cnt = jnp.sum(held, axis=0)
    p_cnt = (cnt + MAIN * SEG + tmoe - 1) // tmoe * tmoe
    p_end = jnp.cumsum(p_cnt)
    p_start = p_end - p_cnt
    seg = (p_start[None, :] + before).reshape(-1)
    n_rows = 2 * T + N_EXPERTS * (n_tok_blocks * (SUBLANES - 1) + tmoe + MAIN * SEG)
    n_rows = (n_rows + tmoe - 1) // tmoe * tmoe
    n_blocks = n_rows // tmoe
    starts = jnp.arange(n_blocks, dtype=jnp.int32) * tmoe
    block_e = jnp.minimum(jnp.sum(starts[:, None] >= p_end[None, :], axis=-1), N_EXPERTS - 1).astype(jnp.int32)
    n_valid = (p_end[-1:] // tmoe).astype(jnp.int32)
    xs = _dispatch(h2.reshape(T, D), route_t, seg, cntb.reshape(-1), n_rows)
    ys = _swiglu(xs, w_gu, w_d, block_e, n_valid, tmoe, D_FF_EXPERT // 2)
    return _combine(ys, route, seg, cntb.reshape(-1), x3, g2, ng)


def _trunk(x, mods, n_row, s0, p):
    B, L, _ = x.shape
    T = B * L
    ng = p['norm_g']
    sh1, sc1, g1, sh2, sc2, g2 = mods[0]
    v, x0 = _hy_in(x, ng[0, 0], sc1, sh1, p['hy_in_w'][0], p['hy_in_b'][0], p['hy_sc_w'][0], p['hy_sc_b'][0], n_row)
    kern = _hyena_kernel_taps(L, p['hy_f_w1'][0], p['hy_f_b1'][0], p['hy_f_w2'][0], p['hy_f_b2'][0],
                              p['hy_f_freq'][0], p['hy_f_w3'][0])
    if L <= 512:
        vx = _fftconv_short(v, x0, kern, p['hy_skip'][0])
    else:
        vx = _fftconv_long(v, x0, kern, p['hy_skip'][0])
    x1, h2 = _hy_out(vx, p['hy_out_w'][0], p['hy_out_b'][0], x, g1, ng[0, 1], ng[0, 2], sc2, sh2)
    tm = 1024
    ones = jnp.zeros((T // tm,), jnp.int32)
    f = _swiglu(h2.reshape(T, D), p['ffn_wgu'], p['ffn_wd'], ones, jnp.full((1,), T // tm, jnp.int32), tm, D_FF)
    g2_0, ng3_0 = g2, ng[0, 3]
    sh1, sc1, g1, sh2, sc2, g2 = mods[1]
    x2, qkv, gate, r_gate = _gla_in(x1, f, g2_0, ng3_0, ng[1, 0], sc1, sh1, p['gla_qkv_w'][0], p['gla_gk_w1'][0],
                                    p['gla_gk_w2'][0], p['gla_gk_b'][0], p['gla_r_w'][0], p['gla_r_b'][0])
    o_f, o_b, s_new = _gla_scan(qkv, gate, s0)
    x3, h2, route, route_t, counts = _gla_out(o_f, o_b, r_gate, p['gla_onorm_g'][0], p['gla_out_w'][0], x2, g1, ng[1, 1],
                                     ng[1, 2], sc2, sh2, p['moe_router'][0])
    x4 = _moe(h2, route, route_t, counts, x3, g2, ng[1, 3], p['moe_wgu'], p['moe_wd'], 1024 if T >= 16384 else 512)
    return x4, s_new


def kernel(x_prompt, x_sample, state_gla, c, c_ctx, ada_w, ada_b, norm_g, hy_in_w, hy_in_b, hy_sc_w, hy_sc_b, hy_f_w1, hy_f_b1, hy_f_w2, hy_f_b2, hy_f_freq, hy_f_w3, hy_skip, hy_out_w, hy_out_b, gla_qkv_w, gla_gk_w1, gla_gk_w2, gla_gk_b, gla_r_w, gla_r_b, gla_onorm_g, gla_out_w, ffn_wgu, ffn_wd, moe_router, moe_wgu, moe_wd):
    p = dict(norm_g=norm_g, hy_in_w=hy_in_w, hy_in_b=hy_in_b, hy_sc_w=hy_sc_w, hy_sc_b=hy_sc_b, hy_f_w1=hy_f_w1,
             hy_f_b1=hy_f_b1, hy_f_w2=hy_f_w2, hy_f_b2=hy_f_b2, hy_f_freq=hy_f_freq, hy_f_w3=hy_f_w3,
             hy_skip=hy_skip, hy_out_w=hy_out_w, hy_out_b=hy_out_b, gla_qkv_w=gla_qkv_w, gla_gk_w1=gla_gk_w1,
             gla_gk_w2=gla_gk_w2, gla_gk_b=gla_gk_b, gla_r_w=gla_r_w, gla_r_b=gla_r_b, gla_onorm_g=gla_onorm_g,
             gla_out_w=gla_out_w, ffn_wgu=ffn_wgu.astype(BF), ffn_wd=ffn_wd.astype(BF),
             moe_router=moe_router, moe_wgu=moe_wgu[0].astype(BF), moe_wd=moe_wd[0].astype(BF))
    n_dec = c.shape[0]
    cond = jnp.concatenate([c_ctx[None, :], c, jnp.zeros((16 - 1 - n_dec, D), F32)], axis=0)
    mod = _ada(cond, ada_w, ada_b)
    mods_ctx = [[m[:, None, :] for m in jnp.split(mod[l, 0:1], 6, axis=-1)] for l in range(DEPTH)]
    mods_dec = [[m[:, None, :] for m in jnp.split(mod[l, 1:1 + n_dec], 6, axis=-1)] for l in range(DEPTH)]
    y_prompt, state_new = _trunk(x_prompt, mods_ctx, x_prompt.shape[1], None, p)
    grid_w = 64
    y_sample, _ = _trunk(x_sample, mods_dec, grid_w, state_gla, p)
    return y_prompt, y_sample, state_new
```

```python
import functools
import math

import jax
import jax.numpy as jnp
from jax import lax
from jax.experimental import pallas as pl
from jax.experimental.pallas import tpu as pltpu

F32 = jnp.float32
BF = jnp.bfloat16

D = 1024
RMS_EPS = 1e-6
DEPTH = 2
HY_SHORT = 3
HY_EMB = 33
HY_BANDS = (HY_EMB - 1) // 2
HY_FFN = 64
HY_MAX_DECAY = math.log(1e-2) / 0.3
HY_MIN_DECAY = math.log(1e-2) / 1.5
H = 4
DK = 128
DV = 256
KD = H * DK
GLA_RANK = 16
GLA_GATE_NORM = 16.0
CHUNK = 64
D_FF = 11 * D // 4
N_EXPERTS = 8
D_FF_EXPERT = 7 * D // 2

LANES = 128
SUBLANES = 8
VMEM_LIMIT_BYTES = 56 * 1024 * 1024
ROW_TILE = 512
FFT_S2 = 128


def _params(n_axes):
    return pltpu.CompilerParams(dimension_semantics=("arbitrary",) * n_axes,
                                vmem_limit_bytes=VMEM_LIMIT_BYTES)


def _dot(a, b):
    return jnp.dot(a, b, preferred_element_type=F32)


def _dot_nt(a, b):
    return lax.dot_general(a, b, (((1,), (1,)), ((), ())), preferred_element_type=F32)


def _dot_tn(a, b):
    return lax.dot_general(a, b, (((0,), (0,)), ((), ())), preferred_element_type=F32)


def _rms(x, g):
    return x * lax.rsqrt(jnp.mean(x * x, axis=-1, keepdims=True) + RMS_EPS) * g


def _silu(x):
    return x * (1.0 / (1.0 + jnp.exp(-x)))


class _Rows:
    def __init__(self, B, L, tm=ROW_TILE):
        self.B, self.L = B, L
        if L >= tm:
            self.bb, self.tl = 1, tm
        else:
            self.bb, self.tl = tm // L, L
        assert L % self.tl == 0 and B % self.bb == 0
        self.nl = L // self.tl
        self.n = (B // self.bb) * self.nl
        self.tm = self.bb * self.tl

    def act(self, width):
        nl = self.nl
        return pl.BlockSpec((self.bb, self.tl, width), lambda i: (i // nl, i % nl, 0))

    def mod(self, m):
        nl = self.nl
        if m.shape[0] == 1:
            return pl.BlockSpec((1, 1, D), lambda i: (0, 0, 0))
        return pl.BlockSpec((self.bb, 1, D), lambda i: (i // nl, 0, 0))


def _const(shape):
    nd = len(shape)
    return pl.BlockSpec(shape, lambda *_: (0,) * nd)


def _ada_kernel(c_ref, w_ref, b_ref, o_ref):
    cs = _silu(c_ref[...])
    o_ref[0] = _dot(cs.astype(BF), w_ref[0].astype(BF)) + b_ref[0]


def _ada(cond, ada_w, ada_b):
    R = cond.shape[0]
    tn = 1536
    return pl.pallas_call(
        _ada_kernel,
        grid=(DEPTH, 6 * D // tn),
        in_specs=[pl.BlockSpec((R, D), lambda l, n: (0, 0)),
                  pl.BlockSpec((1, D, tn), lambda l, n: (l, 0, n)),
                  pl.BlockSpec((1, 1, tn), lambda l, n: (l, 0, n))],
        out_specs=pl.BlockSpec((1, R, tn), lambda l, n: (l, 0, n)),
        out_shape=jax.ShapeDtypeStruct((DEPTH, R, 6 * D), F32),
        compiler_params=_params(2),
    )(cond, ada_w, ada_b.reshape(DEPTH, 1, 6 * D))


def _hy_in_kernel(x_ref, ng_ref, sc_ref, sh_ref, w_ref, b_ref, cw_ref, cb_ref, v_ref, x0_ref, *, n_row):
    x = x_ref[...]
    bb, tl, _ = x.shape
    tm = bb * tl
    h = _rms(x, ng_ref[...]) * (1.0 + sc_ref[...]) + sh_ref[...]
    hb = h.reshape(tm, D).astype(BF)
    pos = lax.broadcasted_iota(jnp.int32, (tm, D), 0) & (n_row - 1)
    first = pos == 0
    last = pos == n_row - 1
    parts = []
    for j in range(3):
        cols = slice(j * D, (j + 1) * D)
        u = _dot(hb, w_ref[:, cols]) + b_ref[:, cols]
        up = jnp.where(first, 0.0, pltpu.roll(u, 1, 0))
        dn = jnp.where(last, 0.0, pltpu.roll(u, tm - 1, 0))
        parts.append(cb_ref[:, cols] + up * cw_ref[0:1, cols] + u * cw_ref[1:2, cols] + dn * cw_ref[2:3, cols])
    x0, x1, v = parts
    v_ref[...] = (v * x1).astype(BF).reshape(bb, tl, D)
    x0_ref[...] = x0.astype(BF).reshape(bb, tl, D)


def _hy_in(x, ng, sc, sh, w, b, cw, cb, n_row):
    B, L, _ = x.shape
    assert n_row & (n_row - 1) == 0
    r = _Rows(B, L)
    assert r.tl % n_row == 0
    return pl.pallas_call(
        functools.partial(_hy_in_kernel, n_row=n_row),
        grid=(r.n,),
        in_specs=[r.act(D), _const((1, 1, D)), r.mod(sc), r.mod(sh),
                  _const((D, 3 * D)), _const((1, 3 * D)), _const((HY_SHORT, 3 * D)), _const((1, 3 * D))],
        out_specs=[r.act(D), r.act(D)],
        out_shape=[jax.ShapeDtypeStruct((B, L, D), BF)] * 2,
        compiler_params=_params(1),
    )(x, ng.reshape(1, 1, D), sc, sh, w.astype(BF), b.reshape(1, 3 * D), cw, cb.reshape(1, 3 * D))


def _filter_kernel(z_ref, w1_ref, b1_ref, w2_ref, b2_ref, fr_ref, w3_ref, dl_ref, o_ref, *, L):
    tr = z_ref.shape[0]
    hp = lax.Precision.HIGHEST
    z = z_ref[...]
    h = jnp.sin(fr_ref[0:1, :] * (jnp.dot(z, w1_ref[...], precision=hp, preferred_element_type=F32) + b1_ref[...]))
    h = jnp.sin(fr_ref[1:2, :] * (jnp.dot(h, w2_ref[...], precision=hp, preferred_element_type=F32) + b2_ref[...]))
    hw = _dot(h.astype(BF), w3_ref[...])
    n = pl.program_id(0) * tr + lax.broadcasted_iota(jnp.int32, (tr, D), 0)
    taps = jnp.where(n < L, hw[:, :D], hw[:, D:]) * jnp.exp(-z[:, 0:1] * dl_ref[...])
    o_ref[...] = jnp.where(n == L, 0.0, taps)


def _hyena_kernel_taps(L, w1, b1, w2, b2, freq, w3):
    n = jnp.arange(2 * L, dtype=jnp.int32)
    pos = jnp.where(n < L, n, 2 * L - n) % L
    t = (pos.astype(F32) / (L - 1))[:, None]
    w = (2.0 * math.pi * pos.astype(F32) / L)[:, None]
    f = jnp.linspace(1e-4, HY_BANDS - 1, HY_BANDS, dtype=F32)[None, :]
    z = jnp.concatenate([t, jnp.cos(f * w), -jnp.sin(f * w)], axis=-1)
    z = jnp.pad(z, ((0, 0), (0, LANES - HY_EMB)))
    pad = LANES - HY_FFN
    w1p = jnp.pad(w1, ((0, LANES - HY_EMB), (0, pad)))
    w2p = jnp.pad(w2, ((0, pad), (0, pad)))
    w3p = jnp.pad(w3, ((0, pad), (0, 0))).astype(BF)
    b1p = jnp.pad(b1, (0, pad)).reshape(1, LANES)
    b2p = jnp.pad(b2, (0, pad)).reshape(1, LANES)
    frp = jnp.pad(freq, ((0, 0), (0, pad)))
    deltas = jnp.abs(jnp.linspace(HY_MIN_DECAY, HY_MAX_DECAY, D, dtype=F32)).reshape(1, D)
    tr = 512
    return pl.pallas_call(
        functools.partial(_filter_kernel, L=L),
        grid=(2 * L // tr,),
        in_specs=[pl.BlockSpec((tr, LANES), lambda i: (i, 0)), _const((LANES, LANES)), _const((1, LANES)),
                  _const((LANES, LANES)), _const((1, LANES)), _const((2, LANES)), _const((LANES, 2 * D)),
                  _const((1, D))],
        out_specs=pl.BlockSpec((tr, D), lambda i: (i, 0)),
        out_shape=jax.ShapeDtypeStruct((2 * L, D), F32),
        compiler_params=_params(1),
    )(z, w1p, b1p, w2p, b2p, frp, w3p, deltas)


def _cis(rows, cols, n, sign, scale=1.0):
    ph = (rows[:, None] * cols[None, :]) % n
    ang = ph.astype(F32) * (2.0 * math.pi / n)
    return jnp.cos(ang) * scale, jnp.sin(ang) * (sign * scale)


def _cplx_block(cr, ci):
    return jnp.concatenate([jnp.concatenate([cr, -ci], 1), jnp.concatenate([ci, cr], 1)], 0)


def _fft_short_kernel(v_ref, x0_ref, kern_ref, mk_ref, mf_ref, mi_ref, sk_ref, o_ref, ks_ref, *, L):
    n2 = 2 * L

    @pl.when(pl.program_id(0) == 0)
    def _():
        ks_ref[...] = _dot(mk_ref[...], kern_ref[...].astype(BF))

    z = jnp.concatenate([v_ref[0], v_ref[1]], 0)
    u = _dot(mf_ref[...], z)
    ur, ui = u[:n2], u[n2:]
    kr, ki = ks_ref[:n2, :], ks_ref[n2:, :]
    y = jnp.concatenate([ur * kr - ui * ki, ur * ki + ui * kr], 0).astype(BF)
    t = _dot(mi_ref[...], y)
    for j in range(2):
        conv = t[j * L:(j + 1) * L]
        o_ref[j] = ((conv + v_ref[j].astype(F32) * sk_ref[...]) * x0_ref[j].astype(F32)).astype(BF)


def _fftconv_short(v, x0, kern, skip):
    B, L, _ = v.shape
    n2 = 2 * L
    k = jnp.arange(n2, dtype=jnp.int32)
    s = jnp.arange(L, dtype=jnp.int32)
    fr, fi = _cis(k, s, n2, -1.0)
    mf = _cplx_block(fr, fi).astype(BF)
    kr, ki = _cis(k, k, n2, -1.0)
    mk = jnp.concatenate([kr, ki], 0).astype(BF)
    ir, ii = _cis(s, k, n2, 1.0, 1.0 / n2)
    mi = _cplx_block(ir, ii).astype(BF)
    pair = lambda p: (p, 0, 0)
    return pl.pallas_call(
        functools.partial(_fft_short_kernel, L=L),
        grid=(B // 2,),
        in_specs=[pl.BlockSpec((2, L, D), pair), pl.BlockSpec((2, L, D), pair), _const((n2, D)),
                  _const((2 * n2, n2)), _const((2 * n2, 2 * L)), _const((2 * L, 2 * n2)), _const((1, D))],
        out_specs=pl.BlockSpec((2, L, D), pair),
        out_shape=jax.ShapeDtypeStruct((B, L, D), BF),
        scratch_shapes=[pltpu.VMEM((2 * n2, D), F32)],
        compiler_params=_params(1),
    )(v, x0, kern, mk, mf, mi, skip.reshape(1, D))


def _lmul_kernel(m_ref, x_ref, o_ref):
    o_ref[0] = _dot(m_ref[...], x_ref[0].astype(BF)).astype(o_ref.dtype)


def _lmul(m, x, cb=8192):
    G, K, NC = x.shape
    R = m.shape[0]
    return pl.pallas_call(
        _lmul_kernel,
        grid=(G, NC // cb),
        in_specs=[pl.BlockSpec((R, K), lambda g, c: (0, 0)), pl.BlockSpec((1, K, cb), lambda g, c: (g, 0, c))],
        out_specs=pl.BlockSpec((1, R, cb), lambda g, c: (g, 0, c)),
        out_shape=jax.ShapeDtypeStruct((G, R, NC), BF),
        compiler_params=_params(2),
    )(m, x)


def _fft_mid_kernel(a_ref, af_ref, mf_ref, mi_ref, o_ref, ks_ref):
    s2 = FFT_S2
    mf = mf_ref[0]

    @pl.when(pl.program_id(1) == 0)
    def _():
        ks_ref[...] = _dot(mf, jnp.concatenate([af_ref[0, 0], af_ref[1, 0]], 0))

    u = _dot(mf, jnp.concatenate([a_ref[0, 0, 0], a_ref[0, 1, 0]], 0))
    ur, ui = u[:s2], u[s2:]
    kr, ki = ks_ref[:s2, :], ks_ref[s2:, :]
    y = jnp.concatenate([ur * kr - ui * ki, ur * ki + ui * kr], 0).astype(BF)
    z = _dot(mi_ref[0], y).astype(BF)
    o_ref[0, 0, 0] = z[:s2]
    o_ref[0, 1, 0] = z[s2:]


def _lmul_out_kernel(m_ref, z_ref, v_ref, x0_ref, sk_ref, o_ref):
    conv = _dot(m_ref[...], z_ref[0])
    o_ref[0] = ((conv + v_ref[0].astype(F32) * sk_ref[...]) * x0_ref[0].astype(F32)).astype(BF)


def _fftconv_long(v, x0, kern, skip, cb=8192):
    B, L, _ = v.shape
    n2 = 2 * L
    s2 = FFT_S2
    s1 = n2 // s2
    s1h = s1 // 2
    G = B // 2
    nc = s2 * D
    i1 = jnp.arange(s1, dtype=jnp.int32)
    i1h = jnp.arange(s1h, dtype=jnp.int32)
    i2 = jnp.arange(s2, dtype=jnp.int32)
    cr, ci = _cis(i1, i1h, s1, -1.0)
    m1 = _cplx_block(cr, ci).astype(BF)
    cr, ci = _cis(i1, i1, s1, -1.0)
    m1f = jnp.concatenate([cr, ci], 0).astype(BF)
    kk = (i1[:, None] + s1 * i2[None, :]).reshape(-1)
    gr, gi = _cis(kk, i2, n2, -1.0)
    mf = jax.vmap(_cplx_block)(gr.reshape(s1, s2, s2), gi.reshape(s1, s2, s2)).astype(BF)
    hr, hi = _cis(i2, kk, n2, 1.0, 1.0 / n2)
    hr = hr.reshape(s2, s1, s2).transpose(1, 0, 2)
    hi = hi.reshape(s2, s1, s2).transpose(1, 0, 2)
    mi = jax.vmap(_cplx_block)(hr, hi).astype(BF)
    er, ei = _cis(i1h, i1, s1, 1.0)
    m3 = _cplx_block(er, ei).astype(BF)

    a = _lmul(m1, v.reshape(G, 2 * s1h, nc), cb)
    af = _lmul(m1f, kern.reshape(1, s1, nc), cb)
    zz = pl.pallas_call(
        _fft_mid_kernel,
        grid=(s1, G),
        in_specs=[pl.BlockSpec((1, 2, 1, s2, D), lambda k, p: (p, 0, k, 0, 0)),
                  pl.BlockSpec((2, 1, s2, D), lambda k, p: (0, k, 0, 0)),
                  pl.BlockSpec((1, 2 * s2, 2 * s2), lambda k, p: (k, 0, 0)),
                  pl.BlockSpec((1, 2 * s2, 2 * s2), lambda k, p: (k, 0, 0))],
        out_specs=pl.BlockSpec((1, 2, 1, s2, D), lambda k, p: (p, 0, k, 0, 0)),
        out_shape=jax.ShapeDtypeStruct((G, 2, s1, s2, D), BF),
        scratch_shapes=[pltpu.VMEM((2 * s2, D), F32)],
        compiler_params=_params(2),
    )(a.reshape(G, 2, s1, s2, D), af.reshape(2, s1, s2, D), mf, mi)
    blk = lambda g, c: (g, 0, c)
    out = pl.pallas_call(
        _lmul_out_kernel,
        grid=(G, nc // cb),
        in_specs=[pl.BlockSpec((2 * s1h, 2 * s1), lambda g, c: (0, 0)), pl.BlockSpec((1, 2 * s1, cb), blk),
                  pl.BlockSpec((1, 2 * s1h, cb), blk), pl.BlockSpec((1, 2 * s1h, cb), blk),
                  pl.BlockSpec((1, cb), lambda g, c: (0, 0))],
        out_specs=pl.BlockSpec((1, 2 * s1h, cb), blk),
        out_shape=jax.ShapeDtypeStruct((G, 2 * s1h, nc), BF),
        compiler_params=_params(2),
    )(m3, zz.reshape(G, 2 * s1, nc), v.reshape(G, 2 * s1h, nc), x0.reshape(G, 2 * s1h, nc),
      jnp.tile(skip.reshape(1, D), (1, cb // D)))
    return out.reshape(B, L, D)


def _hy_out_kernel(vx_ref, w_ref, b_ref, x_ref, g1_ref, ng1_ref, ng2_ref, sc_ref, sh_ref, x1_ref, h_ref):
    bb, tl, _ = x_ref.shape
    m = _dot(vx_ref[...].reshape(bb * tl, D), w_ref[...]) + b_ref[...]
    x1 = x_ref[...] + g1_ref[...] * _rms(m, ng1_ref[0]).reshape(bb, tl, D)
    x1_ref[...] = x1
    h_ref[...] = (_rms(x1, ng2_ref[...]) * (1.0 + sc_ref[...]) + sh_ref[...]).astype(h_ref.dtype)


def _hy_out(vx, w, b, x, g1, ng1, ng2, sc, sh):
    B, L, _ = x.shape
    r = _Rows(B, L)
    return pl.pallas_call(
        _hy_out_kernel,
        grid=(r.n,),
        in_specs=[r.act(D), _const((D, D)), _const((1, D)), r.act(D), r.mod(g1), _const((1, 1, D)),
                  _const((1, 1, D)), r.mod(sc), r.mod(sh)],
        out_specs=[r.act(D), r.act(D)],
        out_shape=[jax.ShapeDtypeStruct((B, L, D), F32), jax.ShapeDtypeStruct((B, L, D), BF)],
        compiler_params=_params(1),
    )(vx, w.astype(BF), b.reshape(1, D), x, g1, ng1.reshape(1, 1, D), ng2.reshape(1, 1, D), sc, sh)


def _swiglu_kernel(be_ref, nv_ref, x_ref, wg_ref, wu_ref, wd_ref, o_ref, acc_ref):
    del be_ref
    f = pl.program_id(1)

    @pl.when(f == 0)
    def _():
        acc_ref[...] = jnp.zeros_like(acc_ref)

    @pl.when(pl.program_id(0) < nv_ref[0])
    def _():
        x = x_ref[...].astype(BF)
        tf = wg_ref.shape[2]
        sub = 256 if tf % 256 == 0 else tf
        part = None
        for c in range(tf // sub):
            cols = slice(c * sub, (c + 1) * sub)
            g = _dot(x, wg_ref[0, :, cols])
            u = _dot(x, wu_ref[0, :, cols])
            y = _dot((_silu(g) * u).astype(BF), wd_ref[0, cols, :])
            part = y if part is None else part + y
        acc_ref[...] += part

    @pl.when(f == pl.num_programs(1) - 1)
    def _():
        o_ref[...] = acc_ref[...]


def _swiglu(x, w_gu, w_d, block_e, n_valid, tm, tf):
    rows = x.shape[0]
    F = w_d.shape[1]
    nf = F // tf
    assert rows % tm == 0 and F % tf == 0
    grid_spec = pltpu.PrefetchScalarGridSpec(
        num_scalar_prefetch=2,
        grid=(rows // tm, nf),
        in_specs=[pl.BlockSpec((tm, D), lambda i, f, be, nv: (i, 0)),
                  pl.BlockSpec((1, D, tf), lambda i, f, be, nv: (be[i], 0, f)),
                  pl.BlockSpec((1, D, tf), lambda i, f, be, nv: (be[i], 0, nf + f)),
                  pl.BlockSpec((1, tf, D), lambda i, f, be, nv: (be[i], f, 0))],
        out_specs=pl.BlockSpec((tm, D), lambda i, f, be, nv: (i, 0)),
        scratch_shapes=[pltpu.VMEM((tm, D), F32)],
    )
    return pl.pallas_call(
        _swiglu_kernel,
        grid_spec=grid_spec,
        out_shape=jax.ShapeDtypeStruct((rows, D), F32),
        compiler_params=_params(2),
    )(block_e, n_valid, x, w_gu, w_gu, w_d)


def _gla_in_kernel(x1_ref, f_ref, g2_ref, ng3_ref, ng0_ref, sc_ref, sh_ref, wqkv_ref, wg1_ref, wg2_ref, bg_ref,
                   wr_ref, br_ref, x2_ref, qkv_ref, g_ref, r_ref):
    bb, tl, _ = x1_ref.shape
    tm = bb * tl
    x2 = x1_ref[...] + g2_ref[...] * _rms(f_ref[...], ng3_ref[...])
    x2_ref[...] = x2
    h = (_rms(x2, ng0_ref[...]) * (1.0 + sc_ref[...]) + sh_ref[...]).reshape(tm, D).astype(BF)
    for c in range(4):
        cols = slice(c * KD, (c + 1) * KD)
        part = _dot(h, wqkv_ref[:, cols])
        if c == 0:
            part = part * (DK ** -0.5)
        qkv_ref[:, :, cols] = part.astype(BF).reshape(bb, tl, KD)
    low = _dot(h, wg1_ref[...]).astype(BF)
    a = _dot(low, wg2_ref[...]) + bg_ref[...]
    log_sig = jnp.minimum(a, 0.0) - jnp.log(1.0 + jnp.exp(-jnp.abs(a)))
    g_ref[...] = (log_sig / GLA_GATE_NORM).reshape(bb, tl, 2 * KD)
    r_ref[...] = _silu(_dot(h, wr_ref[...]) + br_ref[...]).astype(BF).reshape(bb, tl, D)


def _gla_in(x1, f, g2, ng3, ng0, sc, sh, qkv_w, gk_w1, gk_w2, gk_b, r_w, r_b):
    B, L, _ = x1.shape
    r = _Rows(B, L)
    w1 = jnp.zeros((D, LANES), F32).at[:, :GLA_RANK].set(gk_w1[0]).at[:, GLA_RANK:2 * GLA_RANK].set(gk_w1[1])
    w2 = jnp.zeros((LANES, 2 * KD), F32).at[:GLA_RANK, :KD].set(gk_w2[0]).at[GLA_RANK:2 * GLA_RANK, KD:].set(gk_w2[1])
    return pl.pallas_call(
        _gla_in_kernel,
        grid=(r.n,),
        in_specs=[r.act(D), r.act(D), r.mod(g2), _const((1, 1, D)), _const((1, 1, D)), r.mod(sc), r.mod(sh),
                  _const((D, 2 * KD + D)), _const((D, LANES)), _const((LANES, 2 * KD)), _const((1, 2 * KD)),
                  _const((D, D)), _const((1, D))],
        out_specs=[r.act(D), r.act(2 * KD + D), r.act(2 * KD), r.act(D)],
        out_shape=[jax.ShapeDtypeStruct((B, L, D), F32), jax.ShapeDtypeStruct((B, L, 2 * KD + D), BF),
                   jax.ShapeDtypeStruct((B, L, 2 * KD), F32), jax.ShapeDtypeStruct((B, L, D), BF)],
        compiler_params=_params(1),
    )(x1, f.reshape(B, L, D), g2, ng3.reshape(1, 1, D), ng0.reshape(1, 1, D), sc, sh, qkv_w.astype(BF),
      w1.astype(BF), w2.astype(BF), gk_b.reshape(1, 2 * KD), r_w.astype(BF), r_b.reshape(1, D))


def _gla_scan_kernel(*refs, zero_init):
    if zero_init:
        qf_ref, qb_ref, gf_ref, gb_ref, of_ref, ob_ref, sn_ref, st_ref = refs
    else:
        qf_ref, qb_ref, gf_ref, gb_ref, s0_ref, of_ref, ob_ref, sn_ref, st_ref = refs
    j = pl.program_id(1)
    tl = qf_ref.shape[1]
    n_chunk = tl // CHUNK

    @pl.when(j == 0)
    def _():
        for d in range(2):
            for h in range(H):
                if zero_init:
                    st_ref[d, h] = jnp.zeros((DV, DK), F32)
                else:
                    st_ref[d, h] = s0_ref[0, 0, d, h].T

    ri = lax.broadcasted_iota(jnp.int32, (CHUNK, CHUNK), 0)
    ci = lax.broadcasted_iota(jnp.int32, (CHUNK, CHUNK), 1)
    masks = (ci <= ri, ci >= ri)

    def chunk(d, q_ref, g_ref, o_ref, row0):
        rows = pl.ds(row0, CHUNK)
        mask = masks[d]
        tri = jnp.where(mask, 1.0, 0.0).astype(BF)
        g = g_ref[0, rows, :]
        g_hi = g.astype(BF)
        g_lo = (g - g_hi.astype(F32)).astype(BF)
        b = _dot(tri, g_hi) + _dot(tri, g_lo)
        b_last = b[CHUNK - 1:CHUNK, :] if d == 0 else b[0:1, :]
        e_pos = jnp.exp(b)
        e_neg = jnp.exp(-b)
        e_rem = jnp.exp(b_last - b)
        decay = jnp.exp(b_last)
        for h in range(H):
            ks = slice(h * DK, (h + 1) * DK)
            q = q_ref[0, rows, h * DK:(h + 1) * DK].astype(F32)
            k = q_ref[0, rows, KD + h * DK:KD + (h + 1) * DK].astype(F32)
            v = q_ref[0, rows, 2 * KD + h * DV:2 * KD + (h + 1) * DV]
            qd = (q * e_pos[:, ks]).astype(BF)
            kd = (k * e_neg[:, ks]).astype(BF)
            k_state = (k * e_rem[:, ks]).astype(BF)
            att = jnp.where(mask, _dot_nt(qd, kd), 0.0).astype(BF)
            s_t = st_ref[d, h]
            o = _dot(att, v) + _dot_nt(qd, s_t.astype(BF))
            o_ref[0, rows, h * DV:(h + 1) * DV] = o.astype(o_ref.dtype)
            st_ref[d, h] = s_t * decay[:, ks] + _dot_tn(v, k_state)

    def body(c, carry):
        chunk(0, qf_ref, gf_ref, of_ref, pl.multiple_of(c * CHUNK, CHUNK))
        chunk(1, qb_ref, gb_ref, ob_ref, pl.multiple_of((n_chunk - 1 - c) * CHUNK, CHUNK))
        return carry

    lax.fori_loop(0, n_chunk, body, 0)

    @pl.when(j == pl.num_programs(1) - 1)
    def _():
        for d in range(2):
            for h in range(H):
                sn_ref[0, 0, d, h] = st_ref[d, h].T


def _gla_scan(qkv, g, s0):
    B, L, _ = qkv.shape
    tl = min(L, ROW_TILE)
    nl = L // tl
    zero_init = s0 is None
    wq = 2 * KD + D
    in_specs = [pl.BlockSpec((1, tl, wq), lambda b, j: (b, j, 0)),
                pl.BlockSpec((1, tl, wq), lambda b, j: (b, nl - 1 - j, 0)),
                pl.BlockSpec((1, tl, KD), lambda b, j: (b, j, 0)),
                pl.BlockSpec((1, tl, KD), lambda b, j: (b, nl - 1 - j, 1))]
    args = [qkv, qkv, g, g]
    st_spec = pl.BlockSpec((1, 1, 2, H, DK, DV), lambda b, j: (b, 0, 0, 0, 0, 0))
    if not zero_init:
        in_specs.append(st_spec)
        args.append(s0)
    return pl.pallas_call(
        functools.partial(_gla_scan_kernel, zero_init=zero_init),
        grid=(B, nl),
        in_specs=in_specs,
        out_specs=[pl.BlockSpec((1, tl, D), lambda b, j: (b, j, 0)),
                   pl.BlockSpec((1, tl, D), lambda b, j: (b, nl - 1 - j, 0)), st_spec],
        out_shape=[jax.ShapeDtypeStruct((B, L, D), BF), jax.ShapeDtypeStruct((B, L, D), BF),
                   jax.ShapeDtypeStruct((B, 1, 2, H, DK, DV), F32)],
        scratch_shapes=[pltpu.VMEM((2, H, DV, DK), F32)],
        compiler_params=_params(2),
    )(*args)


def _gla_out_kernel(of_ref, ob_ref, r_ref, on_ref, w_ref, x_ref, g1_ref, ng1_ref, ng2_ref, sc_ref, sh_ref, rw_ref,
                    x3_ref, h_ref, route_ref, rt_ref, cnt_ref):
    bb, tl, _ = x_ref.shape
    tm = bb * tl
    o = of_ref[...].astype(F32) + ob_ref[...].astype(F32)
    heads = []
    for h in range(H):
        oh = o[:, :, h * DV:(h + 1) * DV]
        heads.append(oh * lax.rsqrt(jnp.mean(oh * oh, axis=-1, keepdims=True) + RMS_EPS))
    o = jnp.concatenate(heads, -1) * on_ref[...] * r_ref[...].astype(F32)
    m = _dot(o.reshape(tm, D).astype(BF), w_ref[...])
    x3 = x_ref[...] + g1_ref[...] * _rms(m, ng1_ref[0]).reshape(bb, tl, D)
    x3_ref[...] = x3
    h2 = _rms(x3, ng2_ref[...]) * (1.0 + sc_ref[...]) + sh_ref[...]
    h_hi = h2.astype(BF)
    h_ref[...] = h_hi
    h_lo = (h2 - h_hi.astype(F32)).reshape(tm, D).astype(BF)
    both = _dot(h_hi.reshape(tm, D), rw_ref[...])
    logits = both[:, :LANES] + both[:, LANES:] + _dot(h_lo, rw_ref[:, :LANES])
    lane_i = lax.broadcasted_iota(jnp.int32, (tm, LANES), 1)
    lane = lane_i.astype(F32)
    logits = jnp.where(lane_i < N_EXPERTS, logits, -jnp.inf)
    m0 = jnp.max(logits, axis=-1, keepdims=True)
    i0 = jnp.min(jnp.where(logits == m0, lane, float(LANES)), axis=-1, keepdims=True)
    rest = jnp.where(lane == i0, -jnp.inf, logits)
    m1 = jnp.max(rest, axis=-1, keepdims=True)
    i1 = jnp.min(jnp.where(rest == m1, lane, float(LANES)), axis=-1, keepdims=True)
    e = jnp.exp(m1 - m0)
    w0 = 1.0 / (1.0 + e)
    w1 = e * w0
    onehot = jnp.where((lane == i0) | (lane == i1), 1.0, 0.0)
    ri = lax.broadcasted_iota(jnp.int32, (tm, tm), 0)
    ci = lax.broadcasted_iota(jnp.int32, (tm, tm), 1)
    before = jnp.where(ci < ri, 1.0, 0.0).astype(BF)
    prior = _dot(before, onehot.astype(BF))
    rank0 = jnp.sum(jnp.where(lane == i0, prior, 0.0), axis=-1, keepdims=True)
    rank1 = jnp.sum(jnp.where(lane == i1, prior, 0.0), axis=-1, keepdims=True)
    cnt_ref[0] = jnp.broadcast_to(jnp.sum(onehot, axis=0, keepdims=True), cnt_ref.shape[1:])
    fields = (w0, w1, i0, i1, rank0, rank1)
    route = jnp.zeros((tm, LANES), F32)
    for n, val in enumerate(fields):
        route = jnp.where(lane_i == n, val, route)
    route_ref[...] = route
    rt_ref[0] = route.T[0:SUBLANES, :]


def _gla_out(o_f, o_b, r_gate, onorm_g, w, x, g1, ng1, ng2, sc, sh, router_w):
    B, L, _ = x.shape
    r = _Rows(B, L)
    rw = jnp.pad(router_w, ((0, 0), (0, LANES - N_EXPERTS)))
    rw_hi = rw.astype(BF)
    rw = jnp.concatenate([rw_hi, (rw - rw_hi.astype(F32)).astype(BF)], axis=1)
    tm = r.tm
    return pl.pallas_call(
        _gla_out_kernel,
        grid=(r.n,),
        in_specs=[r.act(D), r.act(D), r.act(D), _const((1, 1, D)), _const((D, D)), r.act(D), r.mod(g1),
                  _const((1, 1, D)), _const((1, 1, D)), r.mod(sc), r.mod(sh), _const((D, 2 * LANES))],
        out_specs=[r.act(D), r.act(D), pl.BlockSpec((tm, LANES), lambda i: (i, 0)),
                   pl.BlockSpec((1, SUBLANES, tm), lambda i: (i, 0, 0)),
                   pl.BlockSpec((1, SUBLANES, LANES), lambda i: (i, 0, 0))],
        out_shape=[jax.ShapeDtypeStruct((B, L, D), F32), jax.ShapeDtypeStruct((B, L, D), BF),
                   jax.ShapeDtypeStruct((B * L, LANES), F32), jax.ShapeDtypeStruct((r.n, SUBLANES, tm), F32),
                   jax.ShapeDtypeStruct((r.n, SUBLANES, LANES), F32)],
        compiler_params=_params(1),
    )(o_f, o_b, r_gate, jnp.tile(onorm_g, H).reshape(1, 1, D), w.astype(BF), x, g1, ng1.reshape(1, 1, D),
      ng2.reshape(1, 1, D), sc, sh, rw)


SEG = 128
MAIN = 2


def _seg_start(seg_ref, step, e):
    return pl.multiple_of(seg_ref[step * N_EXPERTS + e], SUBLANES)


def _dispatch_kernel(seg_ref, cnt_ref, h_ref, rt_ref, xs_in_hbm, xs_hbm, stage, extra, sem, esem):
    del xs_in_hbm
    i = pl.program_id(0)
    slot = i % 2
    tm = h_ref.shape[0]
    n_piece = tm // SEG
    h = h_ref[...]
    assert n_piece == 2 * MAIN
    ex0, ex1, lr0, lr1 = (jnp.broadcast_to(rt_ref[0, n:n + 1, :], (MAIN * SEG, tm)) for n in range(2, 6))
    rank_row = lax.broadcasted_iota(jnp.int32, (MAIN * SEG, tm), 0).astype(F32)

    def row0(step, e, k):
        return _seg_start(seg_ref, step, e) + k * SEG

    def copy(step, buf, e, k):
        return pltpu.make_async_copy(stage.at[buf, e, pl.ds(k * SEG, SEG)], xs_hbm.at[pl.ds(row0(step, e, k), SEG)],
                                     sem.at[buf, e * MAIN + k])

    def live(step, e, k):
        return k * SEG < cnt_ref[step * N_EXPERTS + e]

    def each_main(step, buf, fn):
        for e in range(N_EXPERTS):
            for k in range(MAIN):
                @pl.when(live(step, e, k))
                def _(e=e, k=k):
                    fn(copy(step, buf, e, k))

    ranks = []
    for e in range(N_EXPERTS):
        ef = float(e)
        ranks.append(jnp.where(ex0 == ef, lr0, jnp.where(ex1 == ef, lr1, -1.0)))
        pick = jnp.where(ranks[e] == rank_row, 1.0, 0.0).astype(BF)
        stage[slot, e] = _dot(pick, h)

    @pl.when(i > 0)
    def _():
        each_main(i - 1, 1 - slot, lambda cp: cp.wait())

    each_main(i, slot, lambda cp: cp.start())
    for e in range(N_EXPERTS):
        @pl.when(live(i, e, MAIN))
        def _(e=e):
            more = jnp.where(ranks[e] - float(MAIN * SEG) == rank_row, 1.0, 0.0).astype(BF)
            extra[...] = _dot(more, h)
            for k in range(MAIN, n_piece):
                @pl.when(live(i, e, k))
                def _(k=k):
                    cp = pltpu.make_async_copy(extra.at[pl.ds((k - MAIN) * SEG, SEG)],
                                               xs_hbm.at[pl.ds(row0(i, e, k), SEG)], esem)
                    cp.start()
                    cp.wait()

    @pl.when(i == pl.num_programs(0) - 1)
    def _():
        each_main(i, slot, lambda cp: cp.wait())


def _dispatch(h2, route_t, seg, cnt, xs, tm=ROW_TILE):
    T = h2.shape[0]
    grid_spec = pltpu.PrefetchScalarGridSpec(
        num_scalar_prefetch=2,
        grid=(T // tm,),
        in_specs=[pl.BlockSpec((tm, D), lambda i, s, c: (i, 0)),
                  pl.BlockSpec((1, SUBLANES, tm), lambda i, s, c: (i, 0, 0)),
                  pl.BlockSpec(memory_space=pl.ANY)],
        out_specs=pl.BlockSpec(memory_space=pl.ANY),
        scratch_shapes=[pltpu.VMEM((2, N_EXPERTS, MAIN * SEG, D), F32), pltpu.VMEM((MAIN * SEG, D), F32),
                        pltpu.SemaphoreType.DMA((2, N_EXPERTS * MAIN)), pltpu.SemaphoreType.DMA(())],
    )
    return pl.pallas_call(
        _dispatch_kernel,
        grid_spec=grid_spec,
        out_shape=jax.ShapeDtypeStruct(xs.shape, F32),
        input_output_aliases={4: 0},
        compiler_params=_params(1),
    )(seg, cnt, h2, route_t, xs)


def _combine_kernel(seg_ref, cnt_ref, ys_hbm, route_ref, x_ref, g2_ref, ng_ref, o_ref, gbuf, extra, acc_ref, sem,
                    esem):
    i = pl.program_id(0)
    bb, tl, _ = x_ref.shape
    tm = bb * tl
    n_piece = tm // SEG

    def main_copy(step, slot, e, k):
        src = ys_hbm.at[pl.ds(_seg_start(seg_ref, step, e) + k * SEG, SEG)]
        return pltpu.make_async_copy(src, gbuf.at[slot, pl.ds((e * MAIN + k) * SEG, SEG)], sem.at[slot, e * MAIN + k])

    def fetch(step, slot):
        for e in range(N_EXPERTS):
            for k in range(MAIN):
                main_copy(step, slot, e, k).start()

    @pl.when(i == 0)
    def _():
        fetch(0, 0)

    @pl.when(i + 1 < pl.num_programs(0))
    def _():
        fetch(i + 1, (i + 1) % 2)

    slot = i % 2
    route = route_ref[...]

    def spread_cols(e, r0, width):
        ef = float(e)
        gate = jnp.where(route[:, 2:3] == ef, route[:, 0:1], jnp.where(route[:, 3:4] == ef, route[:, 1:2], 0.0))
        rank = jnp.where(route[:, 2:3] == ef, route[:, 4:5], jnp.where(route[:, 3:4] == ef, route[:, 5:6], -1.0))
        col = lax.broadcasted_iota(jnp.int32, (tm, width), 1).astype(F32) + float(r0)
        return (jnp.where(rank == col, 1.0, 0.0) * gate).astype(BF)

    spread = jnp.concatenate([spread_cols(e, 0, MAIN * SEG) for e in range(N_EXPERTS)], axis=1)
    for e in range(N_EXPERTS):
        for k in range(MAIN):
            main_copy(i, slot, e, k).wait()
    acc_ref[...] = _dot(spread, gbuf[slot].astype(BF))
    for e in range(N_EXPERTS):
        for k in range(MAIN, n_piece):
            @pl.when(k * SEG < cnt_ref[i * N_EXPERTS + e])
            def _(e=e, k=k):
                src = ys_hbm.at[pl.ds(_seg_start(seg_ref, i, e) + k * SEG, SEG)]
                cp = pltpu.make_async_copy(src, extra, esem)
                cp.start()
                more = spread_cols(e, k * SEG, SEG)
                cp.wait()
                acc_ref[...] += _dot(more, extra[...].astype(BF))

    o_ref[...] = x_ref[...] + g2_ref[...] * _rms(acc_ref[...], ng_ref[0]).reshape(bb, tl, D)


def _combine(ys, route, seg, cnt, x, g2, ng):
    B, L, _ = x.shape
    r = _Rows(B, L)
    tm = r.tm
    nl = r.nl
    act = pl.BlockSpec((r.bb, r.tl, D), lambda i, s, c: (i // nl, i % nl, 0))
    g2_spec = (pl.BlockSpec((1, 1, D), lambda i, s, c: (0, 0, 0)) if g2.shape[0] == 1 else
               pl.BlockSpec((r.bb, 1, D), lambda i, s, c: (i // nl, 0, 0)))
    grid_spec = pltpu.PrefetchScalarGridSpec(
        num_scalar_prefetch=2,
        grid=(r.n,),
        in_specs=[pl.BlockSpec(memory_space=pl.ANY), pl.BlockSpec((tm, LANES), lambda i, s, c: (i, 0)), act, g2_spec,
                  pl.BlockSpec((1, 1, D), lambda i, s, c: (0, 0, 0))],
        out_specs=act,
        scratch_shapes=[pltpu.VMEM((2, N_EXPERTS * MAIN * SEG, D), F32), pltpu.VMEM((SEG, D), F32),
                        pltpu.VMEM((tm, D), F32), pltpu.SemaphoreType.DMA((2, N_EXPERTS * MAIN)),
                        pltpu.SemaphoreType.DMA(())],
    )
    return pl.pallas_call(
        _combine_kernel,
        grid_spec=grid_spec,
        out_shape=jax.ShapeDtypeStruct((B, L, D), F32),
        compiler_params=_params(1),
    )(seg, cnt, ys, route, x, g2, ng.reshape(1, 1, D))


def _moe(passes, ng, w_gu, w_d, tmoe=1024):
    cntb = jnp.concatenate([ps['blk_cnt'][:, 0, :N_EXPERTS] for ps in passes], axis=0).astype(jnp.int32)
    n_tok_blocks = cntb.shape[0]
    n_tok = sum(ps['h2'].shape[0] * ps['h2'].shape[1] for ps in passes)
    held = (cntb + SUBLANES - 1) // SUBLANES * SUBLANES
    before = jnp.cumsum(held, axis=0) - held
    cnt = jnp.sum(held, axis=0)
    p_cnt = (cnt + MAIN * SEG + tmoe - 1) // tmoe * tmoe
    p_end = jnp.cumsum(p_cnt)
    p_start = p_end - p_cnt
    seg = p_start[None, :] + before
    n_rows = 2 * n_tok + N_EXPERTS * (n_tok_blocks * (SUBLANES - 1) + tmoe + MAIN * SEG)
    n_rows = (n_rows + tmoe - 1) // tmoe * tmoe
    n_blocks = n_rows // tmoe
    starts = jnp.arange(n_blocks, dtype=jnp.int32) * tmoe
    block_e = jnp.minimum(jnp.sum(starts[:, None] >= p_end[None, :], axis=-1), N_EXPERTS - 1).astype(jnp.int32)
    n_valid = (p_end[-1:] // tmoe).astype(jnp.int32)
    xs = jnp.zeros((n_rows, D), F32)
    first = 0
    for ps in passes:
        B, L, _ = ps['h2'].shape
        nb = ps['blk_cnt'].shape[0]
        ps['seg'] = seg[first:first + nb].reshape(-1)
        ps['cnt'] = cntb[first:first + nb].reshape(-1)
        first += nb
        xs = _dispatch(ps['h2'].reshape(B * L, D), ps['route_t'], ps['seg'], ps['cnt'], xs)
    ys = _swiglu(xs, w_gu, w_d, block_e, n_valid, tmoe, D_FF_EXPERT // 2)
    return [_combine(ys, ps['route'], ps['seg'], ps['cnt'], ps['x3'], ps['g2'], ng) for ps in passes]


def _trunk(x, mods, n_row, s0, p):
    B, L, _ = x.shape
    T = B * L
    ng = p['norm_g']
    sh1, sc1, g1, sh2, sc2, g2 = mods[0]
    v, x0 = _hy_in(x, ng[0, 0], sc1, sh1, p['hy_in_w'][0], p['hy_in_b'][0], p['hy_sc_w'][0], p['hy_sc_b'][0], n_row)
    kern = _hyena_kernel_taps(L, p['hy_f_w1'][0], p['hy_f_b1'][0], p['hy_f_w2'][0], p['hy_f_b2'][0],
                              p['hy_f_freq'][0], p['hy_f_w3'][0])
    if L <= 512:
        vx = _fftconv_short(v, x0, kern, p['hy_skip'][0])
    else:
        vx = _fftconv_long(v, x0, kern, p['hy_skip'][0])
    x1, h2 = _hy_out(vx, p['hy_out_w'][0], p['hy_out_b'][0], x, g1, ng[0, 1], ng[0, 2], sc2, sh2)
    tm = 1024
    ones = jnp.zeros((T // tm,), jnp.int32)
    f = _swiglu(h2.reshape(T, D), p['ffn_wgu'], p['ffn_wd'], ones, jnp.full((1,), T // tm, jnp.int32), tm, D_FF)
    g2_0, ng3_0 = g2, ng[0, 3]
    sh1, sc1, g1, sh2, sc2, g2 = mods[1]
    x2, qkv, gate, r_gate = _gla_in(x1, f, g2_0, ng3_0, ng[1, 0], sc1, sh1, p['gla_qkv_w'][0], p['gla_gk_w1'][0],
                                    p['gla_gk_w2'][0], p['gla_gk_b'][0], p['gla_r_w'][0], p['gla_r_b'][0])
    o_f, o_b, s_new = _gla_scan(qkv, gate, s0)
    x3, h2, route, route_t, blk_cnt = _gla_out(o_f, o_b, r_gate, p['gla_onorm_g'][0], p['gla_out_w'][0], x2, g1,
                                               ng[1, 1], ng[1, 2], sc2, sh2, p['moe_router'][0])
    return dict(x3=x3, h2=h2, route=route, route_t=route_t, blk_cnt=blk_cnt, g2=g2), s_new


def kernel(x_prompt, x_sample, state_gla, c, c_ctx, ada_w, ada_b, norm_g, hy_in_w, hy_in_b, hy_sc_w, hy_sc_b, hy_f_w1, hy_f_b1, hy_f_w2, hy_f_b2, hy_f_freq, hy_f_w3, hy_skip, hy_out_w, hy_out_b, gla_qkv_w, gla_gk_w1, gla_gk_w2, gla_gk_b, gla_r_w, gla_r_b, gla_onorm_g, gla_out_w, ffn_wgu, ffn_wd, moe_router, moe_wgu, moe_wd):
    p = dict(norm_g=norm_g, hy_in_w=hy_in_w, hy_in_b=hy_in_b, hy_sc_w=hy_sc_w, hy_sc_b=hy_sc_b, hy_f_w1=hy_f_w1,
             hy_f_b1=hy_f_b1, hy_f_w2=hy_f_w2, hy_f_b2=hy_f_b2, hy_f_freq=hy_f_freq, hy_f_w3=hy_f_w3,
             hy_skip=hy_skip, hy_out_w=hy_out_w, hy_out_b=hy_out_b, gla_qkv_w=gla_qkv_w, gla_gk_w1=gla_gk_w1,
             gla_gk_w2=gla_gk_w2, gla_gk_b=gla_gk_b, gla_r_w=gla_r_w, gla_r_b=gla_r_b, gla_onorm_g=gla_onorm_g,
             gla_out_w=gla_out_w, ffn_wgu=ffn_wgu.astype(BF), ffn_wd=ffn_wd.astype(BF),
             moe_router=moe_router, moe_wgu=moe_wgu[0].astype(BF), moe_wd=moe_wd[0].astype(BF))
    n_dec = c.shape[0]
    cond = jnp.concatenate([c_ctx[None, :], c, jnp.zeros((16 - 1 - n_dec, D), F32)], axis=0)
    mod = _ada(cond, ada_w, ada_b)
    mods_ctx = [[m[:, None, :] for m in jnp.split(mod[l, 0:1], 6, axis=-1)] for l in range(DEPTH)]
    mods_dec = [[m[:, None, :] for m in jnp.split(mod[l, 1:1 + n_dec], 6, axis=-1)] for l in range(DEPTH)]
    ctx, state_new = _trunk(x_prompt, mods_ctx, x_prompt.shape[1], None, p)
    grid_w = 64
    dec, _ = _trunk(x_sample, mods_dec, grid_w, state_gla, p)
    y_prompt, y_sample = _moe([ctx, dec], norm_g[1, 3], p['moe_wgu'], p['moe_wd'])
    return y_prompt, y_sample, state_new
```

```python
import functools
import math

import jax
import jax.numpy as jnp
from jax import lax
from jax.experimental import pallas as pl
from jax.experimental.pallas import tpu as pltpu

F32 = jnp.float32
BF = jnp.bfloat16

D = 1024
RMS_EPS = 1e-6
DEPTH = 2
HY_SHORT = 3
HY_EMB = 33
HY_BANDS = (HY_EMB - 1) // 2
HY_FFN = 64
HY_MAX_DECAY = math.log(1e-2) / 0.3
HY_MIN_DECAY = math.log(1e-2) / 1.5
H = 4
DK = 128
DV = 256
KD = H * DK
GLA_RANK = 16
GLA_GATE_NORM = 16.0
CHUNK = 64
D_FF = 11 * D // 4
N_EXPERTS = 8
D_FF_EXPERT = 7 * D // 2

LANES = 128
SUBLANES = 8
VMEM_LIMIT_BYTES = 56 * 1024 * 1024
ROW_TILE = 512
FFT_S2 = 128


def _params(n_axes):
    return pltpu.CompilerParams(dimension_semantics=("arbitrary",) * n_axes,
                                vmem_limit_bytes=VMEM_LIMIT_BYTES)


def _dot(a, b):
    return jnp.dot(a, b, preferred_element_type=F32)


def _dot_nt(a, b):
    return lax.dot_general(a, b, (((1,), (1,)), ((), ())), preferred_element_type=F32)


def _dot_tn(a, b):
    return lax.dot_general(a, b, (((0,), (0,)), ((), ())), preferred_element_type=F32)


def _rms(x, g):
    return x * lax.rsqrt(jnp.mean(x * x, axis=-1, keepdims=True) + RMS_EPS) * g


def _silu(x):
    return x * (1.0 / (1.0 + jnp.exp(-x)))


class _Rows:
    def __init__(self, B, L, tm=ROW_TILE):
        self.B, self.L = B, L
        if L >= tm:
            self.bb, self.tl = 1, tm
        else:
            self.bb, self.tl = tm // L, L
        assert L % self.tl == 0 and B % self.bb == 0
        self.nl = L // self.tl
        self.n = (B // self.bb) * self.nl
        self.tm = self.bb * self.tl

    def act(self, width):
        nl = self.nl
        return pl.BlockSpec((self.bb, self.tl, width), lambda i: (i // nl, i % nl, 0))

    def mod(self, m):
        nl = self.nl
        if m.shape[0] == 1:
            return pl.BlockSpec((1, 1, D), lambda i: (0, 0, 0))
        return pl.BlockSpec((self.bb, 1, D), lambda i: (i // nl, 0, 0))


def _const(shape):
    nd = len(shape)
    return pl.BlockSpec(shape, lambda *_: (0,) * nd)


def _ada_kernel(c_ref, w_ref, b_ref, o_ref):
    cs = _silu(c_ref[...])
    o_ref[0] = _dot(cs.astype(BF), w_ref[0].astype(BF)) + b_ref[0]


def _ada(cond, ada_w, ada_b):
    R = cond.shape[0]
    tn = 1536
    return pl.pallas_call(
        _ada_kernel,
        grid=(DEPTH, 6 * D // tn),
        in_specs=[pl.BlockSpec((R, D), lambda l, n: (0, 0)),
                  pl.BlockSpec((1, D, tn), lambda l, n: (l, 0, n)),
                  pl.BlockSpec((1, 1, tn), lambda l, n: (l, 0, n))],
        out_specs=pl.BlockSpec((1, R, tn), lambda l, n: (l, 0, n)),
        out_shape=jax.ShapeDtypeStruct((DEPTH, R, 6 * D), F32),
        compiler_params=_params(2),
    )(cond, ada_w, ada_b.reshape(DEPTH, 1, 6 * D))


def _hy_in_kernel(x_ref, ng_ref, sc_ref, sh_ref, w_ref, b_ref, cw_ref, cb_ref, v_ref, x0_ref, *, n_row):
    x = x_ref[...]
    bb, tl, _ = x.shape
    tm = bb * tl
    h = _rms(x, ng_ref[...]) * (1.0 + sc_ref[...]) + sh_ref[...]
    hb = h.reshape(tm, D).astype(BF)
    pos = lax.broadcasted_iota(jnp.int32, (tm, D), 0) & (n_row - 1)
    first = pos == 0
    last = pos == n_row - 1
    parts = []
    for j in range(3):
        cols = slice(j * D, (j + 1) * D)
        u = _dot(hb, w_ref[:, cols]) + b_ref[:, cols]
        up = jnp.where(first, 0.0, pltpu.roll(u, 1, 0))
        dn = jnp.where(last, 0.0, pltpu.roll(u, tm - 1, 0))
        parts.append(cb_ref[:, cols] + up * cw_ref[0:1, cols] + u * cw_ref[1:2, cols] + dn * cw_ref[2:3, cols])
    x0, x1, v = parts
    v_ref[...] = (v * x1).astype(BF).reshape(bb, tl, D)
    x0_ref[...] = x0.astype(BF).reshape(bb, tl, D)


def _hy_in(x, ng, sc, sh, w, b, cw, cb, n_row):
    B, L, _ = x.shape
    assert n_row & (n_row - 1) == 0
    r = _Rows(B, L)
    assert r.tl % n_row == 0
    return pl.pallas_call(
        functools.partial(_hy_in_kernel, n_row=n_row),
        grid=(r.n,),
        in_specs=[r.act(D), _const((1, 1, D)), r.mod(sc), r.mod(sh),
                  _const((D, 3 * D)), _const((1, 3 * D)), _const((HY_SHORT, 3 * D)), _const((1, 3 * D))],
        out_specs=[r.act(D), r.act(D)],
        out_shape=[jax.ShapeDtypeStruct((B, L, D), BF)] * 2,
        compiler_params=_params(1),
    )(x, ng.reshape(1, 1, D), sc, sh, w.astype(BF), b.reshape(1, 3 * D), cw, cb.reshape(1, 3 * D))


def _filter_kernel(z_ref, w1_ref, b1_ref, w2_ref, b2_ref, fr_ref, w3_ref, dl_ref, o_ref, *, L):
    tr = z_ref.shape[0]
    hp = lax.Precision.HIGHEST
    z = z_ref[...]
    h = jnp.sin(fr_ref[0:1, :] * (jnp.dot(z, w1_ref[...], precision=hp, preferred_element_type=F32) + b1_ref[...]))
    h = jnp.sin(fr_ref[1:2, :] * (jnp.dot(h, w2_ref[...], precision=hp, preferred_element_type=F32) + b2_ref[...]))
    hw = _dot(h.astype(BF), w3_ref[...])
    n = pl.program_id(0) * tr + lax.broadcasted_iota(jnp.int32, (tr, D), 0)
    taps = jnp.where(n < L, hw[:, :D], hw[:, D:]) * jnp.exp(-z[:, 0:1] * dl_ref[...])
    o_ref[...] = jnp.where(n == L, 0.0, taps)


def _hyena_kernel_taps(L, w1, b1, w2, b2, freq, w3):
    n = jnp.arange(2 * L, dtype=jnp.int32)
    pos = jnp.where(n < L, n, 2 * L - n) % L
    t = (pos.astype(F32) / (L - 1))[:, None]
    w = (2.0 * math.pi * pos.astype(F32) / L)[:, None]
    f = jnp.linspace(1e-4, HY_BANDS - 1, HY_BANDS, dtype=F32)[None, :]
    z = jnp.concatenate([t, jnp.cos(f * w), -jnp.sin(f * w)], axis=-1)
    z = jnp.pad(z, ((0, 0), (0, LANES - HY_EMB)))
    pad = LANES - HY_FFN
    w1p = jnp.pad(w1, ((0, LANES - HY_EMB), (0, pad)))
    w2p = jnp.pad(w2, ((0, pad), (0, pad)))
    w3p = jnp.pad(w3, ((0, pad), (0, 0))).astype(BF)
    b1p = jnp.pad(b1, (0, pad)).reshape(1, LANES)
    b2p = jnp.pad(b2, (0, pad)).reshape(1, LANES)
    frp = jnp.pad(freq, ((0, 0), (0, pad)))
    deltas = jnp.abs(jnp.linspace(HY_MIN_DECAY, HY_MAX_DECAY, D, dtype=F32)).reshape(1, D)
    tr = 512
    return pl.pallas_call(
        functools.partial(_filter_kernel, L=L),
        grid=(2 * L // tr,),
        in_specs=[pl.BlockSpec((tr, LANES), lambda i: (i, 0)), _const((LANES, LANES)), _const((1, LANES)),
                  _const((LANES, LANES)), _const((1, LANES)), _const((2, LANES)), _const((LANES, 2 * D)),
                  _const((1, D))],
        out_specs=pl.BlockSpec((tr, D), lambda i: (i, 0)),
        out_shape=jax.ShapeDtypeStruct((2 * L, D), F32),
        compiler_params=_params(1),
    )(z, w1p, b1p, w2p, b2p, frp, w3p, deltas)


def _cis(rows, cols, n, sign, scale=1.0):
    ph = (rows[:, None] * cols[None, :]) % n
    ang = ph.astype(F32) * (2.0 * math.pi / n)
    return jnp.cos(ang) * scale, jnp.sin(ang) * (sign * scale)


def _cplx_block(cr, ci):
    return jnp.concatenate([jnp.concatenate([cr, -ci], 1), jnp.concatenate([ci, cr], 1)], 0)


def _fft_short_kernel(v_ref, x0_ref, kern_ref, mk_ref, mf_ref, mi_ref, sk_ref, o_ref, ks_ref, *, L):
    n2 = 2 * L

    @pl.when(pl.program_id(0) == 0)
    def _():
        ks_ref[...] = _dot(mk_ref[...], kern_ref[...].astype(BF))

    z = jnp.concatenate([v_ref[0], v_ref[1]], 0)
    u = _dot(mf_ref[...], z)
    ur, ui = u[:n2], u[n2:]
    kr, ki = ks_ref[:n2, :], ks_ref[n2:, :]
    y = jnp.concatenate([ur * kr - ui * ki, ur * ki + ui * kr], 0).astype(BF)
    t = _dot(mi_ref[...], y)
    for j in range(2):
        conv = t[j * L:(j + 1) * L]
        o_ref[j] = ((conv + v_ref[j].astype(F32) * sk_ref[...]) * x0_ref[j].astype(F32)).astype(BF)


def _fftconv_short(v, x0, kern, skip):
    B, L, _ = v.shape
    n2 = 2 * L
    k = jnp.arange(n2, dtype=jnp.int32)
    s = jnp.arange(L, dtype=jnp.int32)
    fr, fi = _cis(k, s, n2, -1.0)
    mf = _cplx_block(fr, fi).astype(BF)
    kr, ki = _cis(k, k, n2, -1.0)
    mk = jnp.concatenate([kr, ki], 0).astype(BF)
    ir, ii = _cis(s, k, n2, 1.0, 1.0 / n2)
    mi = _cplx_block(ir, ii).astype(BF)
    pair = lambda p: (p, 0, 0)
    return pl.pallas_call(
        functools.partial(_fft_short_kernel, L=L),
        grid=(B // 2,),
        in_specs=[pl.BlockSpec((2, L, D), pair), pl.BlockSpec((2, L, D), pair), _const((n2, D)),
                  _const((2 * n2, n2)), _const((2 * n2, 2 * L)), _const((2 * L, 2 * n2)), _const((1, D))],
        out_specs=pl.BlockSpec((2, L, D), pair),
        out_shape=jax.ShapeDtypeStruct((B, L, D), BF),
        scratch_shapes=[pltpu.VMEM((2 * n2, D), F32)],
        compiler_params=_params(1),
    )(v, x0, kern, mk, mf, mi, skip.reshape(1, D))


def _lmul_kernel(m_ref, x_ref, o_ref):
    o_ref[0] = _dot(m_ref[...], x_ref[0].astype(BF)).astype(o_ref.dtype)


def _lmul(m, x, cb=8192):
    G, K, NC = x.shape
    R = m.shape[0]
    return pl.pallas_call(
        _lmul_kernel,
        grid=(G, NC // cb),
        in_specs=[pl.BlockSpec((R, K), lambda g, c: (0, 0)), pl.BlockSpec((1, K, cb), lambda g, c: (g, 0, c))],
        out_specs=pl.BlockSpec((1, R, cb), lambda g, c: (g, 0, c)),
        out_shape=jax.ShapeDtypeStruct((G, R, NC), BF),
        compiler_params=_params(2),
    )(m, x)


def _fft_mid_kernel(a_ref, af_ref, mf_ref, mi_ref, o_ref, ks_ref):
    s2 = FFT_S2
    mf = mf_ref[0]

    @pl.when(pl.program_id(1) == 0)
    def _():
        ks_ref[...] = _dot(mf, jnp.concatenate([af_ref[0, 0], af_ref[1, 0]], 0))

    u = _dot(mf, jnp.concatenate([a_ref[0, 0, 0], a_ref[0, 1, 0]], 0))
    ur, ui = u[:s2], u[s2:]
    kr, ki = ks_ref[:s2, :], ks_ref[s2:, :]
    y = jnp.concatenate([ur * kr - ui * ki, ur * ki + ui * kr], 0).astype(BF)
    z = _dot(mi_ref[0], y).astype(BF)
    o_ref[0, 0, 0] = z[:s2]
    o_ref[0, 1, 0] = z[s2:]


def _lmul_out_kernel(m_ref, z_ref, v_ref, x0_ref, sk_ref, o_ref):
    conv = _dot(m_ref[...], z_ref[0])
    o_ref[0] = ((conv + v_ref[0].astype(F32) * sk_ref[...]) * x0_ref[0].astype(F32)).astype(BF)


def _fftconv_long(v, x0, kern, skip, cb=8192):
    B, L, _ = v.shape
    n2 = 2 * L
    s2 = FFT_S2
    s1 = n2 // s2
    s1h = s1 // 2
    G = B // 2
    nc = s2 * D
    i1 = jnp.arange(s1, dtype=jnp.int32)
    i1h = jnp.arange(s1h, dtype=jnp.int32)
    i2 = jnp.arange(s2, dtype=jnp.int32)
    cr, ci = _cis(i1, i1h, s1, -1.0)
    m1 = _cplx_block(cr, ci).astype(BF)
    cr, ci = _cis(i1, i1, s1, -1.0)
    m1f = jnp.concatenate([cr, ci], 0).astype(BF)
    kk = (i1[:, None] + s1 * i2[None, :]).reshape(-1)
    gr, gi = _cis(kk, i2, n2, -1.0)
    mf = jax.vmap(_cplx_block)(gr.reshape(s1, s2, s2), gi.reshape(s1, s2, s2)).astype(BF)
    hr, hi = _cis(i2, kk, n2, 1.0, 1.0 / n2)
    hr = hr.reshape(s2, s1, s2).transpose(1, 0, 2)
    hi = hi.reshape(s2, s1, s2).transpose(1, 0, 2)
    mi = jax.vmap(_cplx_block)(hr, hi).astype(BF)
    er, ei = _cis(i1h, i1, s1, 1.0)
    m3 = _cplx_block(er, ei).astype(BF)

    a = _lmul(m1, v.reshape(G, 2 * s1h, nc), cb)
    af = _lmul(m1f, kern.reshape(1, s1, nc), cb)
    zz = pl.pallas_call(
        _fft_mid_kernel,
        grid=(s1, G),
        in_specs=[pl.BlockSpec((1, 2, 1, s2, D), lambda k, p: (p, 0, k, 0, 0)),
                  pl.BlockSpec((2, 1, s2, D), lambda k, p: (0, k, 0, 0)),
                  pl.BlockSpec((1, 2 * s2, 2 * s2), lambda k, p: (k, 0, 0)),
                  pl.BlockSpec((1, 2 * s2, 2 * s2), lambda k, p: (k, 0, 0))],
        out_specs=pl.BlockSpec((1, 2, 1, s2, D), lambda k, p: (p, 0, k, 0, 0)),
        out_shape=jax.ShapeDtypeStruct((G, 2, s1, s2, D), BF),
        scratch_shapes=[pltpu.VMEM((2 * s2, D), F32)],
        compiler_params=_params(2),
    )(a.reshape(G, 2, s1, s2, D), af.reshape(2, s1, s2, D), mf, mi)
    blk = lambda g, c: (g, 0, c)
    out = pl.pallas_call(
        _lmul_out_kernel,
        grid=(G, nc // cb),
        in_specs=[pl.BlockSpec((2 * s1h, 2 * s1), lambda g, c: (0, 0)), pl.BlockSpec((1, 2 * s1, cb), blk),
                  pl.BlockSpec((1, 2 * s1h, cb), blk), pl.BlockSpec((1, 2 * s1h, cb), blk),
                  pl.BlockSpec((1, cb), lambda g, c: (0, 0))],
        out_specs=pl.BlockSpec((1, 2 * s1h, cb), blk),
        out_shape=jax.ShapeDtypeStruct((G, 2 * s1h, nc), BF),
        compiler_params=_params(2),
    )(m3, zz.reshape(G, 2 * s1, nc), v.reshape(G, 2 * s1h, nc), x0.reshape(G, 2 * s1h, nc),
      jnp.tile(skip.reshape(1, D), (1, cb // D)))
    return out.reshape(B, L, D)


def _hy_out_kernel(vx_ref, w_ref, b_ref, x_ref, g1_ref, ng1_ref, ng2_ref, sc_ref, sh_ref, x1_ref, h_ref):
    bb, tl, _ = x_ref.shape
    m = _dot(vx_ref[...].reshape(bb * tl, D), w_ref[...]) + b_ref[...]
    x1 = x_ref[...] + g1_ref[...] * _rms(m, ng1_ref[0]).reshape(bb, tl, D)
    x1_ref[...] = x1
    h_ref[...] = (_rms(x1, ng2_ref[...]) * (1.0 + sc_ref[...]) + sh_ref[...]).astype(h_ref.dtype)


def _hy_out(vx, w, b, x, g1, ng1, ng2, sc, sh):
    B, L, _ = x.shape
    r = _Rows(B, L)
    return pl.pallas_call(
        _hy_out_kernel,
        grid=(r.n,),
        in_specs=[r.act(D), _const((D, D)), _const((1, D)), r.act(D), r.mod(g1), _const((1, 1, D)),
                  _const((1, 1, D)), r.mod(sc), r.mod(sh)],
        out_specs=[r.act(D), r.act(D)],
        out_shape=[jax.ShapeDtypeStruct((B, L, D), F32), jax.ShapeDtypeStruct((B, L, D), BF)],
        compiler_params=_params(1),
    )(vx, w.astype(BF), b.reshape(1, D), x, g1, ng1.reshape(1, 1, D), ng2.reshape(1, 1, D), sc, sh)


def _swiglu_kernel(be_ref, nv_ref, x_ref, wg_ref, wu_ref, wd_ref, o_ref, acc_ref):
    del be_ref
    f = pl.program_id(1)

    @pl.when(f == 0)
    def _():
        acc_ref[...] = jnp.zeros_like(acc_ref)

    @pl.when(pl.program_id(0) < nv_ref[0])
    def _():
        x = x_ref[...].astype(BF)
        tf = wg_ref.shape[2]
        sub = 256 if tf % 256 == 0 else tf
        part = None
        for c in range(tf // sub):
            cols = slice(c * sub, (c + 1) * sub)
            g = _dot(x, wg_ref[0, :, cols])
            u = _dot(x, wu_ref[0, :, cols])
            y = _dot((_silu(g) * u).astype(BF), wd_ref[0, cols, :])
            part = y if part is None else part + y
        acc_ref[...] += part

    @pl.when(f == pl.num_programs(1) - 1)
    def _():
        o_ref[...] = acc_ref[...]


def _swiglu(x, w_gu, w_d, block_e, n_valid, tm, tf):
    rows = x.shape[0]
    F = w_d.shape[1]
    nf = F // tf
    assert rows % tm == 0 and F % tf == 0
    grid_spec = pltpu.PrefetchScalarGridSpec(
        num_scalar_prefetch=2,
        grid=(rows // tm, nf),
        in_specs=[pl.BlockSpec((tm, D), lambda i, f, be, nv: (i, 0)),
                  pl.BlockSpec((1, D, tf), lambda i, f, be, nv: (be[i], 0, f)),
                  pl.BlockSpec((1, D, tf), lambda i, f, be, nv: (be[i], 0, nf + f)),
                  pl.BlockSpec((1, tf, D), lambda i, f, be, nv: (be[i], f, 0))],
        out_specs=pl.BlockSpec((tm, D), lambda i, f, be, nv: (i, 0)),
        scratch_shapes=[pltpu.VMEM((tm, D), F32)],
    )
    return pl.pallas_call(
        _swiglu_kernel,
        grid_spec=grid_spec,
        out_shape=jax.ShapeDtypeStruct((rows, D), F32),
        compiler_params=_params(2),
    )(block_e, n_valid, x, w_gu, w_gu, w_d)


def _gla_in_kernel(x1_ref, f_ref, g2_ref, ng3_ref, ng0_ref, sc_ref, sh_ref, wqkv_ref, wg1_ref, wg2_ref, bg_ref,
                   wr_ref, br_ref, x2_ref, qkv_ref, g_ref, r_ref):
    bb, tl, _ = x1_ref.shape
    n_split = 2 if tl % 32 == 0 else 1
    tls = tl // n_split
    tms = bb * tls
    for s in range(n_split):
        rows = slice(s * tls, (s + 1) * tls)
        x2 = x1_ref[:, rows, :] + g2_ref[...] * _rms(f_ref[:, rows, :], ng3_ref[...])
        x2_ref[:, rows, :] = x2
        h = (_rms(x2, ng0_ref[...]) * (1.0 + sc_ref[...]) + sh_ref[...]).reshape(tms, D).astype(BF)
        for c in range(4):
            cols = slice(c * KD, (c + 1) * KD)
            part = _dot(h, wqkv_ref[:, cols])
            if c == 0:
                part = part * (DK ** -0.5)
            qkv_ref[:, rows, cols] = part.astype(BF).reshape(bb, tls, KD)
        low = _dot(h, wg1_ref[...]).astype(BF)
        a = _dot(low, wg2_ref[...]) + bg_ref[...]
        log_sig = jnp.minimum(a, 0.0) - jnp.log(1.0 + jnp.exp(-jnp.abs(a)))
        g_ref[:, rows, :] = (log_sig / GLA_GATE_NORM).reshape(bb, tls, 2 * KD)
        r_ref[:, rows, :] = _silu(_dot(h, wr_ref[...]) + br_ref[...]).astype(BF).reshape(bb, tls, D)


def _gla_in(x1, f, g2, ng3, ng0, sc, sh, qkv_w, gk_w1, gk_w2, gk_b, r_w, r_b):
    B, L, _ = x1.shape
    r = _Rows(B, L)
    w1 = jnp.zeros((D, LANES), F32).at[:, :GLA_RANK].set(gk_w1[0]).at[:, GLA_RANK:2 * GLA_RANK].set(gk_w1[1])
    w2 = jnp.zeros((LANES, 2 * KD), F32).at[:GLA_RANK, :KD].set(gk_w2[0]).at[GLA_RANK:2 * GLA_RANK, KD:].set(gk_w2[1])
    return pl.pallas_call(
        _gla_in_kernel,
        grid=(r.n,),
        in_specs=[r.act(D), r.act(D), r.mod(g2), _const((1, 1, D)), _const((1, 1, D)), r.mod(sc), r.mod(sh),
                  _const((D, 2 * KD + D)), _const((D, LANES)), _const((LANES, 2 * KD)), _const((1, 2 * KD)),
                  _const((D, D)), _const((1, D))],
        out_specs=[r.act(D), r.act(2 * KD + D), r.act(2 * KD), r.act(D)],
        out_shape=[jax.ShapeDtypeStruct((B, L, D), F32), jax.ShapeDtypeStruct((B, L, 2 * KD + D), BF),
                   jax.ShapeDtypeStruct((B, L, 2 * KD), F32), jax.ShapeDtypeStruct((B, L, D), BF)],
        compiler_params=_params(1),
    )(x1, f.reshape(B, L, D), g2, ng3.reshape(1, 1, D), ng0.reshape(1, 1, D), sc, sh, qkv_w.astype(BF),
      w1.astype(BF), w2.astype(BF), gk_b.reshape(1, 2 * KD), r_w.astype(BF), r_b.reshape(1, D))


def _gla_scan_kernel(*refs, zero_init):
    if zero_init:
        qf_ref, qb_ref, gf_ref, gb_ref, of_ref, ob_ref, sn_ref, st_ref = refs
    else:
        qf_ref, qb_ref, gf_ref, gb_ref, s0_ref, of_ref, ob_ref, sn_ref, st_ref = refs
    j = pl.program_id(1)
    tl = qf_ref.shape[1]
    n_chunk = tl // CHUNK

    @pl.when(j == 0)
    def _():
        for d in range(2):
            for h in range(H):
                if zero_init:
                    st_ref[d, h] = jnp.zeros((DV, DK), F32)
                else:
                    st_ref[d, h] = s0_ref[0, 0, d, h].T

    ri = lax.broadcasted_iota(jnp.int32, (CHUNK, CHUNK), 0)
    ci = lax.broadcasted_iota(jnp.int32, (CHUNK, CHUNK), 1)
    masks = (ci <= ri, ci >= ri)

    def decays(d, g_ref, rows):
        tri = jnp.where(masks[d], 1.0, 0.0).astype(BF)
        g = g_ref[0, rows, :]
        g_hi = g.astype(BF)
        g_lo = (g - g_hi.astype(F32)).astype(BF)
        b = _dot(tri, g_hi) + _dot(tri, g_lo)
        b_last = b[CHUNK - 1:CHUNK, :] if d == 0 else b[0:1, :]
        return jnp.exp(b), jnp.exp(-b), jnp.exp(b_last - b), jnp.exp(b_last)

    def chunk_rows(c):
        return (pl.ds(pl.multiple_of(c * CHUNK, CHUNK), CHUNK),
                pl.ds(pl.multiple_of((n_chunk - 1 - c) * CHUNK, CHUNK), CHUNK))

    def body(c, carry):
        rows = chunk_rows(c)
        q_refs, o_refs = (qf_ref, qb_ref), (of_ref, ob_ref)
        fac = (decays(0, gf_ref, rows[0]), decays(1, gb_ref, rows[1]))
        chains = [(d, h) for h in range(H) for d in range(2)]
        qd, ksc, att, v = {}, {}, {}, {}
        for d, h in chains:
            e_pos, e_neg, e_rem, _ = fac[d]
            ks = slice(h * DK, (h + 1) * DK)
            q = q_refs[d][0, rows[d], h * DK:(h + 1) * DK].astype(F32)
            k = q_refs[d][0, rows[d], KD + h * DK:KD + (h + 1) * DK].astype(F32)
            qd[d, h] = (q * e_pos[:, ks]).astype(BF)
            ksc[d, h] = (k * e_rem[:, ks]).astype(BF)
            att[d, h] = _dot_nt(qd[d, h], (k * e_neg[:, ks]).astype(BF))
        for d, h in chains:
            v[d, h] = q_refs[d][0, rows[d], 2 * KD + h * DV:2 * KD + (h + 1) * DV]
            a = jnp.where(masks[d], att[d, h], 0.0).astype(BF)
            o = _dot(a, v[d, h]) + _dot_nt(qd[d, h], st_ref[d, h].astype(BF))
            o_refs[d][0, rows[d], h * DV:(h + 1) * DV] = o.astype(o_refs[d].dtype)
        for d, h in chains:
            ks = slice(h * DK, (h + 1) * DK)
            st_ref[d, h] = st_ref[d, h] * fac[d][3][:, ks] + _dot_tn(v[d, h], ksc[d, h])
        return carry

    lax.fori_loop(0, n_chunk, body, 0)

    @pl.when(j == pl.num_programs(1) - 1)
    def _():
        for d in range(2):
            for h in range(H):
                sn_ref[0, 0, d, h] = st_ref[d, h].T


def _gla_scan(qkv, g, s0):
    B, L, _ = qkv.shape
    tl = min(L, ROW_TILE)
    nl = L // tl
    zero_init = s0 is None
    wq = 2 * KD + D
    in_specs = [pl.BlockSpec((1, tl, wq), lambda b, j: (b, j, 0)),
                pl.BlockSpec((1, tl, wq), lambda b, j: (b, nl - 1 - j, 0)),
                pl.BlockSpec((1, tl, KD), lambda b, j: (b, j, 0)),
                pl.BlockSpec((1, tl, KD), lambda b, j: (b, nl - 1 - j, 1))]
    args = [qkv, qkv, g, g]
    st_spec = pl.BlockSpec((1, 1, 2, H, DK, DV), lambda b, j: (b, 0, 0, 0, 0, 0))
    if not zero_init:
        in_specs.append(st_spec)
        args.append(s0)
    return pl.pallas_call(
        functools.partial(_gla_scan_kernel, zero_init=zero_init),
        grid=(B, nl),
        in_specs=in_specs,
        out_specs=[pl.BlockSpec((1, tl, D), lambda b, j: (b, j, 0)),
                   pl.BlockSpec((1, tl, D), lambda b, j: (b, nl - 1 - j, 0)), st_spec],
        out_shape=[jax.ShapeDtypeStruct((B, L, D), BF), jax.ShapeDtypeStruct((B, L, D), BF),
                   jax.ShapeDtypeStruct((B, 1, 2, H, DK, DV), F32)],
        scratch_shapes=[pltpu.VMEM((2, H, DV, DK), F32)],
        compiler_params=_params(2),
    )(*args)


def _gla_out_kernel(of_ref, ob_ref, r_ref, on_ref, w_ref, x_ref, g1_ref, ng1_ref, ng2_ref, sc_ref, sh_ref, rw_ref,
                    x3_ref, h_ref, route_ref, rt_ref, cnt_ref):
    bb, tl, _ = x_ref.shape
    tm = bb * tl
    o = of_ref[...].astype(F32) + ob_ref[...].astype(F32)
    heads = []
    for h in range(H):
        oh = o[:, :, h * DV:(h + 1) * DV]
        heads.append(oh * lax.rsqrt(jnp.mean(oh * oh, axis=-1, keepdims=True) + RMS_EPS))
    o = jnp.concatenate(heads, -1) * on_ref[...] * r_ref[...].astype(F32)
    m = _dot(o.reshape(tm, D).astype(BF), w_ref[...])
    x3 = x_ref[...] + g1_ref[...] * _rms(m, ng1_ref[0]).reshape(bb, tl, D)
    x3_ref[...] = x3
    h2 = _rms(x3, ng2_ref[...]) * (1.0 + sc_ref[...]) + sh_ref[...]
    h_hi = h2.astype(BF)
    h_ref[...] = h_hi
    h_lo = (h2 - h_hi.astype(F32)).reshape(tm, D).astype(BF)
    both = _dot(h_hi.reshape(tm, D), rw_ref[...])
    logits = both[:, :LANES] + both[:, LANES:] + _dot(h_lo, rw_ref[:, :LANES])
    lane_i = lax.broadcasted_iota(jnp.int32, (tm, LANES), 1)
    lane = lane_i.astype(F32)
    logits = jnp.where(lane_i < N_EXPERTS, logits, -jnp.inf)
    m0 = jnp.max(logits, axis=-1, keepdims=True)
    i0 = jnp.min(jnp.where(logits == m0, lane, float(LANES)), axis=-1, keepdims=True)
    rest = jnp.where(lane == i0, -jnp.inf, logits)
    m1 = jnp.max(rest, axis=-1, keepdims=True)
    i1 = jnp.min(jnp.where(rest == m1, lane, float(LANES)), axis=-1, keepdims=True)
    e = jnp.exp(m1 - m0)
    w0 = 1.0 / (1.0 + e)
    w1 = e * w0
    onehot = jnp.where((lane == i0) | (lane == i1), 1.0, 0.0)
    ri = lax.broadcasted_iota(jnp.int32, (tm, tm), 0)
    ci = lax.broadcasted_iota(jnp.int32, (tm, tm), 1)
    before = jnp.where(ci < ri, 1.0, 0.0).astype(BF)
    prior = _dot(before, onehot.astype(BF))
    rank0 = jnp.sum(jnp.where(lane == i0, prior, 0.0), axis=-1, keepdims=True)
    rank1 = jnp.sum(jnp.where(lane == i1, prior, 0.0), axis=-1, keepdims=True)
    cnt_ref[0] = jnp.broadcast_to(jnp.sum(onehot, axis=0, keepdims=True), cnt_ref.shape[1:])
    fields = (w0, w1, i0, i1, rank0, rank1)
    route = jnp.zeros((tm, LANES), F32)
    for n, val in enumerate(fields):
        route = jnp.where(lane_i == n, val, route)
    route_ref[...] = route
    rt_ref[0] = route.T[0:SUBLANES, :]


def _gla_out(o_f, o_b, r_gate, onorm_g, w, x, g1, ng1, ng2, sc, sh, router_w):
    B, L, _ = x.shape
    r = _Rows(B, L)
    rw = jnp.pad(router_w, ((0, 0), (0, LANES - N_EXPERTS)))
    rw_hi = rw.astype(BF)
    rw = jnp.concatenate([rw_hi, (rw - rw_hi.astype(F32)).astype(BF)], axis=1)
    tm = r.tm
    return pl.pallas_call(
        _gla_out_kernel,
        grid=(r.n,),
        in_specs=[r.act(D), r.act(D), r.act(D), _const((1, 1, D)), _const((D, D)), r.act(D), r.mod(g1),
                  _const((1, 1, D)), _const((1, 1, D)), r.mod(sc), r.mod(sh), _const((D, 2 * LANES))],
        out_specs=[r.act(D), r.act(D), pl.BlockSpec((tm, LANES), lambda i: (i, 0)),
                   pl.BlockSpec((1, SUBLANES, tm), lambda i: (i, 0, 0)),
                   pl.BlockSpec((1, SUBLANES, LANES), lambda i: (i, 0, 0))],
        out_shape=[jax.ShapeDtypeStruct((B, L, D), F32), jax.ShapeDtypeStruct((B, L, D), BF),
                   jax.ShapeDtypeStruct((B * L, LANES), F32), jax.ShapeDtypeStruct((r.n, SUBLANES, tm), F32),
                   jax.ShapeDtypeStruct((r.n, SUBLANES, LANES), F32)],
        compiler_params=_params(1),
    )(o_f, o_b, r_gate, jnp.tile(onorm_g, H).reshape(1, 1, D), w.astype(BF), x, g1, ng1.reshape(1, 1, D),
      ng2.reshape(1, 1, D), sc, sh, rw)


SEG = 128
MAIN = 2


def _seg_start(seg_ref, step, e):
    return pl.multiple_of(seg_ref[step * N_EXPERTS + e], SUBLANES)


def _dispatch_kernel(seg_ref, cnt_ref, h_ref, rt_ref, xs_in_hbm, xs_hbm, stage, extra, sem, esem):
    del xs_in_hbm
    i = pl.program_id(0)
    slot = i % 2
    tm = h_ref.shape[0]
    n_piece = tm // SEG
    h = h_ref[...]
    assert n_piece == 2 * MAIN
    ex0, ex1, lr0, lr1 = (jnp.broadcast_to(rt_ref[0, n:n + 1, :], (MAIN * SEG, tm)) for n in range(2, 6))
    rank_row = lax.broadcasted_iota(jnp.int32, (MAIN * SEG, tm), 0).astype(F32)

    def row0(step, e, k):
        return _seg_start(seg_ref, step, e) + k * SEG

    def copy(step, buf, e, k):
        return pltpu.make_async_copy(stage.at[buf, e, pl.ds(k * SEG, SEG)], xs_hbm.at[pl.ds(row0(step, e, k), SEG)],
                                     sem.at[buf, e * MAIN + k])

    def live(step, e, k):
        return k * SEG < cnt_ref[step * N_EXPERTS + e]

    def each_main(step, buf, fn):
        for e in range(N_EXPERTS):
            for k in range(MAIN):
                @pl.when(live(step, e, k))
                def _(e=e, k=k):
                    fn(copy(step, buf, e, k))

    ranks = []
    for e in range(N_EXPERTS):
        ef = float(e)
        ranks.append(jnp.where(ex0 == ef, lr0, jnp.where(ex1 == ef, lr1, -1.0)))
        pick = jnp.where(ranks[e] == rank_row, 1.0, 0.0).astype(BF)
        stage[slot, e] = _dot(pick, h)

    @pl.when(i > 0)
    def _():
        each_main(i - 1, 1 - slot, lambda cp: cp.wait())

    each_main(i, slot, lambda cp: cp.start())
    for e in range(N_EXPERTS):
        @pl.when(live(i, e, MAIN))
        def _(e=e):
            more = jnp.where(ranks[e] - float(MAIN * SEG) == rank_row, 1.0, 0.0).astype(BF)
            extra[...] = _dot(more, h)
            for k in range(MAIN, n_piece):
                @pl.when(live(i, e, k))
                def _(k=k):
                    cp = pltpu.make_async_copy(extra.at[pl.ds((k - MAIN) * SEG, SEG)],
                                               xs_hbm.at[pl.ds(row0(i, e, k), SEG)], esem)
                    cp.start()
                    cp.wait()

    @pl.when(i == pl.num_programs(0) - 1)
    def _():
        each_main(i, slot, lambda cp: cp.wait())


def _dispatch(h2, route_t, seg, cnt, xs, tm=ROW_TILE):
    T = h2.shape[0]
    grid_spec = pltpu.PrefetchScalarGridSpec(
        num_scalar_prefetch=2,
        grid=(T // tm,),
        in_specs=[pl.BlockSpec((tm, D), lambda i, s, c: (i, 0)),
                  pl.BlockSpec((1, SUBLANES, tm), lambda i, s, c: (i, 0, 0)),
                  pl.BlockSpec(memory_space=pl.ANY)],
        out_specs=pl.BlockSpec(memory_space=pl.ANY),
        scratch_shapes=[pltpu.VMEM((2, N_EXPERTS, MAIN * SEG, D), F32), pltpu.VMEM((MAIN * SEG, D), F32),
                        pltpu.SemaphoreType.DMA((2, N_EXPERTS * MAIN)), pltpu.SemaphoreType.DMA(())],
    )
    return pl.pallas_call(
        _dispatch_kernel,
        grid_spec=grid_spec,
        out_shape=jax.ShapeDtypeStruct(xs.shape, F32),
        input_output_aliases={4: 0},
        compiler_params=_params(1),
    )(seg, cnt, h2, route_t, xs)


def _combine_kernel(seg_ref, cnt_ref, ys_hbm, route_ref, x_ref, g2_ref, ng_ref, o_ref, gbuf, extra, acc_ref, sem,
                    esem):
    i = pl.program_id(0)
    bb, tl, _ = x_ref.shape
    tm = bb * tl
    n_piece = tm // SEG

    def main_copy(step, slot, e, k):
        src = ys_hbm.at[pl.ds(_seg_start(seg_ref, step, e) + k * SEG, SEG)]
        return pltpu.make_async_copy(src, gbuf.at[slot, pl.ds((e * MAIN + k) * SEG, SEG)], sem.at[slot, e * MAIN + k])

    def fetch(step, slot):
        for e in range(N_EXPERTS):
            for k in range(MAIN):
                main_copy(step, slot, e, k).start()

    @pl.when(i == 0)
    def _():
        fetch(0, 0)

    @pl.when(i + 1 < pl.num_programs(0))
    def _():
        fetch(i + 1, (i + 1) % 2)

    slot = i % 2
    route = route_ref[...]

    def spread_cols(e, r0, width):
        ef = float(e)
        gate = jnp.where(route[:, 2:3] == ef, route[:, 0:1], jnp.where(route[:, 3:4] == ef, route[:, 1:2], 0.0))
        rank = jnp.where(route[:, 2:3] == ef, route[:, 4:5], jnp.where(route[:, 3:4] == ef, route[:, 5:6], -1.0))
        col = lax.broadcasted_iota(jnp.int32, (tm, width), 1).astype(F32) + float(r0)
        return (jnp.where(rank == col, 1.0, 0.0) * gate).astype(BF)

    spread = jnp.concatenate([spread_cols(e, 0, MAIN * SEG) for e in range(N_EXPERTS)], axis=1)
    for e in range(N_EXPERTS):
        for k in range(MAIN):
            main_copy(i, slot, e, k).wait()
    acc_ref[...] = _dot(spread, gbuf[slot].astype(BF))
    for e in range(N_EXPERTS):
        for k in range(MAIN, n_piece):
            @pl.when(k * SEG < cnt_ref[i * N_EXPERTS + e])
            def _(e=e, k=k):
                src = ys_hbm.at[pl.ds(_seg_start(seg_ref, i, e) + k * SEG, SEG)]
                cp = pltpu.make_async_copy(src, extra, esem)
                cp.start()
                more = spread_cols(e, k * SEG, SEG)
                cp.wait()
                acc_ref[...] += _dot(more, extra[...].astype(BF))

    o_ref[...] = x_ref[...] + g2_ref[...] * _rms(acc_ref[...], ng_ref[0]).reshape(bb, tl, D)


def _combine(ys, route, seg, cnt, x, g2, ng):
    B, L, _ = x.shape
    r = _Rows(B, L)
    tm = r.tm
    nl = r.nl
    act = pl.BlockSpec((r.bb, r.tl, D), lambda i, s, c: (i // nl, i % nl, 0))
    g2_spec = (pl.BlockSpec((1, 1, D), lambda i, s, c: (0, 0, 0)) if g2.shape[0] == 1 else
               pl.BlockSpec((r.bb, 1, D), lambda i, s, c: (i // nl, 0, 0)))
    grid_spec = pltpu.PrefetchScalarGridSpec(
        num_scalar_prefetch=2,
        grid=(r.n,),
        in_specs=[pl.BlockSpec(memory_space=pl.ANY), pl.BlockSpec((tm, LANES), lambda i, s, c: (i, 0)), act, g2_spec,
                  pl.BlockSpec((1, 1, D), lambda i, s, c: (0, 0, 0))],
        out_specs=act,
        scratch_shapes=[pltpu.VMEM((2, N_EXPERTS * MAIN * SEG, D), F32), pltpu.VMEM((SEG, D), F32),
                        pltpu.VMEM((tm, D), F32), pltpu.SemaphoreType.DMA((2, N_EXPERTS * MAIN)),
                        pltpu.SemaphoreType.DMA(())],
    )
    return pl.pallas_call(
        _combine_kernel,
        grid_spec=grid_spec,
        out_shape=jax.ShapeDtypeStruct((B, L, D), F32),
        compiler_params=_params(1),
    )(seg, cnt, ys, route, x, g2, ng.reshape(1, 1, D))


def _moe(passes, ng, w_gu, w_d, tmoe=1024):
    cntb = jnp.concatenate([ps['blk_cnt'][:, 0, :N_EXPERTS] for ps in passes], axis=0).astype(jnp.int32)
    n_tok_blocks = cntb.shape[0]
    n_tok = sum(ps['h2'].shape[0] * ps['h2'].shape[1] for ps in passes)
    held = (cntb + SUBLANES - 1) // SUBLANES * SUBLANES
    before = jnp.cumsum(held, axis=0) - held
    cnt = jnp.sum(held, axis=0)
    p_cnt = (cnt + MAIN * SEG + tmoe - 1) // tmoe * tmoe
    p_end = jnp.cumsum(p_cnt)
    p_start = p_end - p_cnt
    seg = p_start[None, :] + before
    n_rows = 2 * n_tok + N_EXPERTS * (n_tok_blocks * (SUBLANES - 1) + tmoe + MAIN * SEG)
    n_rows = (n_rows + tmoe - 1) // tmoe * tmoe
    n_blocks = n_rows // tmoe
    starts = jnp.arange(n_blocks, dtype=jnp.int32) * tmoe
    block_e = jnp.minimum(jnp.sum(starts[:, None] >= p_end[None, :], axis=-1), N_EXPERTS - 1).astype(jnp.int32)
    n_valid = (p_end[-1:] // tmoe).astype(jnp.int32)
    xs = jnp.zeros((n_rows, D), F32)
    first = 0
    for ps in passes:
        B, L, _ = ps['h2'].shape
        nb = ps['blk_cnt'].shape[0]
        ps['seg'] = seg[first:first + nb].reshape(-1)
        ps['cnt'] = cntb[first:first + nb].reshape(-1)
        first += nb
        xs = _dispatch(ps['h2'].reshape(B * L, D), ps['route_t'], ps['seg'], ps['cnt'], xs)
    ys = _swiglu(xs, w_gu, w_d, block_e, n_valid, tmoe, D_FF_EXPERT // 2)
    return [_combine(ys, ps['route'], ps['seg'], ps['cnt'], ps['x3'], ps['g2'], ng) for ps in passes]


def _trunk(x, mods, n_row, s0, p):
    B, L, _ = x.shape
    T = B * L
    ng = p['norm_g']
    sh1, sc1, g1, sh2, sc2, g2 = mods[0]
    v, x0 = _hy_in(x, ng[0, 0], sc1, sh1, p['hy_in_w'][0], p['hy_in_b'][0], p['hy_sc_w'][0], p['hy_sc_b'][0], n_row)
    kern = _hyena_kernel_taps(L, p['hy_f_w1'][0], p['hy_f_b1'][0], p['hy_f_w2'][0], p['hy_f_b2'][0],
                              p['hy_f_freq'][0], p['hy_f_w3'][0])
    if L <= 512:
        vx = _fftconv_short(v, x0, kern, p['hy_skip'][0])
    else:
        vx = _fftconv_long(v, x0, kern, p['hy_skip'][0])
    x1, h2 = _hy_out(vx, p['hy_out_w'][0], p['hy_out_b'][0], x, g1, ng[0, 1], ng[0, 2], sc2, sh2)
    tm = 1024
    ones = jnp.zeros((T // tm,), jnp.int32)
    f = _swiglu(h2.reshape(T, D), p['ffn_wgu'], p['ffn_wd'], ones, jnp.full((1,), T // tm, jnp.int32), tm, D_FF)
    g2_0, ng3_0 = g2, ng[0, 3]
    sh1, sc1, g1, sh2, sc2, g2 = mods[1]
    x2, qkv, gate, r_gate = _gla_in(x1, f, g2_0, ng3_0, ng[1, 0], sc1, sh1, p['gla_qkv_w'][0], p['gla_gk_w1'][0],
                                    p['gla_gk_w2'][0], p['gla_gk_b'][0], p['gla_r_w'][0], p['gla_r_b'][0])
    o_f, o_b, s_new = _gla_scan(qkv, gate, s0)
    x3, h2, route, route_t, blk_cnt = _gla_out(o_f, o_b, r_gate, p['gla_onorm_g'][0], p['gla_out_w'][0], x2, g1,
                                               ng[1, 1], ng[1, 2], sc2, sh2, p['moe_router'][0])
    return dict(x3=x3, h2=h2, route=route, route_t=route_t, blk_cnt=blk_cnt, g2=g2), s_new


def kernel(x_prompt, x_sample, state_gla, c, c_ctx, ada_w, ada_b, norm_g, hy_in_w, hy_in_b, hy_sc_w, hy_sc_b, hy_f_w1, hy_f_b1, hy_f_w2, hy_f_b2, hy_f_freq, hy_f_w3, hy_skip, hy_out_w, hy_out_b, gla_qkv_w, gla_gk_w1, gla_gk_w2, gla_gk_b, gla_r_w, gla_r_b, gla_onorm_g, gla_out_w, ffn_wgu, ffn_wd, moe_router, moe_wgu, moe_wd):
    p = dict(norm_g=norm_g, hy_in_w=hy_in_w, hy_in_b=hy_in_b, hy_sc_w=hy_sc_w, hy_sc_b=hy_sc_b, hy_f_w1=hy_f_w1,
             hy_f_b1=hy_f_b1, hy_f_w2=hy_f_w2, hy_f_b2=hy_f_b2, hy_f_freq=hy_f_freq, hy_f_w3=hy_f_w3,
             hy_skip=hy_skip, hy_out_w=hy_out_w, hy_out_b=hy_out_b, gla_qkv_w=gla_qkv_w, gla_gk_w1=gla_gk_w1,
             gla_gk_w2=gla_gk_w2, gla_gk_b=gla_gk_b, gla_r_w=gla_r_w, gla_r_b=gla_r_b, gla_onorm_g=gla_onorm_g,
             gla_out_w=gla_out_w, ffn_wgu=ffn_wgu.astype(BF), ffn_wd=ffn_wd.astype(BF),
             moe_router=moe_router, moe_wgu=moe_wgu[0].astype(BF), moe_wd=moe_wd[0].astype(BF))
    n_dec = c.shape[0]
    cond = jnp.concatenate([c_ctx[None, :], c, jnp.zeros((16 - 1 - n_dec, D), F32)], axis=0)
    mod = _ada(cond, ada_w, ada_b)
    mods_ctx = [[m[:, None, :] for m in jnp.split(mod[l, 0:1], 6, axis=-1)] for l in range(DEPTH)]
    mods_dec = [[m[:, None, :] for m in jnp.split(mod[l, 1:1 + n_dec], 6, axis=-1)] for l in range(DEPTH)]
    ctx, state_new = _trunk(x_prompt, mods_ctx, x_prompt.shape[1], None, p)
    grid_w = 64
    dec, _ = _trunk(x_sample, mods_dec, grid_w, state_gla, p)
    y_prompt, y_sample = _moe([ctx, dec], norm_g[1, 3], p['moe_wgu'], p['moe_wd'])
    return y_prompt, y_sample, state_new
```

```python
import functools
import math

import jax
import jax.numpy as jnp
from jax import lax
from jax.experimental import pallas as pl
from jax.experimental.pallas import tpu as pltpu

F32 = jnp.float32
BF = jnp.bfloat16

D = 1024
RMS_EPS = 1e-6
DEPTH = 2
HY_SHORT = 3
HY_EMB = 33
HY_BANDS = (HY_EMB - 1) // 2
HY_FFN = 64
HY_MAX_DECAY = math.log(1e-2) / 0.3
HY_MIN_DECAY = math.log(1e-2) / 1.5
H = 4
DK = 128
DV = 256
KD = H * DK
GLA_RANK = 16
GLA_GATE_NORM = 16.0
CHUNK = 64
D_FF = 11 * D // 4
N_EXPERTS = 8
D_FF_EXPERT = 7 * D // 2

LANES = 128
SUBLANES = 8
VMEM_LIMIT_BYTES = 56 * 1024 * 1024
ROW_TILE = 512
MOE_TILE = 256
FFT_S2 = 128


def _params(n_axes):
    return pltpu.CompilerParams(dimension_semantics=("arbitrary",) * n_axes,
                                vmem_limit_bytes=VMEM_LIMIT_BYTES)


def _dot(a, b):
    return jnp.dot(a, b, preferred_element_type=F32)


def _dot_nt(a, b):
    return lax.dot_general(a, b, (((1,), (1,)), ((), ())), preferred_element_type=F32)


def _dot_tn(a, b):
    return lax.dot_general(a, b, (((0,), (0,)), ((), ())), preferred_element_type=F32)


def _rms(x, g):
    return x * lax.rsqrt(jnp.mean(x * x, axis=-1, keepdims=True) + RMS_EPS) * g


def _silu(x):
    return x * (1.0 / (1.0 + jnp.exp(-x)))


class _Rows:
    def __init__(self, B, L, tm=ROW_TILE):
        self.B, self.L = B, L
        if L >= tm:
            self.bb, self.tl = 1, tm
        else:
            self.bb, self.tl = tm // L, L
        assert L % self.tl == 0 and B % self.bb == 0
        self.nl = L // self.tl
        self.n = (B // self.bb) * self.nl
        self.tm = self.bb * self.tl

    def act(self, width):
        nl = self.nl
        return pl.BlockSpec((self.bb, self.tl, width), lambda i: (i // nl, i % nl, 0))

    def mod(self, m):
        nl = self.nl
        if m.shape[0] == 1:
            return pl.BlockSpec((1, 1, D), lambda i: (0, 0, 0))
        return pl.BlockSpec((self.bb, 1, D), lambda i: (i // nl, 0, 0))


def _const(shape):
    nd = len(shape)
    return pl.BlockSpec(shape, lambda *_: (0,) * nd)


def _ada_kernel(c_ref, w_ref, b_ref, o_ref):
    cs = _silu(c_ref[...])
    o_ref[0] = _dot(cs.astype(BF), w_ref[0].astype(BF)) + b_ref[0]


def _ada(cond, ada_w, ada_b):
    R = cond.shape[0]
    tn = 1536
    return pl.pallas_call(
        _ada_kernel,
        grid=(DEPTH, 6 * D // tn),
        in_specs=[pl.BlockSpec((R, D), lambda l, n: (0, 0)),
                  pl.BlockSpec((1, D, tn), lambda l, n: (l, 0, n)),
                  pl.BlockSpec((1, 1, tn), lambda l, n: (l, 0, n))],
        out_specs=pl.BlockSpec((1, R, tn), lambda l, n: (l, 0, n)),
        out_shape=jax.ShapeDtypeStruct((DEPTH, R, 6 * D), F32),
        compiler_params=_params(2),
    )(cond, ada_w, ada_b.reshape(DEPTH, 1, 6 * D))


def _hy_in_kernel(x_ref, ng_ref, sc_ref, sh_ref, w_ref, b_ref, cw_ref, cb_ref, v_ref, x0_ref, *, n_row):
    x = x_ref[...]
    bb, tl, _ = x.shape
    tm = bb * tl
    h = _rms(x, ng_ref[...]) * (1.0 + sc_ref[...]) + sh_ref[...]
    hb = h.reshape(tm, D).astype(BF)
    pos = lax.broadcasted_iota(jnp.int32, (tm, D), 0) & (n_row - 1)
    first = pos == 0
    last = pos == n_row - 1
    parts = []
    for j in range(3):
        cols = slice(j * D, (j + 1) * D)
        u = _dot(hb, w_ref[:, cols]) + b_ref[:, cols]
        up = jnp.where(first, 0.0, pltpu.roll(u, 1, 0))
        dn = jnp.where(last, 0.0, pltpu.roll(u, tm - 1, 0))
        parts.append(cb_ref[:, cols] + up * cw_ref[0:1, cols] + u * cw_ref[1:2, cols] + dn * cw_ref[2:3, cols])
    x0, x1, v = parts
    v_ref[...] = (v * x1).astype(BF).reshape(bb, tl, D)
    x0_ref[...] = x0.astype(BF).reshape(bb, tl, D)


def _hy_in(x, ng, sc, sh, w, b, cw, cb, n_row):
    B, L, _ = x.shape
    assert n_row & (n_row - 1) == 0
    r = _Rows(B, L)
    assert r.tl % n_row == 0
    return pl.pallas_call(
        functools.partial(_hy_in_kernel, n_row=n_row),
        grid=(r.n,),
        in_specs=[r.act(D), _const((1, 1, D)), r.mod(sc), r.mod(sh),
                  _const((D, 3 * D)), _const((1, 3 * D)), _const((HY_SHORT, 3 * D)), _const((1, 3 * D))],
        out_specs=[r.act(D), r.act(D)],
        out_shape=[jax.ShapeDtypeStruct((B, L, D), BF)] * 2,
        compiler_params=_params(1),
    )(x, ng.reshape(1, 1, D), sc, sh, w.astype(BF), b.reshape(1, 3 * D), cw, cb.reshape(1, 3 * D))


def _filter_kernel(z_ref, w1_ref, b1_ref, w2_ref, b2_ref, fr_ref, w3_ref, dl_ref, o_ref, *, L):
    tr = z_ref.shape[0]
    hp = lax.Precision.HIGHEST
    z = z_ref[...]
    h = jnp.sin(fr_ref[0:1, :] * (jnp.dot(z, w1_ref[...], precision=hp, preferred_element_type=F32) + b1_ref[...]))
    h = jnp.sin(fr_ref[1:2, :] * (jnp.dot(h, w2_ref[...], precision=hp, preferred_element_type=F32) + b2_ref[...]))
    hw = _dot(h.astype(BF), w3_ref[...])
    n = pl.program_id(0) * tr + lax.broadcasted_iota(jnp.int32, (tr, D), 0)
    taps = jnp.where(n < L, hw[:, :D], hw[:, D:]) * jnp.exp(-z[:, 0:1] * dl_ref[...])
    o_ref[...] = jnp.where(n == L, 0.0, taps)


def _hyena_kernel_taps(L, w1, b1, w2, b2, freq, w3):
    n = jnp.arange(2 * L, dtype=jnp.int32)
    pos = jnp.where(n < L, n, 2 * L - n) % L
    t = (pos.astype(F32) / (L - 1))[:, None]
    w = (2.0 * math.pi * pos.astype(F32) / L)[:, None]
    f = jnp.linspace(1e-4, HY_BANDS - 1, HY_BANDS, dtype=F32)[None, :]
    z = jnp.concatenate([t, jnp.cos(f * w), -jnp.sin(f * w)], axis=-1)
    z = jnp.pad(z, ((0, 0), (0, LANES - HY_EMB)))
    pad = LANES - HY_FFN
    w1p = jnp.pad(w1, ((0, LANES - HY_EMB), (0, pad)))
    w2p = jnp.pad(w2, ((0, pad), (0, pad)))
    w3p = jnp.pad(w3, ((0, pad), (0, 0))).astype(BF)
    b1p = jnp.pad(b1, (0, pad)).reshape(1, LANES)
    b2p = jnp.pad(b2, (0, pad)).reshape(1, LANES)
    frp = jnp.pad(freq, ((0, 0), (0, pad)))
    deltas = jnp.abs(jnp.linspace(HY_MIN_DECAY, HY_MAX_DECAY, D, dtype=F32)).reshape(1, D)
    tr = 512
    return pl.pallas_call(
        functools.partial(_filter_kernel, L=L),
        grid=(2 * L // tr,),
        in_specs=[pl.BlockSpec((tr, LANES), lambda i: (i, 0)), _const((LANES, LANES)), _const((1, LANES)),
                  _const((LANES, LANES)), _const((1, LANES)), _const((2, LANES)), _const((LANES, 2 * D)),
                  _const((1, D))],
        out_specs=pl.BlockSpec((tr, D), lambda i: (i, 0)),
        out_shape=jax.ShapeDtypeStruct((2 * L, D), F32),
        compiler_params=_params(1),
    )(z, w1p, b1p, w2p, b2p, frp, w3p, deltas)


def _cis(rows, cols, n, sign, scale=1.0):
    ph = (rows[:, None] * cols[None, :]) % n
    ang = ph.astype(F32) * (2.0 * math.pi / n)
    return jnp.cos(ang) * scale, jnp.sin(ang) * (sign * scale)


def _cplx_block(cr, ci):
    return jnp.concatenate([jnp.concatenate([cr, -ci], 1), jnp.concatenate([ci, cr], 1)], 0)


def _fft_short_kernel(v_ref, x0_ref, kern_ref, mk_ref, mf_ref, mi_ref, sk_ref, o_ref, ks_ref, *, L):
    n2 = 2 * L

    @pl.when(pl.program_id(0) == 0)
    def _():
        ks_ref[...] = _dot(mk_ref[...], kern_ref[...].astype(BF))

    z = jnp.concatenate([v_ref[0], v_ref[1]], 0)
    u = _dot(mf_ref[...], z)
    ur, ui = u[:n2], u[n2:]
    kr, ki = ks_ref[:n2, :], ks_ref[n2:, :]
    y = jnp.concatenate([ur * kr - ui * ki, ur * ki + ui * kr], 0).astype(BF)
    t = _dot(mi_ref[...], y)
    for j in range(2):
        conv = t[j * L:(j + 1) * L]
        o_ref[j] = ((conv + v_ref[j].astype(F32) * sk_ref[...]) * x0_ref[j].astype(F32)).astype(BF)


def _fftconv_short(v, x0, kern, skip):
    B, L, _ = v.shape
    n2 = 2 * L
    k = jnp.arange(n2, dtype=jnp.int32)
    s = jnp.arange(L, dtype=jnp.int32)
    fr, fi = _cis(k, s, n2, -1.0)
    mf = _cplx_block(fr, fi).astype(BF)
    kr, ki = _cis(k, k, n2, -1.0)
    mk = jnp.concatenate([kr, ki], 0).astype(BF)
    ir, ii = _cis(s, k, n2, 1.0, 1.0 / n2)
    mi = _cplx_block(ir, ii).astype(BF)
    pair = lambda p: (p, 0, 0)
    return pl.pallas_call(
        functools.partial(_fft_short_kernel, L=L),
        grid=(B // 2,),
        in_specs=[pl.BlockSpec((2, L, D), pair), pl.BlockSpec((2, L, D), pair), _const((n2, D)),
                  _const((2 * n2, n2)), _const((2 * n2, 2 * L)), _const((2 * L, 2 * n2)), _const((1, D))],
        out_specs=pl.BlockSpec((2, L, D), pair),
        out_shape=jax.ShapeDtypeStruct((B, L, D), BF),
        scratch_shapes=[pltpu.VMEM((2 * n2, D), F32)],
        compiler_params=_params(1),
    )(v, x0, kern, mk, mf, mi, skip.reshape(1, D))


def _lmul_kernel(m_ref, x_ref, o_ref):
    o_ref[0] = _dot(m_ref[...], x_ref[0].astype(BF)).astype(o_ref.dtype)


def _lmul(m, x, cb=8192):
    G, K, NC = x.shape
    R = m.shape[0]
    return pl.pallas_call(
        _lmul_kernel,
        grid=(G, NC // cb),
        in_specs=[pl.BlockSpec((R, K), lambda g, c: (0, 0)), pl.BlockSpec((1, K, cb), lambda g, c: (g, 0, c))],
        out_specs=pl.BlockSpec((1, R, cb), lambda g, c: (g, 0, c)),
        out_shape=jax.ShapeDtypeStruct((G, R, NC), BF),
        compiler_params=_params(2),
    )(m, x)


def _fft_mid_kernel(a_ref, af_ref, mf_ref, mi_ref, o_ref):
    s2 = FFT_S2
    n_pair = a_ref.shape[0]
    mf = mf_ref[0]
    ks = _dot(mf, jnp.concatenate([af_ref[0, 0], af_ref[1, 0]], 0))
    kr, ki = ks[:s2], ks[s2:]
    us = [_dot(mf, jnp.concatenate([a_ref[p, 0, 0], a_ref[p, 1, 0]], 0)) for p in range(n_pair)]
    ys = [jnp.concatenate([u[:s2] * kr - u[s2:] * ki, u[:s2] * ki + u[s2:] * kr], 0).astype(BF) for u in us]
    zs = [_dot(mi_ref[0], y).astype(BF) for y in ys]
    for p, z in enumerate(zs):
        o_ref[p, 0, 0] = z[:s2]
        o_ref[p, 1, 0] = z[s2:]


def _lmul_out_kernel(m_ref, z_ref, v_ref, x0_ref, sk_ref, o_ref):
    conv = _dot(m_ref[...], z_ref[0])
    o_ref[0] = ((conv + v_ref[0].astype(F32) * sk_ref[...]) * x0_ref[0].astype(F32)).astype(BF)


def _fftconv_long(v, x0, kern, skip, cb=8192):
    B, L, _ = v.shape
    n2 = 2 * L
    s2 = FFT_S2
    s1 = n2 // s2
    s1h = s1 // 2
    G = B // 2
    nc = s2 * D
    i1 = jnp.arange(s1, dtype=jnp.int32)
    i1h = jnp.arange(s1h, dtype=jnp.int32)
    i2 = jnp.arange(s2, dtype=jnp.int32)
    cr, ci = _cis(i1, i1h, s1, -1.0)
    m1 = _cplx_block(cr, ci).astype(BF)
    cr, ci = _cis(i1, i1, s1, -1.0)
    m1f = jnp.concatenate([cr, ci], 0).astype(BF)
    kk = (i1[:, None] + s1 * i2[None, :]).reshape(-1)
    gr, gi = _cis(kk, i2, n2, -1.0)
    mf = jax.vmap(_cplx_block)(gr.reshape(s1, s2, s2), gi.reshape(s1, s2, s2)).astype(BF)
    hr, hi = _cis(i2, kk, n2, 1.0, 1.0 / n2)
    hr = hr.reshape(s2, s1, s2).transpose(1, 0, 2)
    hi = hi.reshape(s2, s1, s2).transpose(1, 0, 2)
    mi = jax.vmap(_cplx_block)(hr, hi).astype(BF)
    er, ei = _cis(i1h, i1, s1, 1.0)
    m3 = _cplx_block(er, ei).astype(BF)

    a = _lmul(m1, v.reshape(G, 2 * s1h, nc), cb)
    af = _lmul(m1f, kern.reshape(1, s1, nc), cb)
    zz = pl.pallas_call(
        _fft_mid_kernel,
        grid=(s1,),
        in_specs=[pl.BlockSpec((G, 2, 1, s2, D), lambda k: (0, 0, k, 0, 0)),
                  pl.BlockSpec((2, 1, s2, D), lambda k: (0, k, 0, 0)),
                  pl.BlockSpec((1, 2 * s2, 2 * s2), lambda k: (k, 0, 0)),
                  pl.BlockSpec((1, 2 * s2, 2 * s2), lambda k: (k, 0, 0))],
        out_specs=pl.BlockSpec((G, 2, 1, s2, D), lambda k: (0, 0, k, 0, 0)),
        out_shape=jax.ShapeDtypeStruct((G, 2, s1, s2, D), BF),
        compiler_params=_params(1),
    )(a.reshape(G, 2, s1, s2, D), af.reshape(2, s1, s2, D), mf, mi)
    blk = lambda g, c: (g, 0, c)
    out = pl.pallas_call(
        _lmul_out_kernel,
        grid=(G, nc // cb),
        in_specs=[pl.BlockSpec((2 * s1h, 2 * s1), lambda g, c: (0, 0)), pl.BlockSpec((1, 2 * s1, cb), blk),
                  pl.BlockSpec((1, 2 * s1h, cb), blk), pl.BlockSpec((1, 2 * s1h, cb), blk),
                  pl.BlockSpec((1, cb), lambda g, c: (0, 0))],
        out_specs=pl.BlockSpec((1, 2 * s1h, cb), blk),
        out_shape=jax.ShapeDtypeStruct((G, 2 * s1h, nc), BF),
        compiler_params=_params(2),
    )(m3, zz.reshape(G, 2 * s1, nc), v.reshape(G, 2 * s1h, nc), x0.reshape(G, 2 * s1h, nc),
      jnp.tile(skip.reshape(1, D), (1, cb // D)))
    return out.reshape(B, L, D)


def _hy_out_kernel(vx_ref, w_ref, b_ref, x_ref, g1_ref, ng1_ref, ng2_ref, sc_ref, sh_ref, x1_ref, h_ref):
    bb, tl, _ = x_ref.shape
    m = _dot(vx_ref[...].reshape(bb * tl, D), w_ref[...]) + b_ref[...]
    x1 = x_ref[...] + g1_ref[...] * _rms(m, ng1_ref[0]).reshape(bb, tl, D)
    x1_ref[...] = x1
    h_ref[...] = (_rms(x1, ng2_ref[...]) * (1.0 + sc_ref[...]) + sh_ref[...]).astype(h_ref.dtype)


def _hy_out(vx, w, b, x, g1, ng1, ng2, sc, sh):
    B, L, _ = x.shape
    r = _Rows(B, L)
    return pl.pallas_call(
        _hy_out_kernel,
        grid=(r.n,),
        in_specs=[r.act(D), _const((D, D)), _const((1, D)), r.act(D), r.mod(g1), _const((1, 1, D)),
                  _const((1, 1, D)), r.mod(sc), r.mod(sh)],
        out_specs=[r.act(D), r.act(D)],
        out_shape=[jax.ShapeDtypeStruct((B, L, D), F32), jax.ShapeDtypeStruct((B, L, D), BF)],
        compiler_params=_params(1),
    )(vx, w.astype(BF), b.reshape(1, D), x, g1, ng1.reshape(1, 1, D), ng2.reshape(1, 1, D), sc, sh)


def _swiglu_kernel(be_ref, nv_ref, x_ref, wg_ref, wu_ref, wd_ref, o_ref, acc_ref):
    del be_ref
    f = pl.program_id(1)

    @pl.when(f == 0)
    def _():
        acc_ref[...] = jnp.zeros_like(acc_ref)

    @pl.when(pl.program_id(0) < nv_ref[0])
    def _():
        x = x_ref[...].astype(BF)
        tf = wg_ref.shape[2]
        sub = 256 if tf % 256 == 0 else tf
        part = None
        for c in range(tf // sub):
            cols = slice(c * sub, (c + 1) * sub)
            g = _dot(x, wg_ref[0, :, cols])
            u = _dot(x, wu_ref[0, :, cols])
            y = _dot((_silu(g) * u).astype(BF), wd_ref[0, cols, :])
            part = y if part is None else part + y
        acc_ref[...] += part

    @pl.when(f == pl.num_programs(1) - 1)
    def _():
        o_ref[...] = acc_ref[...]


def _swiglu(x, w_gu, w_d, block_e, n_valid, tm, tf):
    rows = x.shape[0]
    F = w_d.shape[1]
    nf = F // tf
    assert rows % tm == 0 and F % tf == 0
    grid_spec = pltpu.PrefetchScalarGridSpec(
        num_scalar_prefetch=2,
        grid=(rows // tm, nf),
        in_specs=[pl.BlockSpec((tm, D), lambda i, f, be, nv: (i, 0)),
                  pl.BlockSpec((1, D, tf), lambda i, f, be, nv: (be[i], 0, f)),
                  pl.BlockSpec((1, D, tf), lambda i, f, be, nv: (be[i], 0, nf + f)),
                  pl.BlockSpec((1, tf, D), lambda i, f, be, nv: (be[i], f, 0))],
        out_specs=pl.BlockSpec((tm, D), lambda i, f, be, nv: (i, 0)),
        scratch_shapes=[pltpu.VMEM((tm, D), F32)],
    )
    return pl.pallas_call(
        _swiglu_kernel,
        grid_spec=grid_spec,
        out_shape=jax.ShapeDtypeStruct((rows, D), F32),
        compiler_params=_params(2),
    )(block_e, n_valid, x, w_gu, w_gu, w_d)


def _gla_in_kernel(x1_ref, f_ref, g2_ref, ng3_ref, ng0_ref, sc_ref, sh_ref, wqkv_ref, wg1_ref, wg2_ref, bg_ref,
                   wr_ref, br_ref, x2_ref, qkv_ref, g_ref, r_ref):
    bb, tl, _ = x1_ref.shape
    tm = bb * tl
    x2 = x1_ref[...] + g2_ref[...] * _rms(f_ref[...], ng3_ref[...])
    x2_ref[...] = x2
    h = (_rms(x2, ng0_ref[...]) * (1.0 + sc_ref[...]) + sh_ref[...]).reshape(tm, D).astype(BF)
    for c in range(4):
        cols = slice(c * KD, (c + 1) * KD)
        part = _dot(h, wqkv_ref[:, cols])
        if c == 0:
            part = part * (DK ** -0.5)
        qkv_ref[:, :, cols] = part.astype(BF).reshape(bb, tl, KD)
    low = _dot(h, wg1_ref[...]).astype(BF)
    a = _dot(low, wg2_ref[...]) + bg_ref[...]
    log_sig = jnp.minimum(a, 0.0) - jnp.log(1.0 + jnp.exp(-jnp.abs(a)))
    g_ref[...] = (log_sig / GLA_GATE_NORM).reshape(bb, tl, 2 * KD)
    r_ref[...] = _silu(_dot(h, wr_ref[...]) + br_ref[...]).astype(BF).reshape(bb, tl, D)


def _gla_in(x1, f, g2, ng3, ng0, sc, sh, qkv_w, gk_w1, gk_w2, gk_b, r_w, r_b):
    B, L, _ = x1.shape
    r = _Rows(B, L)
    w1 = jnp.zeros((D, LANES), F32).at[:, :GLA_RANK].set(gk_w1[0]).at[:, GLA_RANK:2 * GLA_RANK].set(gk_w1[1])
    w2 = jnp.zeros((LANES, 2 * KD), F32).at[:GLA_RANK, :KD].set(gk_w2[0]).at[GLA_RANK:2 * GLA_RANK, KD:].set(gk_w2[1])
    return pl.pallas_call(
        _gla_in_kernel,
        grid=(r.n,),
        in_specs=[r.act(D), r.act(D), r.mod(g2), _const((1, 1, D)), _const((1, 1, D)), r.mod(sc), r.mod(sh),
                  _const((D, 2 * KD + D)), _const((D, LANES)), _const((LANES, 2 * KD)), _const((1, 2 * KD)),
                  _const((D, D)), _const((1, D))],
        out_specs=[r.act(D), r.act(2 * KD + D), r.act(2 * KD), r.act(D)],
        out_shape=[jax.ShapeDtypeStruct((B, L, D), F32), jax.ShapeDtypeStruct((B, L, 2 * KD + D), BF),
                   jax.ShapeDtypeStruct((B, L, 2 * KD), F32), jax.ShapeDtypeStruct((B, L, D), BF)],
        compiler_params=_params(1),
    )(x1, f.reshape(B, L, D), g2, ng3.reshape(1, 1, D), ng0.reshape(1, 1, D), sc, sh, qkv_w.astype(BF),
      w1.astype(BF), w2.astype(BF), gk_b.reshape(1, 2 * KD), r_w.astype(BF), r_b.reshape(1, D))


def _gla_scan_kernel(*refs, zero_init):
    if zero_init:
        qf_ref, qb_ref, gf_ref, gb_ref, of_ref, ob_ref, sn_ref, st_ref = refs
    else:
        qf_ref, qb_ref, gf_ref, gb_ref, s0_ref, of_ref, ob_ref, sn_ref, st_ref = refs
    j = pl.program_id(1)
    tl = qf_ref.shape[1]
    n_chunk = tl // CHUNK

    @pl.when(j == 0)
    def _():
        for d in range(2):
            for h in range(H):
                if zero_init:
                    st_ref[d, h] = jnp.zeros((DV, DK), F32)
                else:
                    st_ref[d, h] = s0_ref[0, 0, d, h].T

    ri = lax.broadcasted_iota(jnp.int32, (CHUNK, CHUNK), 0)
    ci = lax.broadcasted_iota(jnp.int32, (CHUNK, CHUNK), 1)
    masks = (ci <= ri, ci >= ri)

    def decays(d, g_ref, rows):
        tri = jnp.where(masks[d], 1.0, 0.0).astype(BF)
        g = g_ref[0, rows, :]
        g_hi = g.astype(BF)
        g_lo = (g - g_hi.astype(F32)).astype(BF)
        b = _dot(tri, g_hi) + _dot(tri, g_lo)
        b_last = b[CHUNK - 1:CHUNK, :] if d == 0 else b[0:1, :]
        return jnp.exp(b), jnp.exp(-b), jnp.exp(b_last - b), jnp.exp(b_last)

    def chunk_rows(c):
        return (pl.ds(pl.multiple_of(c * CHUNK, CHUNK), CHUNK),
                pl.ds(pl.multiple_of((n_chunk - 1 - c) * CHUNK, CHUNK), CHUNK))

    def body(c, carry):
        rows = chunk_rows(c)
        q_refs, o_refs = (qf_ref, qb_ref), (of_ref, ob_ref)
        fac = (decays(0, gf_ref, rows[0]), decays(1, gb_ref, rows[1]))
        chains = [(d, h) for h in range(H) for d in range(2)]
        qd, ksc, att, v = {}, {}, {}, {}
        for d, h in chains:
            e_pos, e_neg, e_rem, _ = fac[d]
            ks = slice(h * DK, (h + 1) * DK)
            q = q_refs[d][0, rows[d], h * DK:(h + 1) * DK].astype(F32)
            k = q_refs[d][0, rows[d], KD + h * DK:KD + (h + 1) * DK].astype(F32)
            qd[d, h] = (q * e_pos[:, ks]).astype(BF)
            ksc[d, h] = (k * e_rem[:, ks]).astype(BF)
            att[d, h] = _dot_nt(qd[d, h], (k * e_neg[:, ks]).astype(BF))
        for d, h in chains:
            v[d, h] = q_refs[d][0, rows[d], 2 * KD + h * DV:2 * KD + (h + 1) * DV]
            a = jnp.where(masks[d], att[d, h], 0.0).astype(BF)
            o = _dot(a, v[d, h]) + _dot_nt(qd[d, h], st_ref[d, h].astype(BF))
            o_refs[d][0, rows[d], h * DV:(h + 1) * DV] = o.astype(o_refs[d].dtype)
        for d, h in chains:
            ks = slice(h * DK, (h + 1) * DK)
            st_ref[d, h] = st_ref[d, h] * fac[d][3][:, ks] + _dot_tn(v[d, h], ksc[d, h])
        return carry

    lax.fori_loop(0, n_chunk, body, 0)

    @pl.when(j == pl.num_programs(1) - 1)
    def _():
        for d in range(2):
            for h in range(H):
                sn_ref[0, 0, d, h] = st_ref[d, h].T


def _gla_scan(qkv, g, s0):
    B, L, _ = qkv.shape
    tl = min(L, ROW_TILE)
    nl = L // tl
    zero_init = s0 is None
    wq = 2 * KD + D
    in_specs = [pl.BlockSpec((1, tl, wq), lambda b, j: (b, j, 0)),
                pl.BlockSpec((1, tl, wq), lambda b, j: (b, nl - 1 - j, 0)),
                pl.BlockSpec((1, tl, KD), lambda b, j: (b, j, 0)),
                pl.BlockSpec((1, tl, KD), lambda b, j: (b, nl - 1 - j, 1))]
    args = [qkv, qkv, g, g]
    st_spec = pl.BlockSpec((1, 1, 2, H, DK, DV), lambda b, j: (b, 0, 0, 0, 0, 0))
    if not zero_init:
        in_specs.append(st_spec)
        args.append(s0)
    return pl.pallas_call(
        functools.partial(_gla_scan_kernel, zero_init=zero_init),
        grid=(B, nl),
        in_specs=in_specs,
        out_specs=[pl.BlockSpec((1, tl, D), lambda b, j: (b, j, 0)),
                   pl.BlockSpec((1, tl, D), lambda b, j: (b, nl - 1 - j, 0)), st_spec],
        out_shape=[jax.ShapeDtypeStruct((B, L, D), BF), jax.ShapeDtypeStruct((B, L, D), BF),
                   jax.ShapeDtypeStruct((B, 1, 2, H, DK, DV), F32)],
        scratch_shapes=[pltpu.VMEM((2, H, DV, DK), F32)],
        compiler_params=_params(2),
    )(*args)


def _gla_out_kernel(of_ref, ob_ref, r_ref, on_ref, w_ref, x_ref, g1_ref, ng1_ref, ng2_ref, sc_ref, sh_ref, rw_ref,
                    x3_ref, h_ref, route_ref, rt_ref, cnt_ref):
    bb, tl, _ = x_ref.shape
    tm = bb * tl
    o = of_ref[...].astype(F32) + ob_ref[...].astype(F32)
    heads = []
    for h in range(H):
        oh = o[:, :, h * DV:(h + 1) * DV]
        heads.append(oh * lax.rsqrt(jnp.mean(oh * oh, axis=-1, keepdims=True) + RMS_EPS))
    o = jnp.concatenate(heads, -1) * on_ref[...] * r_ref[...].astype(F32)
    m = _dot(o.reshape(tm, D).astype(BF), w_ref[...])
    x3 = x_ref[...] + g1_ref[...] * _rms(m, ng1_ref[0]).reshape(bb, tl, D)
    x3_ref[...] = x3
    h2 = _rms(x3, ng2_ref[...]) * (1.0 + sc_ref[...]) + sh_ref[...]
    h_hi = h2.astype(BF)
    h_ref[...] = h_hi
    h_lo = (h2 - h_hi.astype(F32)).reshape(tm, D).astype(BF)
    both = _dot(h_hi.reshape(tm, D), rw_ref[...])
    logits = both[:, :LANES] + both[:, LANES:] + _dot(h_lo, rw_ref[:, :LANES])
    lane_i = lax.broadcasted_iota(jnp.int32, (tm, LANES), 1)
    lane = lane_i.astype(F32)
    logits = jnp.where(lane_i < N_EXPERTS, logits, -jnp.inf)
    m0 = jnp.max(logits, axis=-1, keepdims=True)
    i0 = jnp.min(jnp.where(logits == m0, lane, float(LANES)), axis=-1, keepdims=True)
    rest = jnp.where(lane == i0, -jnp.inf, logits)
    m1 = jnp.max(rest, axis=-1, keepdims=True)
    i1 = jnp.min(jnp.where(rest == m1, lane, float(LANES)), axis=-1, keepdims=True)
    e = jnp.exp(m1 - m0)
    w0 = 1.0 / (1.0 + e)
    w1 = e * w0
    onehot = jnp.where((lane == i0) | (lane == i1), 1.0, 0.0)
    ri = lax.broadcasted_iota(jnp.int32, (tm, tm), 0)
    ci = lax.broadcasted_iota(jnp.int32, (tm, tm), 1)
    same_tile = (ri // MOE_TILE) == (ci // MOE_TILE)
    before = jnp.where(same_tile, jnp.where(ci < ri, 1.0, 0.0), 0.0).astype(BF)
    prior = _dot(before, onehot.astype(BF))
    rank0 = jnp.sum(jnp.where(lane == i0, prior, 0.0), axis=-1, keepdims=True)
    rank1 = jnp.sum(jnp.where(lane == i1, prior, 0.0), axis=-1, keepdims=True)
    fields = (w0, w1, i0, i1, rank0, rank1)
    route = jnp.zeros((tm, LANES), F32)
    for n, val in enumerate(fields):
        route = jnp.where(lane_i == n, val, route)
    route_ref[...] = route
    route_t = route.T
    for s in range(tm // MOE_TILE):
        rows = slice(s * MOE_TILE, (s + 1) * MOE_TILE)
        cnt_ref[s] = jnp.broadcast_to(jnp.sum(onehot[rows], axis=0, keepdims=True), cnt_ref.shape[1:])
        rt_ref[s] = route_t[0:SUBLANES, rows]


def _gla_out(o_f, o_b, r_gate, onorm_g, w, x, g1, ng1, ng2, sc, sh, router_w):
    B, L, _ = x.shape
    r = _Rows(B, L)
    rw = jnp.pad(router_w, ((0, 0), (0, LANES - N_EXPERTS)))
    rw_hi = rw.astype(BF)
    rw = jnp.concatenate([rw_hi, (rw - rw_hi.astype(F32)).astype(BF)], axis=1)
    tm = r.tm
    n_sub = tm // MOE_TILE
    return pl.pallas_call(
        _gla_out_kernel,
        grid=(r.n,),
        in_specs=[r.act(D), r.act(D), r.act(D), _const((1, 1, D)), _const((D, D)), r.act(D), r.mod(g1),
                  _const((1, 1, D)), _const((1, 1, D)), r.mod(sc), r.mod(sh), _const((D, 2 * LANES))],
        out_specs=[r.act(D), r.act(D), pl.BlockSpec((tm, LANES), lambda i: (i, 0)),
                   pl.BlockSpec((n_sub, SUBLANES, MOE_TILE), lambda i: (i, 0, 0)),
                   pl.BlockSpec((n_sub, SUBLANES, LANES), lambda i: (i, 0, 0))],
        out_shape=[jax.ShapeDtypeStruct((B, L, D), F32), jax.ShapeDtypeStruct((B, L, D), BF),
                   jax.ShapeDtypeStruct((B * L, LANES), F32),
                   jax.ShapeDtypeStruct((r.n * n_sub, SUBLANES, MOE_TILE), F32),
                   jax.ShapeDtypeStruct((r.n * n_sub, SUBLANES, LANES), F32)],
        compiler_params=_params(1),
    )(o_f, o_b, r_gate, jnp.tile(onorm_g, H).reshape(1, 1, D), w.astype(BF), x, g1, ng1.reshape(1, 1, D),
      ng2.reshape(1, 1, D), sc, sh, rw)


SEG = 64
MAIN = 2


def _seg_start(seg_ref, step, e):
    return pl.multiple_of(seg_ref[step * N_EXPERTS + e], SUBLANES)


def _dispatch_kernel(seg_ref, cnt_ref, h_ref, rt_ref, xs_in_hbm, xs_hbm, stage, extra, sem, esem):
    del xs_in_hbm
    i = pl.program_id(0)
    slot = i % 2
    tm = h_ref.shape[0]
    n_piece = tm // SEG
    h = h_ref[...]
    assert n_piece == 2 * MAIN
    ex0, ex1, lr0, lr1 = (jnp.broadcast_to(rt_ref[0, n:n + 1, :], (MAIN * SEG, tm)) for n in range(2, 6))
    rank_row = lax.broadcasted_iota(jnp.int32, (MAIN * SEG, tm), 0).astype(F32)

    def row0(step, e, k):
        return _seg_start(seg_ref, step, e) + k * SEG

    def copy(step, buf, e, k):
        return pltpu.make_async_copy(stage.at[buf, e, pl.ds(k * SEG, SEG)], xs_hbm.at[pl.ds(row0(step, e, k), SEG)],
                                     sem.at[buf, e * MAIN + k])

    def live(step, e, k):
        return k * SEG < cnt_ref[step * N_EXPERTS + e]

    def each_main(step, buf, fn):
        for e in range(N_EXPERTS):
            for k in range(MAIN):
                @pl.when(live(step, e, k))
                def _(e=e, k=k):
                    fn(copy(step, buf, e, k))

    ranks = []
    for e in range(N_EXPERTS):
        ef = float(e)
        ranks.append(jnp.where(ex0 == ef, lr0, jnp.where(ex1 == ef, lr1, -1.0)))
        pick = jnp.where(ranks[e] == rank_row, 1.0, 0.0).astype(BF)
        stage[slot, e] = _dot(pick, h)

    @pl.when(i > 0)
    def _():
        each_main(i - 1, 1 - slot, lambda cp: cp.wait())

    each_main(i, slot, lambda cp: cp.start())
    for e in range(N_EXPERTS):
        @pl.when(live(i, e, MAIN))
        def _(e=e):
            more = jnp.where(ranks[e] - float(MAIN * SEG) == rank_row, 1.0, 0.0).astype(BF)
            extra[...] = _dot(more, h)
            for k in range(MAIN, n_piece):
                @pl.when(live(i, e, k))
                def _(k=k):
                    cp = pltpu.make_async_copy(extra.at[pl.ds((k - MAIN) * SEG, SEG)],
                                               xs_hbm.at[pl.ds(row0(i, e, k), SEG)], esem)
                    cp.start()
                    cp.wait()

    @pl.when(i == pl.num_programs(0) - 1)
    def _():
        each_main(i, slot, lambda cp: cp.wait())


def _dispatch(h2, route_t, seg, cnt, xs, tm=MOE_TILE):
    T = h2.shape[0]
    grid_spec = pltpu.PrefetchScalarGridSpec(
        num_scalar_prefetch=2,
        grid=(T // tm,),
        in_specs=[pl.BlockSpec((tm, D), lambda i, s, c: (i, 0)),
                  pl.BlockSpec((1, SUBLANES, tm), lambda i, s, c: (i, 0, 0)),
                  pl.BlockSpec(memory_space=pl.ANY)],
        out_specs=pl.BlockSpec(memory_space=pl.ANY),
        scratch_shapes=[pltpu.VMEM((2, N_EXPERTS, MAIN * SEG, D), F32), pltpu.VMEM((MAIN * SEG, D), F32),
                        pltpu.SemaphoreType.DMA((2, N_EXPERTS * MAIN)), pltpu.SemaphoreType.DMA(())],
    )
    return pl.pallas_call(
        _dispatch_kernel,
        grid_spec=grid_spec,
        out_shape=jax.ShapeDtypeStruct(xs.shape, F32),
        input_output_aliases={4: 0},
        compiler_params=_params(1),
    )(seg, cnt, h2, route_t, xs)


def _combine_kernel(seg_ref, cnt_ref, ys_hbm, route_ref, x_ref, g2_ref, ng_ref, o_ref, gbuf, extra, acc_ref, sem,
                    esem):
    i = pl.program_id(0)
    bb, tl, _ = x_ref.shape
    tm = bb * tl
    n_piece = tm // SEG

    def main_copy(step, slot, e, k):
        src = ys_hbm.at[pl.ds(_seg_start(seg_ref, step, e) + k * SEG, SEG)]
        return pltpu.make_async_copy(src, gbuf.at[slot, pl.ds((e * MAIN + k) * SEG, SEG)], sem.at[slot, e * MAIN + k])

    def fetch(step, slot):
        for e in range(N_EXPERTS):
            for k in range(MAIN):
                main_copy(step, slot, e, k).start()

    @pl.when(i == 0)
    def _():
        fetch(0, 0)

    @pl.when(i + 1 < pl.num_programs(0))
    def _():
        fetch(i + 1, (i + 1) % 2)

    slot = i % 2
    route = route_ref[...]

    def spread_cols(e, r0, width):
        ef = float(e)
        gate = jnp.where(route[:, 2:3] == ef, route[:, 0:1], jnp.where(route[:, 3:4] == ef, route[:, 1:2], 0.0))
        rank = jnp.where(route[:, 2:3] == ef, route[:, 4:5], jnp.where(route[:, 3:4] == ef, route[:, 5:6], -1.0))
        col = lax.broadcasted_iota(jnp.int32, (tm, width), 1).astype(F32) + float(r0)
        return (jnp.where(rank == col, 1.0, 0.0) * gate).astype(BF)

    spread = jnp.concatenate([spread_cols(e, 0, MAIN * SEG) for e in range(N_EXPERTS)], axis=1)
    for e in range(N_EXPERTS):
        for k in range(MAIN):
            main_copy(i, slot, e, k).wait()
    acc_ref[...] = _dot(spread, gbuf[slot].astype(BF))
    for e in range(N_EXPERTS):
        for k in range(MAIN, n_piece):
            @pl.when(k * SEG < cnt_ref[i * N_EXPERTS + e])
            def _(e=e, k=k):
                src = ys_hbm.at[pl.ds(_seg_start(seg_ref, i, e) + k * SEG, SEG)]
                cp = pltpu.make_async_copy(src, extra, esem)
                cp.start()
                more = spread_cols(e, k * SEG, SEG)
                cp.wait()
                acc_ref[...] += _dot(more, extra[...].astype(BF))

    o_ref[...] = x_ref[...] + g2_ref[...] * _rms(acc_ref[...], ng_ref[0]).reshape(bb, tl, D)


def _combine(ys, route, seg, cnt, x, g2, ng):
    B, L, _ = x.shape
    r = _Rows(B, L, MOE_TILE)
    tm = r.tm
    nl = r.nl
    act = pl.BlockSpec((r.bb, r.tl, D), lambda i, s, c: (i // nl, i % nl, 0))
    g2_spec = (pl.BlockSpec((1, 1, D), lambda i, s, c: (0, 0, 0)) if g2.shape[0] == 1 else
               pl.BlockSpec((r.bb, 1, D), lambda i, s, c: (i // nl, 0, 0)))
    grid_spec = pltpu.PrefetchScalarGridSpec(
        num_scalar_prefetch=2,
        grid=(r.n,),
        in_specs=[pl.BlockSpec(memory_space=pl.ANY), pl.BlockSpec((tm, LANES), lambda i, s, c: (i, 0)), act, g2_spec,
                  pl.BlockSpec((1, 1, D), lambda i, s, c: (0, 0, 0))],
        out_specs=act,
        scratch_shapes=[pltpu.VMEM((2, N_EXPERTS * MAIN * SEG, D), F32), pltpu.VMEM((SEG, D), F32),
                        pltpu.VMEM((tm, D), F32), pltpu.SemaphoreType.DMA((2, N_EXPERTS * MAIN)),
                        pltpu.SemaphoreType.DMA(())],
    )
    return pl.pallas_call(
        _combine_kernel,
        grid_spec=grid_spec,
        out_shape=jax.ShapeDtypeStruct((B, L, D), F32),
        compiler_params=_params(1),
    )(seg, cnt, ys, route, x, g2, ng.reshape(1, 1, D))


def _moe(passes, ng, w_gu, w_d, tmoe=1024):
    cntb = jnp.concatenate([ps['blk_cnt'][:, 0, :N_EXPERTS] for ps in passes], axis=0).astype(jnp.int32)
    n_tok_blocks = cntb.shape[0]
    n_tok = sum(ps['h2'].shape[0] * ps['h2'].shape[1] for ps in passes)
    held = (cntb + SUBLANES - 1) // SUBLANES * SUBLANES
    before = jnp.cumsum(held, axis=0) - held
    cnt = jnp.sum(held, axis=0)
    p_cnt = (cnt + MAIN * SEG + tmoe - 1) // tmoe * tmoe
    p_end = jnp.cumsum(p_cnt)
    p_start = p_end - p_cnt
    seg = p_start[None, :] + before
    n_rows = 2 * n_tok + N_EXPERTS * (n_tok_blocks * (SUBLANES - 1) + tmoe + MAIN * SEG)
    n_rows = (n_rows + tmoe - 1) // tmoe * tmoe
    n_blocks = n_rows // tmoe
    starts = jnp.arange(n_blocks, dtype=jnp.int32) * tmoe
    block_e = jnp.minimum(jnp.sum(starts[:, None] >= p_end[None, :], axis=-1), N_EXPERTS - 1).astype(jnp.int32)
    n_valid = (p_end[-1:] // tmoe).astype(jnp.int32)
    xs = jnp.zeros((n_rows, D), F32)
    first = 0
    for ps in passes:
        B, L, _ = ps['h2'].shape
        nb = ps['blk_cnt'].shape[0]
        ps['seg'] = seg[first:first + nb].reshape(-1)
        ps['cnt'] = cntb[first:first + nb].reshape(-1)
        first += nb
        xs = _dispatch(ps['h2'].reshape(B * L, D), ps['route_t'], ps['seg'], ps['cnt'], xs)
    ys = _swiglu(xs, w_gu, w_d, block_e, n_valid, tmoe, D_FF_EXPERT // 2)
    return [_combine(ys, ps['route'], ps['seg'], ps['cnt'], ps['x3'], ps['g2'], ng) for ps in passes]


def _trunk(x, mods, n_row, s0, p):
    B, L, _ = x.shape
    T = B * L
    ng = p['norm_g']
    sh1, sc1, g1, sh2, sc2, g2 = mods[0]
    v, x0 = _hy_in(x, ng[0, 0], sc1, sh1, p['hy_in_w'][0], p['hy_in_b'][0], p['hy_sc_w'][0], p['hy_sc_b'][0], n_row)
    kern = _hyena_kernel_taps(L, p['hy_f_w1'][0], p['hy_f_b1'][0], p['hy_f_w2'][0], p['hy_f_b2'][0],
                              p['hy_f_freq'][0], p['hy_f_w3'][0])
    if L <= 512:
        vx = _fftconv_short(v, x0, kern, p['hy_skip'][0])
    else:
        vx = _fftconv_long(v, x0, kern, p['hy_skip'][0])
    x1, h2 = _hy_out(vx, p['hy_out_w'][0], p['hy_out_b'][0], x, g1, ng[0, 1], ng[0, 2], sc2, sh2)
    tm = 1024
    ones = jnp.zeros((T // tm,), jnp.int32)
    f = _swiglu(h2.reshape(T, D), p['ffn_wgu'], p['ffn_wd'], ones, jnp.full((1,), T // tm, jnp.int32), tm, D_FF)
    g2_0, ng3_0 = g2, ng[0, 3]
    sh1, sc1, g1, sh2, sc2, g2 = mods[1]
    x2, qkv, gate, r_gate = _gla_in(x1, f, g2_0, ng3_0, ng[1, 0], sc1, sh1, p['gla_qkv_w'][0], p['gla_gk_w1'][0],
                                    p['gla_gk_w2'][0], p['gla_gk_b'][0], p['gla_r_w'][0], p['gla_r_b'][0])
    o_f, o_b, s_new = _gla_scan(qkv, gate, s0)
    x3, h2, route, route_t, blk_cnt = _gla_out(o_f, o_b, r_gate, p['gla_onorm_g'][0], p['gla_out_w'][0], x2, g1,
                                               ng[1, 1], ng[1, 2], sc2, sh2, p['moe_router'][0])
    return dict(x3=x3, h2=h2, route=route, route_t=route_t, blk_cnt=blk_cnt, g2=g2), s_new


def kernel(x_prompt, x_sample, state_gla, c, c_ctx, ada_w, ada_b, norm_g, hy_in_w, hy_in_b, hy_sc_w, hy_sc_b, hy_f_w1, hy_f_b1, hy_f_w2, hy_f_b2, hy_f_freq, hy_f_w3, hy_skip, hy_out_w, hy_out_b, gla_qkv_w, gla_gk_w1, gla_gk_w2, gla_gk_b, gla_r_w, gla_r_b, gla_onorm_g, gla_out_w, ffn_wgu, ffn_wd, moe_router, moe_wgu, moe_wd):
    p = dict(norm_g=norm_g, hy_in_w=hy_in_w, hy_in_b=hy_in_b, hy_sc_w=hy_sc_w, hy_sc_b=hy_sc_b, hy_f_w1=hy_f_w1,
             hy_f_b1=hy_f_b1, hy_f_w2=hy_f_w2, hy_f_b2=hy_f_b2, hy_f_freq=hy_f_freq, hy_f_w3=hy_f_w3,
             hy_skip=hy_skip, hy_out_w=hy_out_w, hy_out_b=hy_out_b, gla_qkv_w=gla_qkv_w, gla_gk_w1=gla_gk_w1,
             gla_gk_w2=gla_gk_w2, gla_gk_b=gla_gk_b, gla_r_w=gla_r_w, gla_r_b=gla_r_b, gla_onorm_g=gla_onorm_g,
             gla_out_w=gla_out_w, ffn_wgu=ffn_wgu.astype(BF), ffn_wd=ffn_wd.astype(BF),
             moe_router=moe_router, moe_wgu=moe_wgu[0].astype(BF), moe_wd=moe_wd[0].astype(BF))
    n_dec = c.shape[0]
    cond = jnp.concatenate([c_ctx[None, :], c, jnp.zeros((16 - 1 - n_dec, D), F32)], axis=0)
    mod = _ada(cond, ada_w, ada_b)
    mods_ctx = [[m[:, None, :] for m in jnp.split(mod[l, 0:1], 6, axis=-1)] for l in range(DEPTH)]
    mods_dec = [[m[:, None, :] for m in jnp.split(mod[l, 1:1 + n_dec], 6, axis=-1)] for l in range(DEPTH)]
    ctx, state_new = _trunk(x_prompt, mods_ctx, x_prompt.shape[1], None, p)
    grid_w = 64
    dec, _ = _trunk(x_sample, mods_dec, grid_w, state_gla, p)
    y_prompt, y_sample = _moe([ctx, dec], norm_g[1, 3], p['moe_wgu'], p['moe_wd'])
    return y_prompt, y_sample, state_new
```

```python
import functools
import math

import jax
import jax.numpy as jnp
from jax import lax
from jax.experimental import pallas as pl
from jax.experimental.pallas import tpu as pltpu

F32 = jnp.float32
BF = jnp.bfloat16

D = 1024
RMS_EPS = 1e-6
DEPTH = 2
HY_SHORT = 3
HY_EMB = 33
HY_BANDS = (HY_EMB - 1) // 2
HY_FFN = 64
HY_MAX_DECAY = math.log(1e-2) / 0.3
HY_MIN_DECAY = math.log(1e-2) / 1.5
H = 4
DK = 128
DV = 256
KD = H * DK
GLA_RANK = 16
GLA_GATE_NORM = 16.0
CHUNK = 64
D_FF = 11 * D // 4
N_EXPERTS = 8
D_FF_EXPERT = 7 * D // 2

LANES = 128
SUBLANES = 8
VMEM_LIMIT_BYTES = 56 * 1024 * 1024
ROW_TILE = 512
MOE_TILE = 512
PACKED_ROWS = 16
FFT_S2 = 128


def _params(n_axes):
    return pltpu.CompilerParams(dimension_semantics=("arbitrary",) * n_axes,
                                vmem_limit_bytes=VMEM_LIMIT_BYTES)


def _dot(a, b):
    return jnp.dot(a, b, preferred_element_type=F32)


def _dot_nt(a, b):
    return lax.dot_general(a, b, (((1,), (1,)), ((), ())), preferred_element_type=F32)


def _dot_tn(a, b):
    return lax.dot_general(a, b, (((0,), (0,)), ((), ())), preferred_element_type=F32)


def _rms(x, g):
    return x * lax.rsqrt(jnp.mean(x * x, axis=-1, keepdims=True) + RMS_EPS) * g


def _silu(x):
    return x * (1.0 / (1.0 + jnp.exp(-x)))


class _Rows:
    def __init__(self, B, L, tm=ROW_TILE):
        self.B, self.L = B, L
        if L >= tm:
            self.bb, self.tl = 1, tm
        else:
            self.bb, self.tl = tm // L, L
        assert L % self.tl == 0 and B % self.bb == 0
        self.nl = L // self.tl
        self.n = (B // self.bb) * self.nl
        self.tm = self.bb * self.tl

    def act(self, width):
        nl = self.nl
        return pl.BlockSpec((self.bb, self.tl, width), lambda i: (i // nl, i % nl, 0))

    def mod(self, m):
        nl = self.nl
        if m.shape[0] == 1:
            return pl.BlockSpec((1, 1, D), lambda i: (0, 0, 0))
        return pl.BlockSpec((self.bb, 1, D), lambda i: (i // nl, 0, 0))


def _const(shape):
    nd = len(shape)
    return pl.BlockSpec(shape, lambda *_: (0,) * nd)


def _ada_kernel(c_ref, w_ref, b_ref, o_ref):
    cs = _silu(c_ref[...])
    o_ref[0] = _dot(cs.astype(BF), w_ref[0].astype(BF)) + b_ref[0]


def _ada(cond, ada_w, ada_b):
    R = cond.shape[0]
    tn = 1536
    return pl.pallas_call(
        _ada_kernel,
        grid=(DEPTH, 6 * D // tn),
        in_specs=[pl.BlockSpec((R, D), lambda l, n: (0, 0)),
                  pl.BlockSpec((1, D, tn), lambda l, n: (l, 0, n)),
                  pl.BlockSpec((1, 1, tn), lambda l, n: (l, 0, n))],
        out_specs=pl.BlockSpec((1, R, tn), lambda l, n: (l, 0, n)),
        out_shape=jax.ShapeDtypeStruct((DEPTH, R, 6 * D), F32),
        compiler_params=_params(2),
    )(cond, ada_w, ada_b.reshape(DEPTH, 1, 6 * D))


def _hy_in_kernel(x_ref, ng_ref, sc_ref, sh_ref, w_ref, b_ref, cw_ref, cb_ref, v_ref, x0_ref, *, n_row):
    x = x_ref[...]
    bb, tl, _ = x.shape
    tm = bb * tl
    h = _rms(x, ng_ref[...]) * (1.0 + sc_ref[...]) + sh_ref[...]
    hb = h.reshape(tm, D).astype(BF)
    pos = lax.broadcasted_iota(jnp.int32, (tm, D), 0) & (n_row - 1)
    first = pos == 0
    last = pos == n_row - 1
    parts = []
    for j in range(3):
        cols = slice(j * D, (j + 1) * D)
        u = _dot(hb, w_ref[:, cols]) + b_ref[:, cols]
        up = jnp.where(first, 0.0, pltpu.roll(u, 1, 0))
        dn = jnp.where(last, 0.0, pltpu.roll(u, tm - 1, 0))
        parts.append(cb_ref[:, cols] + up * cw_ref[0:1, cols] + u * cw_ref[1:2, cols] + dn * cw_ref[2:3, cols])
    x0, x1, v = parts
    v_ref[...] = (v * x1).astype(BF).reshape(bb, tl, D)
    x0_ref[...] = x0.astype(BF).reshape(bb, tl, D)


def _hy_in(x, ng, sc, sh, w, b, cw, cb, n_row):
    B, L, _ = x.shape
    assert n_row & (n_row - 1) == 0
    r = _Rows(B, L)
    assert r.tl % n_row == 0
    return pl.pallas_call(
        functools.partial(_hy_in_kernel, n_row=n_row),
        grid=(r.n,),
        in_specs=[r.act(D), _const((1, 1, D)), r.mod(sc), r.mod(sh),
                  _const((D, 3 * D)), _const((1, 3 * D)), _const((HY_SHORT, 3 * D)), _const((1, 3 * D))],
        out_specs=[r.act(D), r.act(D)],
        out_shape=[jax.ShapeDtypeStruct((B, L, D), BF)] * 2,
        compiler_params=_params(1),
    )(x, ng.reshape(1, 1, D), sc, sh, w.astype(BF), b.reshape(1, 3 * D), cw, cb.reshape(1, 3 * D))


def _filter_kernel(z_ref, w1_ref, b1_ref, w2_ref, b2_ref, fr_ref, w3_ref, dl_ref, o_ref, *, L):
    tr = z_ref.shape[0]
    hp = lax.Precision.HIGHEST
    z = z_ref[...]
    h = jnp.sin(fr_ref[0:1, :] * (jnp.dot(z, w1_ref[...], precision=hp, preferred_element_type=F32) + b1_ref[...]))
    h = jnp.sin(fr_ref[1:2, :] * (jnp.dot(h, w2_ref[...], precision=hp, preferred_element_type=F32) + b2_ref[...]))
    hw = _dot(h.astype(BF), w3_ref[...])
    n = pl.program_id(0) * tr + lax.broadcasted_iota(jnp.int32, (tr, D), 0)
    taps = jnp.where(n < L, hw[:, :D], hw[:, D:]) * jnp.exp(-z[:, 0:1] * dl_ref[...])
    o_ref[...] = jnp.where(n == L, 0.0, taps)


def _hyena_kernel_taps(L, w1, b1, w2, b2, freq, w3):
    n = jnp.arange(2 * L, dtype=jnp.int32)
    pos = jnp.where(n < L, n, 2 * L - n) % L
    t = (pos.astype(F32) / (L - 1))[:, None]
    w = (2.0 * math.pi * pos.astype(F32) / L)[:, None]
    f = jnp.linspace(1e-4, HY_BANDS - 1, HY_BANDS, dtype=F32)[None, :]
    z = jnp.concatenate([t, jnp.cos(f * w), -jnp.sin(f * w)], axis=-1)
    z = jnp.pad(z, ((0, 0), (0, LANES - HY_EMB)))
    pad = LANES - HY_FFN
    w1p = jnp.pad(w1, ((0, LANES - HY_EMB), (0, pad)))
    w2p = jnp.pad(w2, ((0, pad), (0, pad)))
    w3p = jnp.pad(w3, ((0, pad), (0, 0))).astype(BF)
    b1p = jnp.pad(b1, (0, pad)).reshape(1, LANES)
    b2p = jnp.pad(b2, (0, pad)).reshape(1, LANES)
    frp = jnp.pad(freq, ((0, 0), (0, pad)))
    deltas = jnp.abs(jnp.linspace(HY_MIN_DECAY, HY_MAX_DECAY, D, dtype=F32)).reshape(1, D)
    tr = 512
    return pl.pallas_call(
        functools.partial(_filter_kernel, L=L),
        grid=(2 * L // tr,),
        in_specs=[pl.BlockSpec((tr, LANES), lambda i: (i, 0)), _const((LANES, LANES)), _const((1, LANES)),
                  _const((LANES, LANES)), _const((1, LANES)), _const((2, LANES)), _const((LANES, 2 * D)),
                  _const((1, D))],
        out_specs=pl.BlockSpec((tr, D), lambda i: (i, 0)),
        out_shape=jax.ShapeDtypeStruct((2 * L, D), F32),
        compiler_params=_params(1),
    )(z, w1p, b1p, w2p, b2p, frp, w3p, deltas)


def _cis(rows, cols, n, sign, scale=1.0):
    ph = (rows[:, None] * cols[None, :]) % n
    ang = ph.astype(F32) * (2.0 * math.pi / n)
    return jnp.cos(ang) * scale, jnp.sin(ang) * (sign * scale)


def _cplx_block(cr, ci):
    return jnp.concatenate([jnp.concatenate([cr, -ci], 1), jnp.concatenate([ci, cr], 1)], 0)


def _fft_short_kernel(v_ref, x0_ref, kern_ref, mk_ref, mf_ref, mi_ref, sk_ref, o_ref, ks_ref, *, L):
    n2 = 2 * L

    @pl.when(pl.program_id(0) == 0)
    def _():
        ks_ref[...] = _dot(mk_ref[...], kern_ref[...].astype(BF))

    z = jnp.concatenate([v_ref[0], v_ref[1]], 0)
    u = _dot(mf_ref[...], z)
    ur, ui = u[:n2], u[n2:]
    kr, ki = ks_ref[:n2, :], ks_ref[n2:, :]
    y = jnp.concatenate([ur * kr - ui * ki, ur * ki + ui * kr], 0).astype(BF)
    t = _dot(mi_ref[...], y)
    for j in range(2):
        conv = t[j * L:(j + 1) * L]
        o_ref[j] = ((conv + v_ref[j].astype(F32) * sk_ref[...]) * x0_ref[j].astype(F32)).astype(BF)


def _fftconv_short(v, x0, kern, skip):
    B, L, _ = v.shape
    n2 = 2 * L
    k = jnp.arange(n2, dtype=jnp.int32)
    s = jnp.arange(L, dtype=jnp.int32)
    fr, fi = _cis(k, s, n2, -1.0)
    mf = _cplx_block(fr, fi).astype(BF)
    kr, ki = _cis(k, k, n2, -1.0)
    mk = jnp.concatenate([kr, ki], 0).astype(BF)
    ir, ii = _cis(s, k, n2, 1.0, 1.0 / n2)
    mi = _cplx_block(ir, ii).astype(BF)
    pair = lambda p: (p, 0, 0)
    return pl.pallas_call(
        functools.partial(_fft_short_kernel, L=L),
        grid=(B // 2,),
        in_specs=[pl.BlockSpec((2, L, D), pair), pl.BlockSpec((2, L, D), pair), _const((n2, D)),
                  _const((2 * n2, n2)), _const((2 * n2, 2 * L)), _const((2 * L, 2 * n2)), _const((1, D))],
        out_specs=pl.BlockSpec((2, L, D), pair),
        out_shape=jax.ShapeDtypeStruct((B, L, D), BF),
        scratch_shapes=[pltpu.VMEM((2 * n2, D), F32)],
        compiler_params=_params(1),
    )(v, x0, kern, mk, mf, mi, skip.reshape(1, D))


def _lmul_kernel(m_ref, x_ref, o_ref):
    o_ref[0] = _dot(m_ref[...], x_ref[0].astype(BF)).astype(o_ref.dtype)


def _lmul(m, x, cb=8192):
    G, K, NC = x.shape
    R = m.shape[0]
    return pl.pallas_call(
        _lmul_kernel,
        grid=(G, NC // cb),
        in_specs=[pl.BlockSpec((R, K), lambda g, c: (0, 0)), pl.BlockSpec((1, K, cb), lambda g, c: (g, 0, c))],
        out_specs=pl.BlockSpec((1, R, cb), lambda g, c: (g, 0, c)),
        out_shape=jax.ShapeDtypeStruct((G, R, NC), BF),
        compiler_params=_params(2),
    )(m, x)


def _fft_mid_kernel(a_ref, af_ref, mf_ref, mi_ref, o_ref):
    s2 = FFT_S2
    n_pair = a_ref.shape[0]
    mf = mf_ref[0]
    ks = _dot(mf, jnp.concatenate([af_ref[0, 0], af_ref[1, 0]], 0))
    kr, ki = ks[:s2], ks[s2:]
    us = [_dot(mf, jnp.concatenate([a_ref[p, 0, 0], a_ref[p, 1, 0]], 0)) for p in range(n_pair)]
    ys = [jnp.concatenate([u[:s2] * kr - u[s2:] * ki, u[:s2] * ki + u[s2:] * kr], 0).astype(BF) for u in us]
    zs = [_dot(mi_ref[0], y).astype(BF) for y in ys]
    for p, z in enumerate(zs):
        o_ref[p, 0, 0] = z[:s2]
        o_ref[p, 1, 0] = z[s2:]


def _lmul_out_kernel(m_ref, z_ref, v_ref, x0_ref, sk_ref, o_ref):
    conv = _dot(m_ref[...], z_ref[0])
    o_ref[0] = ((conv + v_ref[0].astype(F32) * sk_ref[...]) * x0_ref[0].astype(F32)).astype(BF)


def _fftconv_long(v, x0, kern, skip, cb=8192):
    B, L, _ = v.shape
    n2 = 2 * L
    s2 = FFT_S2
    s1 = n2 // s2
    s1h = s1 // 2
    G = B // 2
    nc = s2 * D
    i1 = jnp.arange(s1, dtype=jnp.int32)
    i1h = jnp.arange(s1h, dtype=jnp.int32)
    i2 = jnp.arange(s2, dtype=jnp.int32)
    cr, ci = _cis(i1, i1h, s1, -1.0)
    m1 = _cplx_block(cr, ci).astype(BF)
    cr, ci = _cis(i1, i1, s1, -1.0)
    m1f = jnp.concatenate([cr, ci], 0).astype(BF)
    kk = (i1[:, None] + s1 * i2[None, :]).reshape(-1)
    gr, gi = _cis(kk, i2, n2, -1.0)
    mf = jax.vmap(_cplx_block)(gr.reshape(s1, s2, s2), gi.reshape(s1, s2, s2)).astype(BF)
    hr, hi = _cis(i2, kk, n2, 1.0, 1.0 / n2)
    hr = hr.reshape(s2, s1, s2).transpose(1, 0, 2)
    hi = hi.reshape(s2, s1, s2).transpose(1, 0, 2)
    mi = jax.vmap(_cplx_block)(hr, hi).astype(BF)
    er, ei = _cis(i1h, i1, s1, 1.0)
    m3 = _cplx_block(er, ei).astype(BF)

    a = _lmul(m1, v.reshape(G, 2 * s1h, nc), cb)
    af = _lmul(m1f, kern.reshape(1, s1, nc), cb)
    zz = pl.pallas_call(
        _fft_mid_kernel,
        grid=(s1,),
        in_specs=[pl.BlockSpec((G, 2, 1, s2, D), lambda k: (0, 0, k, 0, 0)),
                  pl.BlockSpec((2, 1, s2, D), lambda k: (0, k, 0, 0)),
                  pl.BlockSpec((1, 2 * s2, 2 * s2), lambda k: (k, 0, 0)),
                  pl.BlockSpec((1, 2 * s2, 2 * s2), lambda k: (k, 0, 0))],
        out_specs=pl.BlockSpec((G, 2, 1, s2, D), lambda k: (0, 0, k, 0, 0)),
        out_shape=jax.ShapeDtypeStruct((G, 2, s1, s2, D), BF),
        compiler_params=_params(1),
    )(a.reshape(G, 2, s1, s2, D), af.reshape(2, s1, s2, D), mf, mi)
    blk = lambda g, c: (g, 0, c)
    out = pl.pallas_call(
        _lmul_out_kernel,
        grid=(G, nc // cb),
        in_specs=[pl.BlockSpec((2 * s1h, 2 * s1), lambda g, c: (0, 0)), pl.BlockSpec((1, 2 * s1, cb), blk),
                  pl.BlockSpec((1, 2 * s1h, cb), blk), pl.BlockSpec((1, 2 * s1h, cb), blk),
                  pl.BlockSpec((1, cb), lambda g, c: (0, 0))],
        out_specs=pl.BlockSpec((1, 2 * s1h, cb), blk),
        out_shape=jax.ShapeDtypeStruct((G, 2 * s1h, nc), BF),
        compiler_params=_params(2),
    )(m3, zz.reshape(G, 2 * s1, nc), v.reshape(G, 2 * s1h, nc), x0.reshape(G, 2 * s1h, nc),
      jnp.tile(skip.reshape(1, D), (1, cb // D)))
    return out.reshape(B, L, D)


def _hy_out_kernel(vx_ref, w_ref, b_ref, x_ref, g1_ref, ng1_ref, ng2_ref, sc_ref, sh_ref, x1_ref, h_ref):
    bb, tl, _ = x_ref.shape
    m = _dot(vx_ref[...].reshape(bb * tl, D), w_ref[...]) + b_ref[...]
    x1 = x_ref[...] + g1_ref[...] * _rms(m, ng1_ref[0]).reshape(bb, tl, D)
    x1_ref[...] = x1
    h_ref[...] = (_rms(x1, ng2_ref[...]) * (1.0 + sc_ref[...]) + sh_ref[...]).astype(h_ref.dtype)


def _hy_out(vx, w, b, x, g1, ng1, ng2, sc, sh):
    B, L, _ = x.shape
    r = _Rows(B, L)
    return pl.pallas_call(
        _hy_out_kernel,
        grid=(r.n,),
        in_specs=[r.act(D), _const((D, D)), _const((1, D)), r.act(D), r.mod(g1), _const((1, 1, D)),
                  _const((1, 1, D)), r.mod(sc), r.mod(sh)],
        out_specs=[r.act(D), r.act(D)],
        out_shape=[jax.ShapeDtypeStruct((B, L, D), F32), jax.ShapeDtypeStruct((B, L, D), BF)],
        compiler_params=_params(1),
    )(vx, w.astype(BF), b.reshape(1, D), x, g1, ng1.reshape(1, 1, D), ng2.reshape(1, 1, D), sc, sh)


def _swiglu_kernel(be_ref, nv_ref, x_ref, wg_ref, wu_ref, wd_ref, o_ref, acc_ref):
    del be_ref
    f = pl.program_id(1)

    @pl.when(f == 0)
    def _():
        acc_ref[...] = jnp.zeros_like(acc_ref)

    @pl.when(pl.program_id(0) < nv_ref[0])
    def _():
        x = x_ref[...].astype(BF)
        tf = wg_ref.shape[2]
        sub = 256 if tf % 256 == 0 else tf
        part = None
        for c in range(tf // sub):
            cols = slice(c * sub, (c + 1) * sub)
            g = _dot(x, wg_ref[0, :, cols])
            u = _dot(x, wu_ref[0, :, cols])
            y = _dot((_silu(g) * u).astype(BF), wd_ref[0, cols, :])
            part = y if part is None else part + y
        acc_ref[...] += part

    @pl.when(f == pl.num_programs(1) - 1)
    def _():
        o_ref[...] = acc_ref[...].astype(o_ref.dtype)


def _swiglu(x, w_gu, w_d, block_e, n_valid, tm, tf, out_dtype):
    rows = x.shape[0]
    F = w_d.shape[1]
    nf = F // tf
    assert rows % tm == 0 and F % tf == 0
    grid_spec = pltpu.PrefetchScalarGridSpec(
        num_scalar_prefetch=2,
        grid=(rows // tm, nf),
        in_specs=[pl.BlockSpec((tm, D), lambda i, f, be, nv: (i, 0)),
                  pl.BlockSpec((1, D, tf), lambda i, f, be, nv: (be[i], 0, f)),
                  pl.BlockSpec((1, D, tf), lambda i, f, be, nv: (be[i], 0, nf + f)),
                  pl.BlockSpec((1, tf, D), lambda i, f, be, nv: (be[i], f, 0))],
        out_specs=pl.BlockSpec((tm, D), lambda i, f, be, nv: (i, 0)),
        scratch_shapes=[pltpu.VMEM((tm, D), F32)],
    )
    return pl.pallas_call(
        _swiglu_kernel,
        grid_spec=grid_spec,
        out_shape=jax.ShapeDtypeStruct((rows, D), out_dtype),
        compiler_params=_params(2),
    )(block_e, n_valid, x, w_gu, w_gu, w_d)


def _gla_in_kernel(x1_ref, f_ref, g2_ref, ng3_ref, ng0_ref, sc_ref, sh_ref, wqkv_ref, wg1_ref, wg2_ref, bg_ref,
                   wr_ref, br_ref, x2_ref, qkv_ref, g_ref, r_ref):
    bb, tl, _ = x1_ref.shape
    tm = bb * tl
    x2 = x1_ref[...] + g2_ref[...] * _rms(f_ref[...], ng3_ref[...])
    x2_ref[...] = x2
    h = (_rms(x2, ng0_ref[...]) * (1.0 + sc_ref[...]) + sh_ref[...]).reshape(tm, D).astype(BF)
    for c in range(4):
        cols = slice(c * KD, (c + 1) * KD)
        part = _dot(h, wqkv_ref[:, cols])
        if c == 0:
            part = part * (DK ** -0.5)
        qkv_ref[:, :, cols] = part.astype(BF).reshape(bb, tl, KD)
    low = _dot(h, wg1_ref[...]).astype(BF)
    a = _dot(low, wg2_ref[...]) + bg_ref[...]
    log_sig = jnp.minimum(a, 0.0) - jnp.log(1.0 + jnp.exp(-jnp.abs(a)))
    g_ref[...] = (log_sig / GLA_GATE_NORM).reshape(bb, tl, 2 * KD)
    r_ref[...] = _silu(_dot(h, wr_ref[...]) + br_ref[...]).astype(BF).reshape(bb, tl, D)


def _gla_in(x1, f, g2, ng3, ng0, sc, sh, qkv_w, gk_w1, gk_w2, gk_b, r_w, r_b):
    B, L, _ = x1.shape
    r = _Rows(B, L)
    w1 = jnp.zeros((D, LANES), F32).at[:, :GLA_RANK].set(gk_w1[0]).at[:, GLA_RANK:2 * GLA_RANK].set(gk_w1[1])
    w2 = jnp.zeros((LANES, 2 * KD), F32).at[:GLA_RANK, :KD].set(gk_w2[0]).at[GLA_RANK:2 * GLA_RANK, KD:].set(gk_w2[1])
    return pl.pallas_call(
        _gla_in_kernel,
        grid=(r.n,),
        in_specs=[r.act(D), r.act(D), r.mod(g2), _const((1, 1, D)), _const((1, 1, D)), r.mod(sc), r.mod(sh),
                  _const((D, 2 * KD + D)), _const((D, LANES)), _const((LANES, 2 * KD)), _const((1, 2 * KD)),
                  _const((D, D)), _const((1, D))],
        out_specs=[r.act(D), r.act(2 * KD + D), r.act(2 * KD), r.act(D)],
        out_shape=[jax.ShapeDtypeStruct((B, L, D), F32), jax.ShapeDtypeStruct((B, L, 2 * KD + D), BF),
                   jax.ShapeDtypeStruct((B, L, 2 * KD), F32), jax.ShapeDtypeStruct((B, L, D), BF)],
        compiler_params=_params(1),
    )(x1, f.reshape(B, L, D), g2, ng3.reshape(1, 1, D), ng0.reshape(1, 1, D), sc, sh, qkv_w.astype(BF),
      w1.astype(BF), w2.astype(BF), gk_b.reshape(1, 2 * KD), r_w.astype(BF), r_b.reshape(1, D))


def _gla_scan_kernel(*refs, zero_init):
    if zero_init:
        qf_ref, qb_ref, gf_ref, gb_ref, of_ref, ob_ref, sn_ref, st_ref = refs
    else:
        qf_ref, qb_ref, gf_ref, gb_ref, s0_ref, of_ref, ob_ref, sn_ref, st_ref = refs
    j = pl.program_id(1)
    tl = qf_ref.shape[1]
    n_chunk = tl // CHUNK

    @pl.when(j == 0)
    def _():
        for d in range(2):
            for h in range(H):
                if zero_init:
                    st_ref[d, h] = jnp.zeros((DV, DK), F32)
                else:
                    st_ref[d, h] = s0_ref[0, 0, d, h].T

    ri = lax.broadcasted_iota(jnp.int32, (CHUNK, CHUNK), 0)
    ci = lax.broadcasted_iota(jnp.int32, (CHUNK, CHUNK), 1)
    masks = (ci <= ri, ci >= ri)

    def decays(d, g_ref, rows):
        tri = jnp.where(masks[d], 1.0, 0.0).astype(BF)
        g = g_ref[0, rows, :]
        g_hi = g.astype(BF)
        g_lo = (g - g_hi.astype(F32)).astype(BF)
        b = _dot(tri, g_hi) + _dot(tri, g_lo)
        b_last = b[CHUNK - 1:CHUNK, :] if d == 0 else b[0:1, :]
        return jnp.exp(b), jnp.exp(-b), jnp.exp(b_last - b), jnp.exp(b_last)

    def chunk_rows(c):
        return (pl.ds(pl.multiple_of(c * CHUNK, CHUNK), CHUNK),
                pl.ds(pl.multiple_of((n_chunk - 1 - c) * CHUNK, CHUNK), CHUNK))

    def body(c, carry):
        rows = chunk_rows(c)
        q_refs, o_refs = (qf_ref, qb_ref), (of_ref, ob_ref)
        fac = (decays(0, gf_ref, rows[0]), decays(1, gb_ref, rows[1]))
        chains = [(d, h) for h in range(H) for d in range(2)]
        qd, ksc, att, v = {}, {}, {}, {}
        for d, h in chains:
            e_pos, e_neg, e_rem, _ = fac[d]
            ks = slice(h * DK, (h + 1) * DK)
            q = q_refs[d][0, rows[d], h * DK:(h + 1) * DK].astype(F32)
            k = q_refs[d][0, rows[d], KD + h * DK:KD + (h + 1) * DK].astype(F32)
            qd[d, h] = (q * e_pos[:, ks]).astype(BF)
            ksc[d, h] = (k * e_rem[:, ks]).astype(BF)
            att[d, h] = _dot_nt(qd[d, h], (k * e_neg[:, ks]).astype(BF))
        for d, h in chains:
            v[d, h] = q_refs[d][0, rows[d], 2 * KD + h * DV:2 * KD + (h + 1) * DV]
            a = jnp.where(masks[d], att[d, h], 0.0).astype(BF)
            o = _dot(a, v[d, h]) + _dot_nt(qd[d, h], st_ref[d, h].astype(BF))
            o_refs[d][0, rows[d], h * DV:(h + 1) * DV] = o.astype(o_refs[d].dtype)
        for d, h in chains:
            ks = slice(h * DK, (h + 1) * DK)
            st_ref[d, h] = st_ref[d, h] * fac[d][3][:, ks] + _dot_tn(v[d, h], ksc[d, h])
        return carry

    lax.fori_loop(0, n_chunk, body, 0)

    @pl.when(j == pl.num_programs(1) - 1)
    def _():
        for d in range(2):
            for h in range(H):
                sn_ref[0, 0, d, h] = st_ref[d, h].T


def _gla_scan(qkv, g, s0):
    B, L, _ = qkv.shape
    tl = min(L, ROW_TILE)
    nl = L // tl
    zero_init = s0 is None
    wq = 2 * KD + D
    in_specs = [pl.BlockSpec((1, tl, wq), lambda b, j: (b, j, 0)),
                pl.BlockSpec((1, tl, wq), lambda b, j: (b, nl - 1 - j, 0)),
                pl.BlockSpec((1, tl, KD), lambda b, j: (b, j, 0)),
                pl.BlockSpec((1, tl, KD), lambda b, j: (b, nl - 1 - j, 1))]
    args = [qkv, qkv, g, g]
    st_spec = pl.BlockSpec((1, 1, 2, H, DK, DV), lambda b, j: (b, 0, 0, 0, 0, 0))
    if not zero_init:
        in_specs.append(st_spec)
        args.append(s0)
    return pl.pallas_call(
        functools.partial(_gla_scan_kernel, zero_init=zero_init),
        grid=(B, nl),
        in_specs=in_specs,
        out_specs=[pl.BlockSpec((1, tl, D), lambda b, j: (b, j, 0)),
                   pl.BlockSpec((1, tl, D), lambda b, j: (b, nl - 1 - j, 0)), st_spec],
        out_shape=[jax.ShapeDtypeStruct((B, L, D), BF), jax.ShapeDtypeStruct((B, L, D), BF),
                   jax.ShapeDtypeStruct((B, 1, 2, H, DK, DV), F32)],
        scratch_shapes=[pltpu.VMEM((2, H, DV, DK), F32)],
        compiler_params=_params(2),
    )(*args)


def _gla_out_kernel(of_ref, ob_ref, r_ref, on_ref, w_ref, x_ref, g1_ref, ng1_ref, ng2_ref, sc_ref, sh_ref, rw_ref,
                    x3_ref, h_ref, route_ref, rt_ref, cnt_ref):
    bb, tl, _ = x_ref.shape
    tm = bb * tl
    o = of_ref[...].astype(F32) + ob_ref[...].astype(F32)
    heads = []
    for h in range(H):
        oh = o[:, :, h * DV:(h + 1) * DV]
        heads.append(oh * lax.rsqrt(jnp.mean(oh * oh, axis=-1, keepdims=True) + RMS_EPS))
    o = jnp.concatenate(heads, -1) * on_ref[...] * r_ref[...].astype(F32)
    m = _dot(o.reshape(tm, D).astype(BF), w_ref[...])
    x3 = x_ref[...] + g1_ref[...] * _rms(m, ng1_ref[0]).reshape(bb, tl, D)
    x3_ref[...] = x3
    h2 = _rms(x3, ng2_ref[...]) * (1.0 + sc_ref[...]) + sh_ref[...]
    h_hi = h2.astype(BF)
    h_ref[...] = h_hi
    h_lo = (h2 - h_hi.astype(F32)).reshape(tm, D).astype(BF)
    both = _dot(h_hi.reshape(tm, D), rw_ref[...])
    logits = both[:, :LANES] + both[:, LANES:] + _dot(h_lo, rw_ref[:, :LANES])
    lane_i = lax.broadcasted_iota(jnp.int32, (tm, LANES), 1)
    lane = lane_i.astype(F32)
    logits = jnp.where(lane_i < N_EXPERTS, logits, -jnp.inf)
    m0 = jnp.max(logits, axis=-1, keepdims=True)
    i0 = jnp.min(jnp.where(logits == m0, lane, float(LANES)), axis=-1, keepdims=True)
    rest = jnp.where(lane == i0, -jnp.inf, logits)
    m1 = jnp.max(rest, axis=-1, keepdims=True)
    i1 = jnp.min(jnp.where(rest == m1, lane, float(LANES)), axis=-1, keepdims=True)
    e = jnp.exp(m1 - m0)
    w0 = 1.0 / (1.0 + e)
    w1 = e * w0
    onehot = jnp.where((lane == i0) | (lane == i1), 1.0, 0.0)
    ri = lax.broadcasted_iota(jnp.int32, (tm, tm), 0)
    ci = lax.broadcasted_iota(jnp.int32, (tm, tm), 1)
    same_tile = (ri // MOE_TILE) == (ci // MOE_TILE)
    before = jnp.where(same_tile, jnp.where(ci < ri, 1.0, 0.0), 0.0).astype(BF)
    prior = _dot(before, onehot.astype(BF))
    rank0 = jnp.sum(jnp.where(lane == i0, prior, 0.0), axis=-1, keepdims=True)
    rank1 = jnp.sum(jnp.where(lane == i1, prior, 0.0), axis=-1, keepdims=True)
    fields = (w0, w1, i0, i1, rank0, rank1)
    route = jnp.zeros((tm, LANES), F32)
    for n, val in enumerate(fields):
        route = jnp.where(lane_i == n, val, route)
    route_ref[...] = route
    route_t = route.T
    for s in range(tm // MOE_TILE):
        rows = slice(s * MOE_TILE, (s + 1) * MOE_TILE)
        cnt_ref[s] = jnp.broadcast_to(jnp.sum(onehot[rows], axis=0, keepdims=True), cnt_ref.shape[1:])
        rt_ref[s] = route_t[0:SUBLANES, rows]


def _gla_out(o_f, o_b, r_gate, onorm_g, w, x, g1, ng1, ng2, sc, sh, router_w):
    B, L, _ = x.shape
    r = _Rows(B, L)
    rw = jnp.pad(router_w, ((0, 0), (0, LANES - N_EXPERTS)))
    rw_hi = rw.astype(BF)
    rw = jnp.concatenate([rw_hi, (rw - rw_hi.astype(F32)).astype(BF)], axis=1)
    tm = r.tm
    n_sub = tm // MOE_TILE
    return pl.pallas_call(
        _gla_out_kernel,
        grid=(r.n,),
        in_specs=[r.act(D), r.act(D), r.act(D), _const((1, 1, D)), _const((D, D)), r.act(D), r.mod(g1),
                  _const((1, 1, D)), _const((1, 1, D)), r.mod(sc), r.mod(sh), _const((D, 2 * LANES))],
        out_specs=[r.act(D), r.act(D), pl.BlockSpec((tm, LANES), lambda i: (i, 0)),
                   pl.BlockSpec((n_sub, SUBLANES, MOE_TILE), lambda i: (i, 0, 0)),
                   pl.BlockSpec((n_sub, SUBLANES, LANES), lambda i: (i, 0, 0))],
        out_shape=[jax.ShapeDtypeStruct((B, L, D), F32), jax.ShapeDtypeStruct((B, L, D), BF),
                   jax.ShapeDtypeStruct((B * L, LANES), F32),
                   jax.ShapeDtypeStruct((r.n * n_sub, SUBLANES, MOE_TILE), F32),
                   jax.ShapeDtypeStruct((r.n * n_sub, SUBLANES, LANES), F32)],
        compiler_params=_params(1),
    )(o_f, o_b, r_gate, jnp.tile(onorm_g, H).reshape(1, 1, D), w.astype(BF), x, g1, ng1.reshape(1, 1, D),
      ng2.reshape(1, 1, D), sc, sh, rw)


SEG = 128
MAIN = 2


def _seg_start(seg_ref, step, e):
    return pl.multiple_of(seg_ref[step * N_EXPERTS + e], PACKED_ROWS)


def _dispatch_kernel(seg_ref, cnt_ref, h_ref, rt_ref, xs_in_hbm, xs_hbm, stage, extra, sem, esem):
    del xs_in_hbm
    i = pl.program_id(0)
    slot = i % 2
    tm = h_ref.shape[0]
    n_piece = tm // SEG
    h = h_ref[...]
    assert n_piece == 2 * MAIN
    ex0, ex1, lr0, lr1 = (jnp.broadcast_to(rt_ref[0, n:n + 1, :], (MAIN * SEG, tm)) for n in range(2, 6))
    rank_row = lax.broadcasted_iota(jnp.int32, (MAIN * SEG, tm), 0).astype(F32)

    def row0(step, e, k):
        return _seg_start(seg_ref, step, e) + k * SEG

    def copy(step, buf, e, k):
        return pltpu.make_async_copy(stage.at[buf, e, pl.ds(k * SEG, SEG)], xs_hbm.at[pl.ds(row0(step, e, k), SEG)],
                                     sem.at[buf, e * MAIN + k])

    def live(step, e, k):
        return k * SEG < cnt_ref[step * N_EXPERTS + e]

    def each_main(step, buf, fn):
        for e in range(N_EXPERTS):
            for k in range(MAIN):
                @pl.when(live(step, e, k))
                def _(e=e, k=k):
                    fn(copy(step, buf, e, k))

    ranks = []
    for e in range(N_EXPERTS):
        ef = float(e)
        ranks.append(jnp.where(ex0 == ef, lr0, jnp.where(ex1 == ef, lr1, -1.0)))
        pick = jnp.where(ranks[e] == rank_row, 1.0, 0.0).astype(BF)
        stage[slot, e] = _dot(pick, h).astype(BF)

    @pl.when(i > 0)
    def _():
        each_main(i - 1, 1 - slot, lambda cp: cp.wait())

    each_main(i, slot, lambda cp: cp.start())
    for e in range(N_EXPERTS):
        @pl.when(live(i, e, MAIN))
        def _(e=e):
            more = jnp.where(ranks[e] - float(MAIN * SEG) == rank_row, 1.0, 0.0).astype(BF)
            extra[...] = _dot(more, h).astype(BF)
            for k in range(MAIN, n_piece):
                @pl.when(live(i, e, k))
                def _(k=k):
                    cp = pltpu.make_async_copy(extra.at[pl.ds((k - MAIN) * SEG, SEG)],
                                               xs_hbm.at[pl.ds(row0(i, e, k), SEG)], esem)
                    cp.start()
                    cp.wait()

    @pl.when(i == pl.num_programs(0) - 1)
    def _():
        each_main(i, slot, lambda cp: cp.wait())


def _dispatch(h2, route_t, seg, cnt, xs, tm=MOE_TILE):
    T = h2.shape[0]
    grid_spec = pltpu.PrefetchScalarGridSpec(
        num_scalar_prefetch=2,
        grid=(T // tm,),
        in_specs=[pl.BlockSpec((tm, D), lambda i, s, c: (i, 0)),
                  pl.BlockSpec((1, SUBLANES, tm), lambda i, s, c: (i, 0, 0)),
                  pl.BlockSpec(memory_space=pl.ANY)],
        out_specs=pl.BlockSpec(memory_space=pl.ANY),
        scratch_shapes=[pltpu.VMEM((2, N_EXPERTS, MAIN * SEG, D), BF), pltpu.VMEM((MAIN * SEG, D), BF),
                        pltpu.SemaphoreType.DMA((2, N_EXPERTS * MAIN)), pltpu.SemaphoreType.DMA(())],
    )
    return pl.pallas_call(
        _dispatch_kernel,
        grid_spec=grid_spec,
        out_shape=jax.ShapeDtypeStruct(xs.shape, xs.dtype),
        input_output_aliases={4: 0},
        compiler_params=_params(1),
    )(seg, cnt, h2, route_t, xs)


def _combine_kernel(seg_ref, cnt_ref, ys_hbm, route_ref, x_ref, g2_ref, ng_ref, o_ref, gbuf, extra, acc_ref, sem,
                    esem):
    i = pl.program_id(0)
    bb, tl, _ = x_ref.shape
    tm = bb * tl
    n_piece = tm // SEG

    def main_copy(step, slot, e, k):
        src = ys_hbm.at[pl.ds(_seg_start(seg_ref, step, e) + k * SEG, SEG)]
        return pltpu.make_async_copy(src, gbuf.at[slot, pl.ds((e * MAIN + k) * SEG, SEG)], sem.at[slot, e * MAIN + k])

    def fetch(step, slot):
        for e in range(N_EXPERTS):
            for k in range(MAIN):
                main_copy(step, slot, e, k).start()

    @pl.when(i == 0)
    def _():
        fetch(0, 0)

    @pl.when(i + 1 < pl.num_programs(0))
    def _():
        fetch(i + 1, (i + 1) % 2)

    slot = i % 2
    route = route_ref[...]

    def spread_cols(e, r0, width):
        ef = float(e)
        gate = jnp.where(route[:, 2:3] == ef, route[:, 0:1], jnp.where(route[:, 3:4] == ef, route[:, 1:2], 0.0))
        rank = jnp.where(route[:, 2:3] == ef, route[:, 4:5], jnp.where(route[:, 3:4] == ef, route[:, 5:6], -1.0))
        col = lax.broadcasted_iota(jnp.int32, (tm, width), 1).astype(F32) + float(r0)
        return (jnp.where(rank == col, 1.0, 0.0) * gate).astype(BF)

    spread = jnp.concatenate([spread_cols(e, 0, MAIN * SEG) for e in range(N_EXPERTS)], axis=1)
    for e in range(N_EXPERTS):
        for k in range(MAIN):
            main_copy(i, slot, e, k).wait()
    acc_ref[...] = _dot(spread, gbuf[slot].astype(BF))
    for e in range(N_EXPERTS):
        for k in range(MAIN, n_piece):
            @pl.when(k * SEG < cnt_ref[i * N_EXPERTS + e])
            def _(e=e, k=k):
                src = ys_hbm.at[pl.ds(_seg_start(seg_ref, i, e) + k * SEG, SEG)]
                cp = pltpu.make_async_copy(src, extra, esem)
                cp.start()
                more = spread_cols(e, k * SEG, SEG)
                cp.wait()
                acc_ref[...] += _dot(more, extra[...].astype(BF))

    o_ref[...] = x_ref[...] + g2_ref[...] * _rms(acc_ref[...], ng_ref[0]).reshape(bb, tl, D)


def _combine(ys, route, seg, cnt, x, g2, ng):
    B, L, _ = x.shape
    r = _Rows(B, L, MOE_TILE)
    tm = r.tm
    nl = r.nl
    act = pl.BlockSpec((r.bb, r.tl, D), lambda i, s, c: (i // nl, i % nl, 0))
    g2_spec = (pl.BlockSpec((1, 1, D), lambda i, s, c: (0, 0, 0)) if g2.shape[0] == 1 else
               pl.BlockSpec((r.bb, 1, D), lambda i, s, c: (i // nl, 0, 0)))
    grid_spec = pltpu.PrefetchScalarGridSpec(
        num_scalar_prefetch=2,
        grid=(r.n,),
        in_specs=[pl.BlockSpec(memory_space=pl.ANY), pl.BlockSpec((tm, LANES), lambda i, s, c: (i, 0)), act, g2_spec,
                  pl.BlockSpec((1, 1, D), lambda i, s, c: (0, 0, 0))],
        out_specs=act,
        scratch_shapes=[pltpu.VMEM((2, N_EXPERTS * MAIN * SEG, D), BF), pltpu.VMEM((SEG, D), BF),
                        pltpu.VMEM((tm, D), F32), pltpu.SemaphoreType.DMA((2, N_EXPERTS * MAIN)),
                        pltpu.SemaphoreType.DMA(())],
    )
    return pl.pallas_call(
        _combine_kernel,
        grid_spec=grid_spec,
        out_shape=jax.ShapeDtypeStruct((B, L, D), F32),
        compiler_params=_params(1),
    )(seg, cnt, ys, route, x, g2, ng.reshape(1, 1, D))


def _moe(passes, ng, w_gu, w_d, tmoe=1024):
    cntb = jnp.concatenate([ps['blk_cnt'][:, 0, :N_EXPERTS] for ps in passes], axis=0).astype(jnp.int32)
    n_tok_blocks = cntb.shape[0]
    n_tok = sum(ps['h2'].shape[0] * ps['h2'].shape[1] for ps in passes)
    held = (cntb + PACKED_ROWS - 1) // PACKED_ROWS * PACKED_ROWS
    before = jnp.cumsum(held, axis=0) - held
    cnt = jnp.sum(held, axis=0)
    p_cnt = (cnt + MAIN * SEG + tmoe - 1) // tmoe * tmoe
    p_end = jnp.cumsum(p_cnt)
    p_start = p_end - p_cnt
    seg = p_start[None, :] + before
    n_rows = 2 * n_tok + N_EXPERTS * (n_tok_blocks * (PACKED_ROWS - 1) + tmoe + MAIN * SEG)
    n_rows = (n_rows + tmoe - 1) // tmoe * tmoe
    n_blocks = n_rows // tmoe
    starts = jnp.arange(n_blocks, dtype=jnp.int32) * tmoe
    block_e = jnp.minimum(jnp.sum(starts[:, None] >= p_end[None, :], axis=-1), N_EXPERTS - 1).astype(jnp.int32)
    n_valid = (p_end[-1:] // tmoe).astype(jnp.int32)
    xs = jnp.zeros((n_rows, D), BF)
    first = 0
    for ps in passes:
        B, L, _ = ps['h2'].shape
        nb = ps['blk_cnt'].shape[0]
        ps['seg'] = seg[first:first + nb].reshape(-1)
        ps['cnt'] = cntb[first:first + nb].reshape(-1)
        first += nb
        xs = _dispatch(ps['h2'].reshape(B * L, D), ps['route_t'], ps['seg'], ps['cnt'], xs)
    ys = _swiglu(xs, w_gu, w_d, block_e, n_valid, tmoe, D_FF_EXPERT // 2, BF)
    return [_combine(ys, ps['route'], ps['seg'], ps['cnt'], ps['x3'], ps['g2'], ng) for ps in passes]


def _trunk(x, mods, n_row, s0, p):
    B, L, _ = x.shape
    T = B * L
    ng = p['norm_g']
    sh1, sc1, g1, sh2, sc2, g2 = mods[0]
    v, x0 = _hy_in(x, ng[0, 0], sc1, sh1, p['hy_in_w'][0], p['hy_in_b'][0], p['hy_sc_w'][0], p['hy_sc_b'][0], n_row)
    kern = _hyena_kernel_taps(L, p['hy_f_w1'][0], p['hy_f_b1'][0], p['hy_f_w2'][0], p['hy_f_b2'][0],
                              p['hy_f_freq'][0], p['hy_f_w3'][0])
    if L <= 512:
        vx = _fftconv_short(v, x0, kern, p['hy_skip'][0])
    else:
        vx = _fftconv_long(v, x0, kern, p['hy_skip'][0])
    x1, h2 = _hy_out(vx, p['hy_out_w'][0], p['hy_out_b'][0], x, g1, ng[0, 1], ng[0, 2], sc2, sh2)
    tm = 1024
    ones = jnp.zeros((T // tm,), jnp.int32)
    f = _swiglu(h2.reshape(T, D), p['ffn_wgu'], p['ffn_wd'], ones, jnp.full((1,), T // tm, jnp.int32), tm, D_FF, F32)
    g2_0, ng3_0 = g2, ng[0, 3]
    sh1, sc1, g1, sh2, sc2, g2 = mods[1]
    x2, qkv, gate, r_gate = _gla_in(x1, f, g2_0, ng3_0, ng[1, 0], sc1, sh1, p['gla_qkv_w'][0], p['gla_gk_w1'][0],
                                    p['gla_gk_w2'][0], p['gla_gk_b'][0], p['gla_r_w'][0], p['gla_r_b'][0])
    o_f, o_b, s_new = _gla_scan(qkv, gate, s0)
    x3, h2, route, route_t, blk_cnt = _gla_out(o_f, o_b, r_gate, p['gla_onorm_g'][0], p['gla_out_w'][0], x2, g1,
                                               ng[1, 1], ng[1, 2], sc2, sh2, p['moe_router'][0])
    return dict(x3=x3, h2=h2, route=route, route_t=route_t, blk_cnt=blk_cnt, g2=g2), s_new


def kernel(x_prompt, x_sample, state_gla, c, c_ctx, ada_w, ada_b, norm_g, hy_in_w, hy_in_b, hy_sc_w, hy_sc_b, hy_f_w1, hy_f_b1, hy_f_w2, hy_f_b2, hy_f_freq, hy_f_w3, hy_skip, hy_out_w, hy_out_b, gla_qkv_w, gla_gk_w1, gla_gk_w2, gla_gk_b, gla_r_w, gla_r_b, gla_onorm_g, gla_out_w, ffn_wgu, ffn_wd, moe_router, moe_wgu, moe_wd):
    p = dict(norm_g=norm_g, hy_in_w=hy_in_w, hy_in_b=hy_in_b, hy_sc_w=hy_sc_w, hy_sc_b=hy_sc_b, hy_f_w1=hy_f_w1,
             hy_f_b1=hy_f_b1, hy_f_w2=hy_f_w2, hy_f_b2=hy_f_b2, hy_f_freq=hy_f_freq, hy_f_w3=hy_f_w3,
             hy_skip=hy_skip, hy_out_w=hy_out_w, hy_out_b=hy_out_b, gla_qkv_w=gla_qkv_w, gla_gk_w1=gla_gk_w1,
             gla_gk_w2=gla_gk_w2, gla_gk_b=gla_gk_b, gla_r_w=gla_r_w, gla_r_b=gla_r_b, gla_onorm_g=gla_onorm_g,
             gla_out_w=gla_out_w, ffn_wgu=ffn_wgu.astype(BF), ffn_wd=ffn_wd.astype(BF),
             moe_router=moe_router, moe_wgu=moe_wgu[0].astype(BF), moe_wd=moe_wd[0].astype(BF))
    n_dec = c.shape[0]
    cond = jnp.concatenate([c_ctx[None, :], c, jnp.zeros((16 - 1 - n_dec, D), F32)], axis=0)
    mod = _ada(cond, ada_w, ada_b)
    mods_ctx = [[m[:, None, :] for m in jnp.split(mod[l, 0:1], 6, axis=-1)] for l in range(DEPTH)]
    mods_dec = [[m[:, None, :] for m in jnp.split(mod[l, 1:1 + n_dec], 6, axis=-1)] for l in range(DEPTH)]
    ctx, state_new = _trunk(x_prompt, mods_ctx, x_prompt.shape[1], None, p)
    grid_w = 64
    dec, _ = _trunk(x_sample, mods_dec, grid_w, state_gla, p)
    y_prompt, y_sample = _moe([ctx, dec], norm_g[1, 3], p['moe_wgu'], p['moe_wd'])
    return y_prompt, y_sample, state_new
```

```python
import functools
import math

import jax
import jax.numpy as jnp
from jax import lax
from jax.experimental import pallas as pl
from jax.experimental.pallas import tpu as pltpu

F32 = jnp.float32
BF = jnp.bfloat16

D = 1024
RMS_EPS = 1e-6
DEPTH = 2
HY_SHORT = 3
HY_EMB = 33
HY_BANDS = (HY_EMB - 1) // 2
HY_FFN = 64
HY_MAX_DECAY = math.log(1e-2) / 0.3
HY_MIN_DECAY = math.log(1e-2) / 1.5
H = 4
DK = 128
DV = 256
KD = H * DK
GLA_RANK = 16
GLA_GATE_NORM = 16.0
CHUNK = 64
D_FF = 11 * D // 4
N_EXPERTS = 8
D_FF_EXPERT = 7 * D // 2

LANES = 128
SUBLANES = 8
VMEM_LIMIT_BYTES = 56 * 1024 * 1024
ROW_TILE = 512
MOE_TILE = 512
PACKED_ROWS = 16
FFT_S2 = 128


def _params(n_axes):
    return pltpu.CompilerParams(dimension_semantics=("arbitrary",) * n_axes,
                                vmem_limit_bytes=VMEM_LIMIT_BYTES)


def _dot(a, b):
    return jnp.dot(a, b, preferred_element_type=F32)


def _dot_nt(a, b):
    return lax.dot_general(a, b, (((1,), (1,)), ((), ())), preferred_element_type=F32)


def _dot_tn(a, b):
    return lax.dot_general(a, b, (((0,), (0,)), ((), ())), preferred_element_type=F32)


def _rms(x, g):
    return x * lax.rsqrt(jnp.mean(x * x, axis=-1, keepdims=True) + RMS_EPS) * g


def _silu(x):
    return x * (1.0 / (1.0 + jnp.exp(-x)))


class _Rows:
    def __init__(self, B, L, tm=ROW_TILE):
        self.B, self.L = B, L
        if L >= tm:
            self.bb, self.tl = 1, tm
        else:
            self.bb, self.tl = tm // L, L
        assert L % self.tl == 0 and B % self.bb == 0
        self.nl = L // self.tl
        self.n = (B // self.bb) * self.nl
        self.tm = self.bb * self.tl

    def act(self, width):
        nl = self.nl
        return pl.BlockSpec((self.bb, self.tl, width), lambda i: (i // nl, i % nl, 0))

    def mod(self, m):
        nl = self.nl
        if m.shape[0] == 1:
            return pl.BlockSpec((1, 1, D), lambda i: (0, 0, 0))
        return pl.BlockSpec((self.bb, 1, D), lambda i: (i // nl, 0, 0))


def _const(shape):
    nd = len(shape)
    return pl.BlockSpec(shape, lambda *_: (0,) * nd)


def _ada_kernel(c_ref, w_ref, b_ref, o_ref):
    cs = _silu(c_ref[...])
    o_ref[0] = _dot(cs.astype(BF), w_ref[0].astype(BF)) + b_ref[0]


def _ada(cond, ada_w, ada_b):
    R = cond.shape[0]
    tn = 1536
    return pl.pallas_call(
        _ada_kernel,
        grid=(DEPTH, 6 * D // tn),
        in_specs=[pl.BlockSpec((R, D), lambda l, n: (0, 0)),
                  pl.BlockSpec((1, D, tn), lambda l, n: (l, 0, n)),
                  pl.BlockSpec((1, 1, tn), lambda l, n: (l, 0, n))],
        out_specs=pl.BlockSpec((1, R, tn), lambda l, n: (l, 0, n)),
        out_shape=jax.ShapeDtypeStruct((DEPTH, R, 6 * D), F32),
        compiler_params=_params(2),
    )(cond, ada_w, ada_b.reshape(DEPTH, 1, 6 * D))


def _hy_in_kernel(x_ref, ng_ref, sc_ref, sh_ref, w_ref, b_ref, cw_ref, cb_ref, v_ref, x0_ref, *, n_row):
    x = x_ref[...]
    bb, tl, _ = x.shape
    tm = bb * tl
    h = _rms(x, ng_ref[...]) * (1.0 + sc_ref[...]) + sh_ref[...]
    hb = h.reshape(tm, D).astype(BF)
    pos = lax.broadcasted_iota(jnp.int32, (tm, D), 0) & (n_row - 1)
    first = pos == 0
    last = pos == n_row - 1
    parts = []
    for j in range(3):
        cols = slice(j * D, (j + 1) * D)
        u = _dot(hb, w_ref[:, cols]) + b_ref[:, cols]
        up = jnp.where(first, 0.0, pltpu.roll(u, 1, 0))
        dn = jnp.where(last, 0.0, pltpu.roll(u, tm - 1, 0))
        parts.append(cb_ref[:, cols] + up * cw_ref[0:1, cols] + u * cw_ref[1:2, cols] + dn * cw_ref[2:3, cols])
    x0, x1, v = parts
    v_ref[...] = (v * x1).astype(BF).reshape(bb, tl, D)
    x0_ref[...] = x0.astype(BF).reshape(bb, tl, D)


def _hy_in(x, ng, sc, sh, w, b, cw, cb, n_row):
    B, L, _ = x.shape
    assert n_row & (n_row - 1) == 0
    r = _Rows(B, L)
    assert r.tl % n_row == 0
    return pl.pallas_call(
        functools.partial(_hy_in_kernel, n_row=n_row),
        grid=(r.n,),
        in_specs=[r.act(D), _const((1, 1, D)), r.mod(sc), r.mod(sh),
                  _const((D, 3 * D)), _const((1, 3 * D)), _const((HY_SHORT, 3 * D)), _const((1, 3 * D))],
        out_specs=[r.act(D), r.act(D)],
        out_shape=[jax.ShapeDtypeStruct((B, L, D), BF)] * 2,
        compiler_params=_params(1),
    )(x, ng.reshape(1, 1, D), sc, sh, w.astype(BF), b.reshape(1, 3 * D), cw, cb.reshape(1, 3 * D))


def _filter_kernel(z_ref, w1_ref, b1_ref, w2_ref, b2_ref, fr_ref, w3_ref, dl_ref, o_ref, *, L):
    tr = z_ref.shape[0]
    hp = lax.Precision.HIGHEST
    z = z_ref[...]
    h = jnp.sin(fr_ref[0:1, :] * (jnp.dot(z, w1_ref[...], precision=hp, preferred_element_type=F32) + b1_ref[...]))
    h = jnp.sin(fr_ref[1:2, :] * (jnp.dot(h, w2_ref[...], precision=hp, preferred_element_type=F32) + b2_ref[...]))
    hw = _dot(h.astype(BF), w3_ref[...])
    n = pl.program_id(0) * tr + lax.broadcasted_iota(jnp.int32, (tr, D), 0)
    taps = jnp.where(n < L, hw[:, :D], hw[:, D:]) * jnp.exp(-z[:, 0:1] * dl_ref[...])
    o_ref[...] = jnp.where(n == L, 0.0, taps)


def _hyena_kernel_taps(L, w1, b1, w2, b2, freq, w3):
    n = jnp.arange(2 * L, dtype=jnp.int32)
    pos = jnp.where(n < L, n, 2 * L - n) % L
    t = (pos.astype(F32) / (L - 1))[:, None]
    w = (2.0 * math.pi * pos.astype(F32) / L)[:, None]
    f = jnp.linspace(1e-4, HY_BANDS - 1, HY_BANDS, dtype=F32)[None, :]
    z = jnp.concatenate([t, jnp.cos(f * w), -jnp.sin(f * w)], axis=-1)
    z = jnp.pad(z, ((0, 0), (0, LANES - HY_EMB)))
    pad = LANES - HY_FFN
    w1p = jnp.pad(w1, ((0, LANES - HY_EMB), (0, pad)))
    w2p = jnp.pad(w2, ((0, pad), (0, pad)))
    w3p = jnp.pad(w3, ((0, pad), (0, 0))).astype(BF)
    b1p = jnp.pad(b1, (0, pad)).reshape(1, LANES)
    b2p = jnp.pad(b2, (0, pad)).reshape(1, LANES)
    frp = jnp.pad(freq, ((0, 0), (0, pad)))
    deltas = jnp.abs(jnp.linspace(HY_MIN_DECAY, HY_MAX_DECAY, D, dtype=F32)).reshape(1, D)
    tr = 512
    return pl.pallas_call(
        functools.partial(_filter_kernel, L=L),
        grid=(2 * L // tr,),
        in_specs=[pl.BlockSpec((tr, LANES), lambda i: (i, 0)), _const((LANES, LANES)), _const((1, LANES)),
                  _const((LANES, LANES)), _const((1, LANES)), _const((2, LANES)), _const((LANES, 2 * D)),
                  _const((1, D))],
        out_specs=pl.BlockSpec((tr, D), lambda i: (i, 0)),
        out_shape=jax.ShapeDtypeStruct((2 * L, D), F32),
        compiler_params=_params(1),
    )(z, w1p, b1p, w2p, b2p, frp, w3p, deltas)


def _cis(rows, cols, n, sign, scale=1.0):
    ph = (rows[:, None] * cols[None, :]) % n
    ang = ph.astype(F32) * (2.0 * math.pi / n)
    return jnp.cos(ang) * scale, jnp.sin(ang) * (sign * scale)


def _cplx_block(cr, ci):
    return jnp.concatenate([jnp.concatenate([cr, -ci], 1), jnp.concatenate([ci, cr], 1)], 0)


def _fft_short_kernel(v_ref, x0_ref, kern_ref, mk_ref, mf_ref, mi_ref, sk_ref, o_ref, ks_ref, *, L):
    n2 = 2 * L

    @pl.when(pl.program_id(0) == 0)
    def _():
        ks_ref[...] = _dot(mk_ref[...], kern_ref[...].astype(BF))

    z = jnp.concatenate([v_ref[0], v_ref[1]], 0)
    u = _dot(mf_ref[...], z)
    ur, ui = u[:n2], u[n2:]
    kr, ki = ks_ref[:n2, :], ks_ref[n2:, :]
    y = jnp.concatenate([ur * kr - ui * ki, ur * ki + ui * kr], 0).astype(BF)
    t = _dot(mi_ref[...], y)
    for j in range(2):
        conv = t[j * L:(j + 1) * L]
        o_ref[j] = ((conv + v_ref[j].astype(F32) * sk_ref[...]) * x0_ref[j].astype(F32)).astype(BF)


def _fftconv_short(v, x0, kern, skip):
    B, L, _ = v.shape
    n2 = 2 * L
    k = jnp.arange(n2, dtype=jnp.int32)
    s = jnp.arange(L, dtype=jnp.int32)
    fr, fi = _cis(k, s, n2, -1.0)
    mf = _cplx_block(fr, fi).astype(BF)
    kr, ki = _cis(k, k, n2, -1.0)
    mk = jnp.concatenate([kr, ki], 0).astype(BF)
    ir, ii = _cis(s, k, n2, 1.0, 1.0 / n2)
    mi = _cplx_block(ir, ii).astype(BF)
    pair = lambda p: (p, 0, 0)
    return pl.pallas_call(
        functools.partial(_fft_short_kernel, L=L),
        grid=(B // 2,),
        in_specs=[pl.BlockSpec((2, L, D), pair), pl.BlockSpec((2, L, D), pair), _const((n2, D)),
                  _const((2 * n2, n2)), _const((2 * n2, 2 * L)), _const((2 * L, 2 * n2)), _const((1, D))],
        out_specs=pl.BlockSpec((2, L, D), pair),
        out_shape=jax.ShapeDtypeStruct((B, L, D), BF),
        scratch_shapes=[pltpu.VMEM((2 * n2, D), F32)],
        compiler_params=_params(1),
    )(v, x0, kern, mk, mf, mi, skip.reshape(1, D))


def _lmul_kernel(m_ref, x_ref, o_ref):
    o_ref[0] = _dot(m_ref[...], x_ref[0].astype(BF)).astype(o_ref.dtype)


def _lmul(m, x, cb=8192):
    G, K, NC = x.shape
    R = m.shape[0]
    return pl.pallas_call(
        _lmul_kernel,
        grid=(G, NC // cb),
        in_specs=[pl.BlockSpec((R, K), lambda g, c: (0, 0)), pl.BlockSpec((1, K, cb), lambda g, c: (g, 0, c))],
        out_specs=pl.BlockSpec((1, R, cb), lambda g, c: (g, 0, c)),
        out_shape=jax.ShapeDtypeStruct((G, R, NC), BF),
        compiler_params=_params(2),
    )(m, x)


def _fft_mid_kernel(a_ref, af_ref, mf_ref, mi_ref, o_ref):
    s2 = FFT_S2
    n_pair = a_ref.shape[0]
    mf = mf_ref[0]
    ks = _dot(mf, jnp.concatenate([af_ref[0, 0], af_ref[1, 0]], 0))
    kr, ki = ks[:s2], ks[s2:]
    us = [_dot(mf, jnp.concatenate([a_ref[p, 0, 0], a_ref[p, 1, 0]], 0)) for p in range(n_pair)]
    ys = [jnp.concatenate([u[:s2] * kr - u[s2:] * ki, u[:s2] * ki + u[s2:] * kr], 0).astype(BF) for u in us]
    zs = [_dot(mi_ref[0], y).astype(BF) for y in ys]
    for p, z in enumerate(zs):
        o_ref[p, 0, 0] = z[:s2]
        o_ref[p, 1, 0] = z[s2:]


def _lmul_out_kernel(m_ref, z_ref, v_ref, x0_ref, sk_ref, o_ref):
    conv = _dot(m_ref[...], z_ref[0])
    o_ref[0] = ((conv + v_ref[0].astype(F32) * sk_ref[...]) * x0_ref[0].astype(F32)).astype(BF)


def _fftconv_long(v, x0, kern, skip, cb=8192):
    B, L, _ = v.shape
    n2 = 2 * L
    s2 = FFT_S2
    s1 = n2 // s2
    s1h = s1 // 2
    G = B // 2
    nc = s2 * D
    i1 = jnp.arange(s1, dtype=jnp.int32)
    i1h = jnp.arange(s1h, dtype=jnp.int32)
    i2 = jnp.arange(s2, dtype=jnp.int32)
    cr, ci = _cis(i1, i1h, s1, -1.0)
    m1 = _cplx_block(cr, ci).astype(BF)
    cr, ci = _cis(i1, i1, s1, -1.0)
    m1f = jnp.concatenate([cr, ci], 0).astype(BF)
    kk = (i1[:, None] + s1 * i2[None, :]).reshape(-1)
    gr, gi = _cis(kk, i2, n2, -1.0)
    mf = jax.vmap(_cplx_block)(gr.reshape(s1, s2, s2), gi.reshape(s1, s2, s2)).astype(BF)
    hr, hi = _cis(i2, kk, n2, 1.0, 1.0 / n2)
    hr = hr.reshape(s2, s1, s2).transpose(1, 0, 2)
    hi = hi.reshape(s2, s1, s2).transpose(1, 0, 2)
    mi = jax.vmap(_cplx_block)(hr, hi).astype(BF)
    er, ei = _cis(i1h, i1, s1, 1.0)
    m3 = _cplx_block(er, ei).astype(BF)

    a = _lmul(m1, v.reshape(G, 2 * s1h, nc), cb)
    af = _lmul(m1f, kern.reshape(1, s1, nc), cb)
    zz = pl.pallas_call(
        _fft_mid_kernel,
        grid=(s1,),
        in_specs=[pl.BlockSpec((G, 2, 1, s2, D), lambda k: (0, 0, k, 0, 0)),
                  pl.BlockSpec((2, 1, s2, D), lambda k: (0, k, 0, 0)),
                  pl.BlockSpec((1, 2 * s2, 2 * s2), lambda k: (k, 0, 0)),
                  pl.BlockSpec((1, 2 * s2, 2 * s2), lambda k: (k, 0, 0))],
        out_specs=pl.BlockSpec((G, 2, 1, s2, D), lambda k: (0, 0, k, 0, 0)),
        out_shape=jax.ShapeDtypeStruct((G, 2, s1, s2, D), BF),
        compiler_params=_params(1),
    )(a.reshape(G, 2, s1, s2, D), af.reshape(2, s1, s2, D), mf, mi)
    blk = lambda g, c: (g, 0, c)
    out = pl.pallas_call(
        _lmul_out_kernel,
        grid=(G, nc // cb),
        in_specs=[pl.BlockSpec((2 * s1h, 2 * s1), lambda g, c: (0, 0)), pl.BlockSpec((1, 2 * s1, cb), blk),
                  pl.BlockSpec((1, 2 * s1h, cb), blk), pl.BlockSpec((1, 2 * s1h, cb), blk),
                  pl.BlockSpec((1, cb), lambda g, c: (0, 0))],
        out_specs=pl.BlockSpec((1, 2 * s1h, cb), blk),
        out_shape=jax.ShapeDtypeStruct((G, 2 * s1h, nc), BF),
        compiler_params=_params(2),
    )(m3, zz.reshape(G, 2 * s1, nc), v.reshape(G, 2 * s1h, nc), x0.reshape(G, 2 * s1h, nc),
      jnp.tile(skip.reshape(1, D), (1, cb // D)))
    return out.reshape(B, L, D)


def _hy_out_kernel(vx_ref, w_ref, b_ref, x_ref, g1_ref, ng1_ref, ng2_ref, sc_ref, sh_ref, x1_ref, h_ref):
    bb, tl, _ = x_ref.shape
    m = _dot(vx_ref[...].reshape(bb * tl, D), w_ref[...]) + b_ref[...]
    x1 = x_ref[...] + g1_ref[...] * _rms(m, ng1_ref[0]).reshape(bb, tl, D)
    x1_ref[...] = x1
    h_ref[...] = (_rms(x1, ng2_ref[...]) * (1.0 + sc_ref[...]) + sh_ref[...]).astype(h_ref.dtype)


def _hy_out(vx, w, b, x, g1, ng1, ng2, sc, sh):
    B, L, _ = x.shape
    r = _Rows(B, L)
    return pl.pallas_call(
        _hy_out_kernel,
        grid=(r.n,),
        in_specs=[r.act(D), _const((D, D)), _const((1, D)), r.act(D), r.mod(g1), _const((1, 1, D)),
                  _const((1, 1, D)), r.mod(sc), r.mod(sh)],
        out_specs=[r.act(D), r.act(D)],
        out_shape=[jax.ShapeDtypeStruct((B, L, D), F32), jax.ShapeDtypeStruct((B, L, D), BF)],
        compiler_params=_params(1),
    )(vx, w.astype(BF), b.reshape(1, D), x, g1, ng1.reshape(1, 1, D), ng2.reshape(1, 1, D), sc, sh)


def _swiglu_kernel(be_ref, nv_ref, x_ref, wg_ref, wu_ref, wd_ref, o_ref, acc_ref):
    del be_ref
    f = pl.program_id(1)

    @pl.when(f == 0)
    def _():
        acc_ref[...] = jnp.zeros_like(acc_ref)

    @pl.when(pl.program_id(0) < nv_ref[0])
    def _():
        x = x_ref[...].astype(BF)
        tf = wg_ref.shape[2]
        sub = 256 if tf % 256 == 0 else tf
        part = None
        for c in range(tf // sub):
            cols = slice(c * sub, (c + 1) * sub)
            g = _dot(x, wg_ref[0, :, cols])
            u = _dot(x, wu_ref[0, :, cols])
            y = _dot((_silu(g) * u).astype(BF), wd_ref[0, cols, :])
            part = y if part is None else part + y
        acc_ref[...] += part

    @pl.when(f == pl.num_programs(1) - 1)
    def _():
        o_ref[...] = acc_ref[...].astype(o_ref.dtype)


def _swiglu(x, w_gu, w_d, block_e, n_valid, tm, tf, out_dtype):
    rows = x.shape[0]
    F = w_d.shape[1]
    nf = F // tf
    assert rows % tm == 0 and F % tf == 0
    grid_spec = pltpu.PrefetchScalarGridSpec(
        num_scalar_prefetch=2,
        grid=(rows // tm, nf),
        in_specs=[pl.BlockSpec((tm, D), lambda i, f, be, nv: (i, 0)),
                  pl.BlockSpec((1, D, tf), lambda i, f, be, nv: (be[i], 0, f)),
                  pl.BlockSpec((1, D, tf), lambda i, f, be, nv: (be[i], 0, nf + f)),
                  pl.BlockSpec((1, tf, D), lambda i, f, be, nv: (be[i], f, 0))],
        out_specs=pl.BlockSpec((tm, D), lambda i, f, be, nv: (i, 0)),
        scratch_shapes=[pltpu.VMEM((tm, D), F32)],
    )
    return pl.pallas_call(
        _swiglu_kernel,
        grid_spec=grid_spec,
        out_shape=jax.ShapeDtypeStruct((rows, D), out_dtype),
        compiler_params=_params(2),
    )(block_e, n_valid, x, w_gu, w_gu, w_d)


def _gla_in_kernel(x1_ref, f_ref, g2_ref, ng3_ref, ng0_ref, sc_ref, sh_ref, wqkv_ref, wg1_ref, wg2_ref, bg_ref,
                   wr_ref, br_ref, x2_ref, qkv_ref, g_ref, r_ref):
    bb, tl, _ = x1_ref.shape
    tm = bb * tl
    x2 = x1_ref[...] + g2_ref[...] * _rms(f_ref[...], ng3_ref[...])
    x2_ref[...] = x2
    h = (_rms(x2, ng0_ref[...]) * (1.0 + sc_ref[...]) + sh_ref[...]).reshape(tm, D).astype(BF)
    for c in range(4):
        cols = slice(c * KD, (c + 1) * KD)
        part = _dot(h, wqkv_ref[:, cols])
        if c == 0:
            part = part * (DK ** -0.5)
        qkv_ref[:, :, cols] = part.astype(BF).reshape(bb, tl, KD)
    low = _dot(h, wg1_ref[...]).astype(BF)
    a = _dot(low, wg2_ref[...]) + bg_ref[...]
    log_sig = jnp.minimum(a, 0.0) - jnp.log(1.0 + jnp.exp(-jnp.abs(a)))
    g_ref[...] = (log_sig / GLA_GATE_NORM).reshape(bb, tl, 2 * KD)
    r_ref[...] = _silu(_dot(h, wr_ref[...]) + br_ref[...]).astype(BF).reshape(bb, tl, D)


def _gla_in(x1, f, g2, ng3, ng0, sc, sh, qkv_w, gk_w1, gk_w2, gk_b, r_w, r_b):
    B, L, _ = x1.shape
    r = _Rows(B, L)
    w1 = jnp.zeros((D, LANES), F32).at[:, :GLA_RANK].set(gk_w1[0]).at[:, GLA_RANK:2 * GLA_RANK].set(gk_w1[1])
    w2 = jnp.zeros((LANES, 2 * KD), F32).at[:GLA_RANK, :KD].set(gk_w2[0]).at[GLA_RANK:2 * GLA_RANK, KD:].set(gk_w2[1])
    return pl.pallas_call(
        _gla_in_kernel,
        grid=(r.n,),
        in_specs=[r.act(D), r.act(D), r.mod(g2), _const((1, 1, D)), _const((1, 1, D)), r.mod(sc), r.mod(sh),
                  _const((D, 2 * KD + D)), _const((D, LANES)), _const((LANES, 2 * KD)), _const((1, 2 * KD)),
                  _const((D, D)), _const((1, D))],
        out_specs=[r.act(D), r.act(2 * KD + D), r.act(2 * KD), r.act(D)],
        out_shape=[jax.ShapeDtypeStruct((B, L, D), F32), jax.ShapeDtypeStruct((B, L, 2 * KD + D), BF),
                   jax.ShapeDtypeStruct((B, L, 2 * KD), F32), jax.ShapeDtypeStruct((B, L, D), BF)],
        compiler_params=_params(1),
    )(x1, f.reshape(B, L, D), g2, ng3.reshape(1, 1, D), ng0.reshape(1, 1, D), sc, sh, qkv_w.astype(BF),
      w1.astype(BF), w2.astype(BF), gk_b.reshape(1, 2 * KD), r_w.astype(BF), r_b.reshape(1, D))


def _gla_scan_kernel(*refs, zero_init):
    if zero_init:
        qf_ref, qb_ref, gf_ref, gb_ref, of_ref, ob_ref, sn_ref, st_ref = refs
    else:
        qf_ref, qb_ref, gf_ref, gb_ref, s0_ref, of_ref, ob_ref, sn_ref, st_ref = refs
    j = pl.program_id(1)
    tl = qf_ref.shape[1]
    n_chunk = tl // CHUNK

    @pl.when(j == 0)
    def _():
        for d in range(2):
            for h in range(H):
                if zero_init:
                    st_ref[d, h] = jnp.zeros((DV, DK), F32)
                else:
                    st_ref[d, h] = s0_ref[0, 0, d, h].T

    ri = lax.broadcasted_iota(jnp.int32, (CHUNK, CHUNK), 0)
    ci = lax.broadcasted_iota(jnp.int32, (CHUNK, CHUNK), 1)
    masks = (ci <= ri, ci >= ri)

    def decays(d, g_ref, rows):
        tri = jnp.where(masks[d], 1.0, 0.0).astype(BF)
        g = g_ref[0, rows, :]
        g_hi = g.astype(BF)
        g_lo = (g - g_hi.astype(F32)).astype(BF)
        b = _dot(tri, g_hi) + _dot(tri, g_lo)
        b_last = b[CHUNK - 1:CHUNK, :] if d == 0 else b[0:1, :]
        return jnp.exp(b), jnp.exp(-b), jnp.exp(b_last - b), jnp.exp(b_last)

    def chunk_rows(c):
        return (pl.ds(pl.multiple_of(c * CHUNK, CHUNK), CHUNK),
                pl.ds(pl.multiple_of((n_chunk - 1 - c) * CHUNK, CHUNK), CHUNK))

    def body(c, carry):
        rows = chunk_rows(c)
        q_refs, o_refs = (qf_ref, qb_ref), (of_ref, ob_ref)
        fac = (decays(0, gf_ref, rows[0]), decays(1, gb_ref, rows[1]))
        chains = [(d, h) for h in range(H) for d in range(2)]
        qd, ksc, att, v = {}, {}, {}, {}
        for d, h in chains:
            e_pos, e_neg, e_rem, _ = fac[d]
            ks = slice(h * DK, (h + 1) * DK)
            q = q_refs[d][0, rows[d], h * DK:(h + 1) * DK].astype(F32)
            k = q_refs[d][0, rows[d], KD + h * DK:KD + (h + 1) * DK].astype(F32)
            qd[d, h] = (q * e_pos[:, ks]).astype(BF)
            ksc[d, h] = (k * e_rem[:, ks]).astype(BF)
            att[d, h] = _dot_nt(qd[d, h], (k * e_neg[:, ks]).astype(BF))
        for d, h in chains:
            v[d, h] = q_refs[d][0, rows[d], 2 * KD + h * DV:2 * KD + (h + 1) * DV]
            a = jnp.where(masks[d], att[d, h], 0.0).astype(BF)
            o = _dot(a, v[d, h]) + _dot_nt(qd[d, h], st_ref[d, h].astype(BF))
            o_refs[d][0, rows[d], h * DV:(h + 1) * DV] = o.astype(o_refs[d].dtype)
        for d, h in chains:
            ks = slice(h * DK, (h + 1) * DK)
            st_ref[d, h] = st_ref[d, h] * fac[d][3][:, ks] + _dot_tn(v[d, h], ksc[d, h])
        return carry

    lax.fori_loop(0, n_chunk, body, 0)

    @pl.when(j == pl.num_programs(1) - 1)
    def _():
        for d in range(2):
            for h in range(H):
                sn_ref[0, 0, d, h] = st_ref[d, h].T


def _gla_scan(qkv, g, s0):
    B, L, _ = qkv.shape
    tl = min(L, ROW_TILE)
    nl = L // tl
    zero_init = s0 is None
    wq = 2 * KD + D
    in_specs = [pl.BlockSpec((1, tl, wq), lambda b, j: (b, j, 0)),
                pl.BlockSpec((1, tl, wq), lambda b, j: (b, nl - 1 - j, 0)),
                pl.BlockSpec((1, tl, KD), lambda b, j: (b, j, 0)),
                pl.BlockSpec((1, tl, KD), lambda b, j: (b, nl - 1 - j, 1))]
    args = [qkv, qkv, g, g]
    st_spec = pl.BlockSpec((1, 1, 2, H, DK, DV), lambda b, j: (b, 0, 0, 0, 0, 0))
    if not zero_init:
        in_specs.append(st_spec)
        args.append(s0)
    return pl.pallas_call(
        functools.partial(_gla_scan_kernel, zero_init=zero_init),
        grid=(B, nl),
        in_specs=in_specs,
        out_specs=[pl.BlockSpec((1, tl, D), lambda b, j: (b, j, 0)),
                   pl.BlockSpec((1, tl, D), lambda b, j: (b, nl - 1 - j, 0)), st_spec],
        out_shape=[jax.ShapeDtypeStruct((B, L, D), BF), jax.ShapeDtypeStruct((B, L, D), BF),
                   jax.ShapeDtypeStruct((B, 1, 2, H, DK, DV), F32)],
        scratch_shapes=[pltpu.VMEM((2, H, DV, DK), F32)],
        compiler_params=_params(2),
    )(*args)


def _gla_out_kernel(of_ref, ob_ref, r_ref, on_ref, w_ref, x_ref, g1_ref, ng1_ref, ng2_ref, sc_ref, sh_ref, rw_ref,
                    x3_ref, h_ref, route_ref, rt_ref, cnt_ref):
    bb, tl, _ = x_ref.shape
    tm = bb * tl
    o = of_ref[...].astype(F32) + ob_ref[...].astype(F32)
    heads = []
    for h in range(H):
        oh = o[:, :, h * DV:(h + 1) * DV]
        heads.append(oh * lax.rsqrt(jnp.mean(oh * oh, axis=-1, keepdims=True) + RMS_EPS))
    o = jnp.concatenate(heads, -1) * on_ref[...] * r_ref[...].astype(F32)
    m = _dot(o.reshape(tm, D).astype(BF), w_ref[...])
    x3 = x_ref[...] + g1_ref[...] * _rms(m, ng1_ref[0]).reshape(bb, tl, D)
    x3_ref[...] = x3
    h2 = _rms(x3, ng2_ref[...]) * (1.0 + sc_ref[...]) + sh_ref[...]
    h_hi = h2.astype(BF)
    h_ref[...] = h_hi
    h_lo = (h2 - h_hi.astype(F32)).reshape(tm, D).astype(BF)
    both = _dot(h_hi.reshape(tm, D), rw_ref[...])
    logits = both[:, :LANES] + both[:, LANES:] + _dot(h_lo, rw_ref[:, :LANES])
    lane_i = lax.broadcasted_iota(jnp.int32, (tm, LANES), 1)
    lane = lane_i.astype(F32)
    logits = jnp.where(lane_i < N_EXPERTS, logits, -jnp.inf)
    m0 = jnp.max(logits, axis=-1, keepdims=True)
    i0 = jnp.min(jnp.where(logits == m0, lane, float(LANES)), axis=-1, keepdims=True)
    rest = jnp.where(lane == i0, -jnp.inf, logits)
    m1 = jnp.max(rest, axis=-1, keepdims=True)
    i1 = jnp.min(jnp.where(rest == m1, lane, float(LANES)), axis=-1, keepdims=True)
    e = jnp.exp(m1 - m0)
    w0 = 1.0 / (1.0 + e)
    w1 = e * w0
    onehot = jnp.where((lane == i0) | (lane == i1), 1.0, 0.0)
    ri = lax.broadcasted_iota(jnp.int32, (tm, tm), 0)
    ci = lax.broadcasted_iota(jnp.int32, (tm, tm), 1)
    same_tile = (ri // MOE_TILE) == (ci // MOE_TILE)
    before = jnp.where(same_tile, jnp.where(ci < ri, 1.0, 0.0), 0.0).astype(BF)
    prior = _dot(before, onehot.astype(BF))
    rank0 = jnp.sum(jnp.where(lane == i0, prior, 0.0), axis=-1, keepdims=True)
    rank1 = jnp.sum(jnp.where(lane == i1, prior, 0.0), axis=-1, keepdims=True)
    fields = (w0, w1, i0, i1, rank0, rank1)
    route = jnp.zeros((tm, LANES), F32)
    for n, val in enumerate(fields):
        route = jnp.where(lane_i == n, val, route)
    route_ref[...] = route
    route_t = route.T
    for s in range(tm // MOE_TILE):
        rows = slice(s * MOE_TILE, (s + 1) * MOE_TILE)
        cnt_ref[s] = jnp.broadcast_to(jnp.sum(onehot[rows], axis=0, keepdims=True), cnt_ref.shape[1:])
        rt_ref[s] = route_t[0:SUBLANES, rows]


def _gla_out(o_f, o_b, r_gate, onorm_g, w, x, g1, ng1, ng2, sc, sh, router_w):
    B, L, _ = x.shape
    r = _Rows(B, L)
    rw = jnp.pad(router_w, ((0, 0), (0, LANES - N_EXPERTS)))
    rw_hi = rw.astype(BF)
    rw = jnp.concatenate([rw_hi, (rw - rw_hi.astype(F32)).astype(BF)], axis=1)
    tm = r.tm
    n_sub = tm // MOE_TILE
    return pl.pallas_call(
        _gla_out_kernel,
        grid=(r.n,),
        in_specs=[r.act(D), r.act(D), r.act(D), _const((1, 1, D)), _const((D, D)), r.act(D), r.mod(g1),
                  _const((1, 1, D)), _const((1, 1, D)), r.mod(sc), r.mod(sh), _const((D, 2 * LANES))],
        out_specs=[r.act(D), r.act(D), pl.BlockSpec((tm, LANES), lambda i: (i, 0)),
                   pl.BlockSpec((n_sub, SUBLANES, MOE_TILE), lambda i: (i, 0, 0)),
                   pl.BlockSpec((n_sub, SUBLANES, LANES), lambda i: (i, 0, 0))],
        out_shape=[jax.ShapeDtypeStruct((B, L, D), F32), jax.ShapeDtypeStruct((B, L, D), BF),
                   jax.ShapeDtypeStruct((B * L, LANES), F32),
                   jax.ShapeDtypeStruct((r.n * n_sub, SUBLANES, MOE_TILE), F32),
                   jax.ShapeDtypeStruct((r.n * n_sub, SUBLANES, LANES), F32)],
        compiler_params=_params(1),
    )(o_f, o_b, r_gate, jnp.tile(onorm_g, H).reshape(1, 1, D), w.astype(BF), x, g1, ng1.reshape(1, 1, D),
      ng2.reshape(1, 1, D), sc, sh, rw)


SEG = 64
MAIN = 3


def _seg_start(seg_ref, step, e):
    return pl.multiple_of(seg_ref[step * N_EXPERTS + e], PACKED_ROWS)


def _dispatch_kernel(seg_ref, cnt_ref, h_ref, rt_ref, xs_in_hbm, xs_hbm, stage, extra, sem, esem):
    del xs_in_hbm
    i = pl.program_id(0)
    slot = i % 2
    tm = h_ref.shape[0]
    n_piece = tm // SEG
    h = h_ref[...]
    n_main = MAIN * SEG
    n_rest = tm - n_main

    def local_rank(e, n_rows):
        ex0, ex1, lr0, lr1 = (jnp.broadcast_to(rt_ref[0, n:n + 1, :], (n_rows, tm)) for n in range(2, 6))
        return jnp.where(ex0 == float(e), lr0, jnp.where(ex1 == float(e), lr1, -1.0))

    def row_index(n_rows):
        return lax.broadcasted_iota(jnp.int32, (n_rows, tm), 0).astype(F32)

    def row0(step, e, k):
        return _seg_start(seg_ref, step, e) + k * SEG

    def copy(step, buf, e, k):
        return pltpu.make_async_copy(stage.at[buf, e, pl.ds(k * SEG, SEG)], xs_hbm.at[pl.ds(row0(step, e, k), SEG)],
                                     sem.at[buf, e * MAIN + k])

    def live(step, e, k):
        return k * SEG < cnt_ref[step * N_EXPERTS + e]

    def each_main(step, buf, fn):
        for e in range(N_EXPERTS):
            for k in range(MAIN):
                @pl.when(live(step, e, k))
                def _(e=e, k=k):
                    fn(copy(step, buf, e, k))

    main_rows = row_index(n_main)
    for e in range(N_EXPERTS):
        pick = jnp.where(local_rank(e, n_main) == main_rows, 1.0, 0.0).astype(BF)
        stage[slot, e] = _dot(pick, h).astype(BF)

    @pl.when(i > 0)
    def _():
        each_main(i - 1, 1 - slot, lambda cp: cp.wait())

    each_main(i, slot, lambda cp: cp.start())
    for e in range(N_EXPERTS):
        @pl.when(live(i, e, MAIN))
        def _(e=e):
            more = jnp.where(local_rank(e, n_rest) - float(n_main) == row_index(n_rest), 1.0, 0.0).astype(BF)
            extra[...] = _dot(more, h).astype(BF)
            for k in range(MAIN, n_piece):
                @pl.when(live(i, e, k))
                def _(k=k):
                    cp = pltpu.make_async_copy(extra.at[pl.ds((k - MAIN) * SEG, SEG)],
                                               xs_hbm.at[pl.ds(row0(i, e, k), SEG)], esem)
                    cp.start()
                    cp.wait()

    @pl.when(i == pl.num_programs(0) - 1)
    def _():
        each_main(i, slot, lambda cp: cp.wait())


def _dispatch(h2, route_t, seg, cnt, xs, tm=MOE_TILE):
    T = h2.shape[0]
    grid_spec = pltpu.PrefetchScalarGridSpec(
        num_scalar_prefetch=2,
        grid=(T // tm,),
        in_specs=[pl.BlockSpec((tm, D), lambda i, s, c: (i, 0)),
                  pl.BlockSpec((1, SUBLANES, tm), lambda i, s, c: (i, 0, 0)),
                  pl.BlockSpec(memory_space=pl.ANY)],
        out_specs=pl.BlockSpec(memory_space=pl.ANY),
        scratch_shapes=[pltpu.VMEM((2, N_EXPERTS, MAIN * SEG, D), BF), pltpu.VMEM((tm - MAIN * SEG, D), BF),
                        pltpu.SemaphoreType.DMA((2, N_EXPERTS * MAIN)), pltpu.SemaphoreType.DMA(())],
    )
    return pl.pallas_call(
        _dispatch_kernel,
        grid_spec=grid_spec,
        out_shape=jax.ShapeDtypeStruct(xs.shape, xs.dtype),
        input_output_aliases={4: 0},
        compiler_params=_params(1),
    )(seg, cnt, h2, route_t, xs)


def _combine_kernel(seg_ref, cnt_ref, ys_hbm, route_ref, x_ref, g2_ref, ng_ref, o_ref, gbuf, extra, acc_ref, sem,
                    esem):
    i = pl.program_id(0)
    bb, tl, _ = x_ref.shape
    tm = bb * tl
    n_piece = tm // SEG

    def main_copy(step, slot, e, k):
        src = ys_hbm.at[pl.ds(_seg_start(seg_ref, step, e) + k * SEG, SEG)]
        return pltpu.make_async_copy(src, gbuf.at[slot, pl.ds((e * MAIN + k) * SEG, SEG)], sem.at[slot, e * MAIN + k])

    def fetch(step, slot):
        for e in range(N_EXPERTS):
            for k in range(MAIN):
                main_copy(step, slot, e, k).start()

    @pl.when(i == 0)
    def _():
        fetch(0, 0)

    @pl.when(i + 1 < pl.num_programs(0))
    def _():
        fetch(i + 1, (i + 1) % 2)

    slot = i % 2
    route = route_ref[...]

    def spread_cols(e, r0, width):
        ef = float(e)
        gate = jnp.where(route[:, 2:3] == ef, route[:, 0:1], jnp.where(route[:, 3:4] == ef, route[:, 1:2], 0.0))
        rank = jnp.where(route[:, 2:3] == ef, route[:, 4:5], jnp.where(route[:, 3:4] == ef, route[:, 5:6], -1.0))
        col = lax.broadcasted_iota(jnp.int32, (tm, width), 1).astype(F32) + float(r0)
        return (jnp.where(rank == col, 1.0, 0.0) * gate).astype(BF)

    n_main = MAIN * SEG
    lane = lax.broadcasted_iota(jnp.int32, (tm, N_EXPERTS * n_main), 1).astype(F32)
    spread = jnp.zeros((tm, N_EXPERTS * n_main), F32)
    for j in range(2):
        rank = route[:, 4 + j:5 + j]
        col = jnp.where(rank < float(n_main), route[:, 2 + j:3 + j] * float(n_main) + rank, -1.0)
        spread = spread + jnp.where(lane == col, 1.0, 0.0) * route[:, j:j + 1]
    spread = spread.astype(BF)
    for e in range(N_EXPERTS):
        for k in range(MAIN):
            main_copy(i, slot, e, k).wait()
    acc_ref[...] = _dot(spread, gbuf[slot].astype(BF))
    for e in range(N_EXPERTS):
        for k in range(MAIN, n_piece):
            @pl.when(k * SEG < cnt_ref[i * N_EXPERTS + e])
            def _(e=e, k=k):
                src = ys_hbm.at[pl.ds(_seg_start(seg_ref, i, e) + k * SEG, SEG)]
                cp = pltpu.make_async_copy(src, extra, esem)
                cp.start()
                more = spread_cols(e, k * SEG, SEG)
                cp.wait()
                acc_ref[...] += _dot(more, extra[...].astype(BF))

    o_ref[...] = x_ref[...] + g2_ref[...] * _rms(acc_ref[...], ng_ref[0]).reshape(bb, tl, D)


def _combine(ys, route, seg, cnt, x, g2, ng):
    B, L, _ = x.shape
    r = _Rows(B, L, MOE_TILE)
    tm = r.tm
    nl = r.nl
    act = pl.BlockSpec((r.bb, r.tl, D), lambda i, s, c: (i // nl, i % nl, 0))
    g2_spec = (pl.BlockSpec((1, 1, D), lambda i, s, c: (0, 0, 0)) if g2.shape[0] == 1 else
               pl.BlockSpec((r.bb, 1, D), lambda i, s, c: (i // nl, 0, 0)))
    grid_spec = pltpu.PrefetchScalarGridSpec(
        num_scalar_prefetch=2,
        grid=(r.n,),
        in_specs=[pl.BlockSpec(memory_space=pl.ANY), pl.BlockSpec((tm, LANES), lambda i, s, c: (i, 0)), act, g2_spec,
                  pl.BlockSpec((1, 1, D), lambda i, s, c: (0, 0, 0))],
        out_specs=act,
        scratch_shapes=[pltpu.VMEM((2, N_EXPERTS * MAIN * SEG, D), BF), pltpu.VMEM((SEG, D), BF),
                        pltpu.VMEM((tm, D), F32), pltpu.SemaphoreType.DMA((2, N_EXPERTS * MAIN)),
                        pltpu.SemaphoreType.DMA(())],
    )
    return pl.pallas_call(
        _combine_kernel,
        grid_spec=grid_spec,
        out_shape=jax.ShapeDtypeStruct((B, L, D), F32),
        compiler_params=_params(1),
    )(seg, cnt, ys, route, x, g2, ng.reshape(1, 1, D))


def _moe(passes, ng, w_gu, w_d, tmoe=1024):
    cntb = jnp.concatenate([ps['blk_cnt'][:, 0, :N_EXPERTS] for ps in passes], axis=0).astype(jnp.int32)
    n_tok_blocks = cntb.shape[0]
    n_tok = sum(ps['h2'].shape[0] * ps['h2'].shape[1] for ps in passes)
    held = (cntb + PACKED_ROWS - 1) // PACKED_ROWS * PACKED_ROWS
    before = jnp.cumsum(held, axis=0) - held
    cnt = jnp.sum(held, axis=0)
    p_cnt = (cnt + MAIN * SEG + tmoe - 1) // tmoe * tmoe
    p_end = jnp.cumsum(p_cnt)
    p_start = p_end - p_cnt
    seg = p_start[None, :] + before
    n_rows = 2 * n_tok + N_EXPERTS * (n_tok_blocks * (PACKED_ROWS - 1) + tmoe + MAIN * SEG)
    n_rows = (n_rows + tmoe - 1) // tmoe * tmoe
    n_blocks = n_rows // tmoe
    starts = jnp.arange(n_blocks, dtype=jnp.int32) * tmoe
    block_e = jnp.minimum(jnp.sum(starts[:, None] >= p_end[None, :], axis=-1), N_EXPERTS - 1).astype(jnp.int32)
    n_valid = (p_end[-1:] // tmoe).astype(jnp.int32)
    xs = jnp.zeros((n_rows, D), BF)
    first = 0
    for ps in passes:
        B, L, _ = ps['h2'].shape
        nb = ps['blk_cnt'].shape[0]
        ps['seg'] = seg[first:first + nb].reshape(-1)
        ps['cnt'] = cntb[first:first + nb].reshape(-1)
        first += nb
        xs = _dispatch(ps['h2'].reshape(B * L, D), ps['route_t'], ps['seg'], ps['cnt'], xs)
    ys = _swiglu(xs, w_gu, w_d, block_e, n_valid, tmoe, D_FF_EXPERT // 2, BF)
    return [_combine(ys, ps['route'], ps['seg'], ps['cnt'], ps['x3'], ps['g2'], ng) for ps in passes]


def _trunk(x, mods, n_row, s0, p):
    B, L, _ = x.shape
    T = B * L
    ng = p['norm_g']
    sh1, sc1, g1, sh2, sc2, g2 = mods[0]
    v, x0 = _hy_in(x, ng[0, 0], sc1, sh1, p['hy_in_w'][0], p['hy_in_b'][0], p['hy_sc_w'][0], p['hy_sc_b'][0], n_row)
    kern = _hyena_kernel_taps(L, p['hy_f_w1'][0], p['hy_f_b1'][0], p['hy_f_w2'][0], p['hy_f_b2'][0],
                              p['hy_f_freq'][0], p['hy_f_w3'][0])
    if L <= 512:
        vx = _fftconv_short(v, x0, kern, p['hy_skip'][0])
    else:
        vx = _fftconv_long(v, x0, kern, p['hy_skip'][0])
    x1, h2 = _hy_out(vx, p['hy_out_w'][0], p['hy_out_b'][0], x, g1, ng[0, 1], ng[0, 2], sc2, sh2)
    tm = 1024
    ones = jnp.zeros((T // tm,), jnp.int32)
    f = _swiglu(h2.reshape(T, D), p['ffn_wgu'], p['ffn_wd'], ones, jnp.full((1,), T // tm, jnp.int32), tm, D_FF, F32)
    g2_0, ng3_0 = g2, ng[0, 3]
    sh1, sc1, g1, sh2, sc2, g2 = mods[1]
    x2, qkv, gate, r_gate = _gla_in(x1, f, g2_0, ng3_0, ng[1, 0], sc1, sh1, p['gla_qkv_w'][0], p['gla_gk_w1'][0],
                                    p['gla_gk_w2'][0], p['gla_gk_b'][0], p['gla_r_w'][0], p['gla_r_b'][0])
    o_f, o_b, s_new = _gla_scan(qkv, gate, s0)
    x3, h2, route, route_t, blk_cnt = _gla_out(o_f, o_b, r_gate, p['gla_onorm_g'][0], p['gla_out_w'][0], x2, g1,
                                               ng[1, 1], ng[1, 2], sc2, sh2, p['moe_router'][0])
    return dict(x3=x3, h2=h2, route=route, route_t=route_t, blk_cnt=blk_cnt, g2=g2), s_new


def kernel(x_prompt, x_sample, state_gla, c, c_ctx, ada_w, ada_b, norm_g, hy_in_w, hy_in_b, hy_sc_w, hy_sc_b, hy_f_w1, hy_f_b1, hy_f_w2, hy_f_b2, hy_f_freq, hy_f_w3, hy_skip, hy_out_w, hy_out_b, gla_qkv_w, gla_gk_w1, gla_gk_w2, gla_gk_b, gla_r_w, gla_r_b, gla_onorm_g, gla_out_w, ffn_wgu, ffn_wd, moe_router, moe_wgu, moe_wd):
    p = dict(norm_g=norm_g, hy_in_w=hy_in_w, hy_in_b=hy_in_b, hy_sc_w=hy_sc_w, hy_sc_b=hy_sc_b, hy_f_w1=hy_f_w1,
             hy_f_b1=hy_f_b1, hy_f_w2=hy_f_w2, hy_f_b2=hy_f_b2, hy_f_freq=hy_f_freq, hy_f_w3=hy_f_w3,
             hy_skip=hy_skip, hy_out_w=hy_out_w, hy_out_b=hy_out_b, gla_qkv_w=gla_qkv_w, gla_gk_w1=gla_gk_w1,
             gla_gk_w2=gla_gk_w2, gla_gk_b=gla_gk_b, gla_r_w=gla_r_w, gla_r_b=gla_r_b, gla_onorm_g=gla_onorm_g,
             gla_out_w=gla_out_w, ffn_wgu=ffn_wgu.astype(BF), ffn_wd=ffn_wd.astype(BF),
             moe_router=moe_router, moe_wgu=moe_wgu[0].astype(BF), moe_wd=moe_wd[0].astype(BF))
    n_dec = c.shape[0]
    cond = jnp.concatenate([c_ctx[None, :], c, jnp.zeros((16 - 1 - n_dec, D), F32)], axis=0)
    mod = _ada(cond, ada_w, ada_b)
    mods_ctx = [[m[:, None, :] for m in jnp.split(mod[l, 0:1], 6, axis=-1)] for l in range(DEPTH)]
    mods_dec = [[m[:, None, :] for m in jnp.split(mod[l, 1:1 + n_dec], 6, axis=-1)] for l in range(DEPTH)]
    ctx, state_new = _trunk(x_prompt, mods_ctx, x_prompt.shape[1], None, p)
    grid_w = 64
    dec, _ = _trunk(x_sample, mods_dec, grid_w, state_gla, p)
    y_prompt, y_sample = _moe([ctx, dec], norm_g[1, 3], p['moe_wgu'], p['moe_wd'])
    return y_prompt, y_sample, state_new
```

```python
import functools
import math

import jax
import jax.numpy as jnp
from jax import lax
from jax.experimental import pallas as pl
from jax.experimental.pallas import tpu as pltpu

F32 = jnp.float32
BF = jnp.bfloat16

D = 1024
RMS_EPS = 1e-6
DEPTH = 2
HY_SHORT = 3
HY_EMB = 33
HY_BANDS = (HY_EMB - 1) // 2
HY_FFN = 64
HY_MAX_DECAY = math.log(1e-2) / 0.3
HY_MIN_DECAY = math.log(1e-2) / 1.5
H = 4
DK = 128
DV = 256
KD = H * DK
GLA_RANK = 16
GLA_GATE_NORM = 16.0
CHUNK = 64
D_FF = 11 * D // 4
N_EXPERTS = 8
D_FF_EXPERT = 7 * D // 2

LANES = 128
SUBLANES = 8
VMEM_LIMIT_BYTES = 56 * 1024 * 1024
ROW_TILE = 512
MOE_TILE = 512
PACKED_ROWS = 16
FFT_S2 = 128


def _params(n_axes):
    return pltpu.CompilerParams(dimension_semantics=("arbitrary",) * n_axes,
                                vmem_limit_bytes=VMEM_LIMIT_BYTES)


def _dot(a, b):
    return jnp.dot(a, b, preferred_element_type=F32)


def _dot_nt(a, b):
    return lax.dot_general(a, b, (((1,), (1,)), ((), ())), preferred_element_type=F32)


def _dot_tn(a, b):
    return lax.dot_general(a, b, (((0,), (0,)), ((), ())), preferred_element_type=F32)


def _rms(x, g):
    return x * lax.rsqrt(jnp.mean(x * x, axis=-1, keepdims=True) + RMS_EPS) * g


def _silu(x):
    return x * (1.0 / (1.0 + jnp.exp(-x)))


class _Rows:
    def __init__(self, B, L, tm=ROW_TILE):
        self.B, self.L = B, L
        if L >= tm:
            self.bb, self.tl = 1, tm
        else:
            self.bb, self.tl = tm // L, L
        assert L % self.tl == 0 and B % self.bb == 0
        self.nl = L // self.tl
        self.n = (B // self.bb) * self.nl
        self.tm = self.bb * self.tl

    def act(self, width):
        nl = self.nl
        return pl.BlockSpec((self.bb, self.tl, width), lambda i: (i // nl, i % nl, 0))

    def mod(self, m):
        nl = self.nl
        if m.shape[0] == 1:
            return pl.BlockSpec((1, 1, D), lambda i: (0, 0, 0))
        return pl.BlockSpec((self.bb, 1, D), lambda i: (i // nl, 0, 0))


def _const(shape):
    nd = len(shape)
    return pl.BlockSpec(shape, lambda *_: (0,) * nd)


def _ada_kernel(c_ref, w_ref, b_ref, o_ref):
    cs = _silu(c_ref[...])
    o_ref[0] = _dot(cs.astype(BF), w_ref[0].astype(BF)) + b_ref[0]


def _ada(cond, ada_w, ada_b):
    R = cond.shape[0]
    tn = 1536
    return pl.pallas_call(
        _ada_kernel,
        grid=(DEPTH, 6 * D // tn),
        in_specs=[pl.BlockSpec((R, D), lambda l, n: (0, 0)),
                  pl.BlockSpec((1, D, tn), lambda l, n: (l, 0, n)),
                  pl.BlockSpec((1, 1, tn), lambda l, n: (l, 0, n))],
        out_specs=pl.BlockSpec((1, R, tn), lambda l, n: (l, 0, n)),
        out_shape=jax.ShapeDtypeStruct((DEPTH, R, 6 * D), F32),
        compiler_params=_params(2),
    )(cond, ada_w, ada_b.reshape(DEPTH, 1, 6 * D))


def _hy_in_kernel(x_ref, ng_ref, sc_ref, sh_ref, w_ref, b_ref, cw_ref, cb_ref, v_ref, x0_ref, *, n_row):
    x = x_ref[...]
    bb, tl, _ = x.shape
    tm = bb * tl
    h = _rms(x, ng_ref[...]) * (1.0 + sc_ref[...]) + sh_ref[...]
    hb = h.reshape(tm, D).astype(BF)
    pos = lax.broadcasted_iota(jnp.int32, (tm, D), 0) & (n_row - 1)
    first = pos == 0
    last = pos == n_row - 1
    parts = []
    for j in range(3):
        cols = slice(j * D, (j + 1) * D)
        u = _dot(hb, w_ref[:, cols]) + b_ref[:, cols]
        up = jnp.where(first, 0.0, pltpu.roll(u, 1, 0))
        dn = jnp.where(last, 0.0, pltpu.roll(u, tm - 1, 0))
        parts.append(cb_ref[:, cols] + up * cw_ref[0:1, cols] + u * cw_ref[1:2, cols] + dn * cw_ref[2:3, cols])
    x0, x1, v = parts
    v_ref[...] = (v * x1).astype(BF).reshape(bb, tl, D)
    x0_ref[...] = x0.astype(BF).reshape(bb, tl, D)


def _hy_in(x, ng, sc, sh, w, b, cw, cb, n_row):
    B, L, _ = x.shape
    assert n_row & (n_row - 1) == 0
    r = _Rows(B, L)
    assert r.tl % n_row == 0
    return pl.pallas_call(
        functools.partial(_hy_in_kernel, n_row=n_row),
        grid=(r.n,),
        in_specs=[r.act(D), _const((1, 1, D)), r.mod(sc), r.mod(sh),
                  _const((D, 3 * D)), _const((1, 3 * D)), _const((HY_SHORT, 3 * D)), _const((1, 3 * D))],
        out_specs=[r.act(D), r.act(D)],
        out_shape=[jax.ShapeDtypeStruct((B, L, D), BF)] * 2,
        compiler_params=_params(1),
    )(x, ng.reshape(1, 1, D), sc, sh, w.astype(BF), b.reshape(1, 3 * D), cw, cb.reshape(1, 3 * D))


def _filter_kernel(z_ref, w1_ref, b1_ref, w2_ref, b2_ref, fr_ref, w3_ref, dl_ref, o_ref, *, L):
    tr = z_ref.shape[0]
    hp = lax.Precision.HIGHEST
    z = z_ref[...]
    h = jnp.sin(fr_ref[0:1, :] * (jnp.dot(z, w1_ref[...], precision=hp, preferred_element_type=F32) + b1_ref[...]))
    h = jnp.sin(fr_ref[1:2, :] * (jnp.dot(h, w2_ref[...], precision=hp, preferred_element_type=F32) + b2_ref[...]))
    hw = _dot(h.astype(BF), w3_ref[...])
    n = pl.program_id(0) * tr + lax.broadcasted_iota(jnp.int32, (tr, D), 0)
    taps = jnp.where(n < L, hw[:, :D], hw[:, D:]) * jnp.exp(-z[:, 0:1] * dl_ref[...])
    o_ref[...] = jnp.where(n == L, 0.0, taps)


def _hyena_kernel_taps(L, w1, b1, w2, b2, freq, w3):
    n = jnp.arange(2 * L, dtype=jnp.int32)
    pos = jnp.where(n < L, n, 2 * L - n) % L
    t = (pos.astype(F32) / (L - 1))[:, None]
    w = (2.0 * math.pi * pos.astype(F32) / L)[:, None]
    f = jnp.linspace(1e-4, HY_BANDS - 1, HY_BANDS, dtype=F32)[None, :]
    z = jnp.concatenate([t, jnp.cos(f * w), -jnp.sin(f * w)], axis=-1)
    z = jnp.pad(z, ((0, 0), (0, LANES - HY_EMB)))
    pad = LANES - HY_FFN
    w1p = jnp.pad(w1, ((0, LANES - HY_EMB), (0, pad)))
    w2p = jnp.pad(w2, ((0, pad), (0, pad)))
    w3p = jnp.pad(w3, ((0, pad), (0, 0))).astype(BF)
    b1p = jnp.pad(b1, (0, pad)).reshape(1, LANES)
    b2p = jnp.pad(b2, (0, pad)).reshape(1, LANES)
    frp = jnp.pad(freq, ((0, 0), (0, pad)))
    deltas = jnp.abs(jnp.linspace(HY_MIN_DECAY, HY_MAX_DECAY, D, dtype=F32)).reshape(1, D)
    tr = 512
    return pl.pallas_call(
        functools.partial(_filter_kernel, L=L),
        grid=(2 * L // tr,),
        in_specs=[pl.BlockSpec((tr, LANES), lambda i: (i, 0)), _const((LANES, LANES)), _const((1, LANES)),
                  _const((LANES, LANES)), _const((1, LANES)), _const((2, LANES)), _const((LANES, 2 * D)),
                  _const((1, D))],
        out_specs=pl.BlockSpec((tr, D), lambda i: (i, 0)),
        out_shape=jax.ShapeDtypeStruct((2 * L, D), F32),
        compiler_params=_params(1),
    )(z, w1p, b1p, w2p, b2p, frp, w3p, deltas)


def _cis(rows, cols, n, sign, scale=1.0):
    ph = (rows[:, None] * cols[None, :]) % n
    ang = ph.astype(F32) * (2.0 * math.pi / n)
    return jnp.cos(ang) * scale, jnp.sin(ang) * (sign * scale)


def _cplx_block(cr, ci):
    return jnp.concatenate([jnp.concatenate([cr, -ci], 1), jnp.concatenate([ci, cr], 1)], 0)


def _fft_short_kernel(v_ref, x0_ref, kern_ref, mk_ref, mf_ref, mi_ref, sk_ref, o_ref, ks_ref, *, L):
    n2 = 2 * L

    @pl.when(pl.program_id(0) == 0)
    def _():
        ks_ref[...] = _dot(mk_ref[...], kern_ref[...].astype(BF))

    z = jnp.concatenate([v_ref[0], v_ref[1]], 0)
    u = _dot(mf_ref[...], z)
    ur, ui = u[:n2], u[n2:]
    kr, ki = ks_ref[:n2, :], ks_ref[n2:, :]
    y = jnp.concatenate([ur * kr - ui * ki, ur * ki + ui * kr], 0).astype(BF)
    t = _dot(mi_ref[...], y)
    for j in range(2):
        conv = t[j * L:(j + 1) * L]
        o_ref[j] = ((conv + v_ref[j].astype(F32) * sk_ref[...]) * x0_ref[j].astype(F32)).astype(BF)


def _fftconv_short(v, x0, kern, skip):
    B, L, _ = v.shape
    n2 = 2 * L
    k = jnp.arange(n2, dtype=jnp.int32)
    s = jnp.arange(L, dtype=jnp.int32)
    fr, fi = _cis(k, s, n2, -1.0)
    mf = _cplx_block(fr, fi).astype(BF)
    kr, ki = _cis(k, k, n2, -1.0)
    mk = jnp.concatenate([kr, ki], 0).astype(BF)
    ir, ii = _cis(s, k, n2, 1.0, 1.0 / n2)
    mi = _cplx_block(ir, ii).astype(BF)
    pair = lambda p: (p, 0, 0)
    return pl.pallas_call(
        functools.partial(_fft_short_kernel, L=L),
        grid=(B // 2,),
        in_specs=[pl.BlockSpec((2, L, D), pair), pl.BlockSpec((2, L, D), pair), _const((n2, D)),
                  _const((2 * n2, n2)), _const((2 * n2, 2 * L)), _const((2 * L, 2 * n2)), _const((1, D))],
        out_specs=pl.BlockSpec((2, L, D), pair),
        out_shape=jax.ShapeDtypeStruct((B, L, D), BF),
        scratch_shapes=[pltpu.VMEM((2 * n2, D), F32)],
        compiler_params=_params(1),
    )(v, x0, kern, mk, mf, mi, skip.reshape(1, D))


def _lmul_kernel(m_ref, x_ref, o_ref):
    o_ref[0] = _dot(m_ref[...], x_ref[0].astype(BF)).astype(o_ref.dtype)


def _lmul(m, x, cb=8192):
    G, K, NC = x.shape
    R = m.shape[0]
    return pl.pallas_call(
        _lmul_kernel,
        grid=(G, NC // cb),
        in_specs=[pl.BlockSpec((R, K), lambda g, c: (0, 0)), pl.BlockSpec((1, K, cb), lambda g, c: (g, 0, c))],
        out_specs=pl.BlockSpec((1, R, cb), lambda g, c: (g, 0, c)),
        out_shape=jax.ShapeDtypeStruct((G, R, NC), BF),
        compiler_params=_params(2),
    )(m, x)


def _fft_mid_kernel(a_ref, af_ref, mf_ref, mi_ref, o_ref):
    s2 = FFT_S2
    n_pair = a_ref.shape[0]
    mf = mf_ref[0]
    ks = _dot(mf, jnp.concatenate([af_ref[0, 0], af_ref[1, 0]], 0))
    kr, ki = ks[:s2], ks[s2:]
    us = [_dot(mf, jnp.concatenate([a_ref[p, 0, 0], a_ref[p, 1, 0]], 0)) for p in range(n_pair)]
    ys = [jnp.concatenate([u[:s2] * kr - u[s2:] * ki, u[:s2] * ki + u[s2:] * kr], 0).astype(BF) for u in us]
    zs = [_dot(mi_ref[0], y).astype(BF) for y in ys]
    for p, z in enumerate(zs):
        o_ref[p, 0, 0] = z[:s2]
        o_ref[p, 1, 0] = z[s2:]


def _lmul_out_kernel(m_ref, z_ref, v_ref, x0_ref, sk_ref, o_ref):
    conv = _dot(m_ref[...], z_ref[0])
    o_ref[0] = ((conv + v_ref[0].astype(F32) * sk_ref[...]) * x0_ref[0].astype(F32)).astype(BF)


def _fftconv_long(v, x0, kern, skip, cb=8192):
    B, L, _ = v.shape
    n2 = 2 * L
    s2 = FFT_S2
    s1 = n2 // s2
    s1h = s1 // 2
    G = B // 2
    nc = s2 * D
    i1 = jnp.arange(s1, dtype=jnp.int32)
    i1h = jnp.arange(s1h, dtype=jnp.int32)
    i2 = jnp.arange(s2, dtype=jnp.int32)
    cr, ci = _cis(i1, i1h, s1, -1.0)
    m1 = _cplx_block(cr, ci).astype(BF)
    cr, ci = _cis(i1, i1, s1, -1.0)
    m1f = jnp.concatenate([cr, ci], 0).astype(BF)
    kk = (i1[:, None] + s1 * i2[None, :]).reshape(-1)
    gr, gi = _cis(kk, i2, n2, -1.0)
    mf = jax.vmap(_cplx_block)(gr.reshape(s1, s2, s2), gi.reshape(s1, s2, s2)).astype(BF)
    hr, hi = _cis(i2, kk, n2, 1.0, 1.0 / n2)
    hr = hr.reshape(s2, s1, s2).transpose(1, 0, 2)
    hi = hi.reshape(s2, s1, s2).transpose(1, 0, 2)
    mi = jax.vmap(_cplx_block)(hr, hi).astype(BF)
    er, ei = _cis(i1h, i1, s1, 1.0)
    m3 = _cplx_block(er, ei).astype(BF)

    a = _lmul(m1, v.reshape(G, 2 * s1h, nc), cb)
    af = _lmul(m1f, kern.reshape(1, s1, nc), cb)
    zz = pl.pallas_call(
        _fft_mid_kernel,
        grid=(s1,),
        in_specs=[pl.BlockSpec((G, 2, 1, s2, D), lambda k: (0, 0, k, 0, 0)),
                  pl.BlockSpec((2, 1, s2, D), lambda k: (0, k, 0, 0)),
                  pl.BlockSpec((1, 2 * s2, 2 * s2), lambda k: (k, 0, 0)),
                  pl.BlockSpec((1, 2 * s2, 2 * s2), lambda k: (k, 0, 0))],
        out_specs=pl.BlockSpec((G, 2, 1, s2, D), lambda k: (0, 0, k, 0, 0)),
        out_shape=jax.ShapeDtypeStruct((G, 2, s1, s2, D), BF),
        compiler_params=_params(1),
    )(a.reshape(G, 2, s1, s2, D), af.reshape(2, s1, s2, D), mf, mi)
    blk = lambda g, c: (g, 0, c)
    out = pl.pallas_call(
        _lmul_out_kernel,
        grid=(G, nc // cb),
        in_specs=[pl.BlockSpec((2 * s1h, 2 * s1), lambda g, c: (0, 0)), pl.BlockSpec((1, 2 * s1, cb), blk),
                  pl.BlockSpec((1, 2 * s1h, cb), blk), pl.BlockSpec((1, 2 * s1h, cb), blk),
                  pl.BlockSpec((1, cb), lambda g, c: (0, 0))],
        out_specs=pl.BlockSpec((1, 2 * s1h, cb), blk),
        out_shape=jax.ShapeDtypeStruct((G, 2 * s1h, nc), BF),
        compiler_params=_params(2),
    )(m3, zz.reshape(G, 2 * s1, nc), v.reshape(G, 2 * s1h, nc), x0.reshape(G, 2 * s1h, nc),
      jnp.tile(skip.reshape(1, D), (1, cb // D)))
    return out.reshape(B, L, D)


def _hy_out_kernel(vx_ref, w_ref, b_ref, x_ref, g1_ref, ng1_ref, ng2_ref, sc_ref, sh_ref, x1_ref, h_ref):
    bb, tl, _ = x_ref.shape
    m = _dot(vx_ref[...].reshape(bb * tl, D), w_ref[...]) + b_ref[...]
    x1 = x_ref[...] + g1_ref[...] * _rms(m, ng1_ref[0]).reshape(bb, tl, D)
    x1_ref[...] = x1
    h_ref[...] = (_rms(x1, ng2_ref[...]) * (1.0 + sc_ref[...]) + sh_ref[...]).astype(h_ref.dtype)


def _hy_out(vx, w, b, x, g1, ng1, ng2, sc, sh):
    B, L, _ = x.shape
    r = _Rows(B, L)
    return pl.pallas_call(
        _hy_out_kernel,
        grid=(r.n,),
        in_specs=[r.act(D), _const((D, D)), _const((1, D)), r.act(D), r.mod(g1), _const((1, 1, D)),
                  _const((1, 1, D)), r.mod(sc), r.mod(sh)],
        out_specs=[r.act(D), r.act(D)],
        out_shape=[jax.ShapeDtypeStruct((B, L, D), F32), jax.ShapeDtypeStruct((B, L, D), BF)],
        compiler_params=_params(1),
    )(vx, w.astype(BF), b.reshape(1, D), x, g1, ng1.reshape(1, 1, D), ng2.reshape(1, 1, D), sc, sh)


def _swiglu_kernel(be_ref, nv_ref, x_ref, wg_ref, wu_ref, wd_ref, o_ref, acc_ref):
    del be_ref
    f = pl.program_id(1)

    @pl.when(f == 0)
    def _():
        acc_ref[...] = jnp.zeros_like(acc_ref)

    @pl.when(pl.program_id(0) < nv_ref[0])
    def _():
        x = x_ref[...].astype(BF)
        tf = wg_ref.shape[2]
        sub = 256 if tf % 256 == 0 else tf
        part = None
        for c in range(tf // sub):
            cols = slice(c * sub, (c + 1) * sub)
            g = _dot(x, wg_ref[0, :, cols])
            u = _dot(x, wu_ref[0, :, cols])
            y = _dot((_silu(g) * u).astype(BF), wd_ref[0, cols, :])
            part = y if part is None else part + y
        acc_ref[...] += part

    @pl.when(f == pl.num_programs(1) - 1)
    def _():
        o_ref[...] = acc_ref[...].astype(o_ref.dtype)


def _swiglu(x, w_gu, w_d, block_e, n_valid, tm, tf, out_dtype):
    rows = x.shape[0]
    F = w_d.shape[1]
    nf = F // tf
    assert rows % tm == 0 and F % tf == 0
    grid_spec = pltpu.PrefetchScalarGridSpec(
        num_scalar_prefetch=2,
        grid=(rows // tm, nf),
        in_specs=[pl.BlockSpec((tm, D), lambda i, f, be, nv: (i, 0)),
                  pl.BlockSpec((1, D, tf), lambda i, f, be, nv: (be[i], 0, f)),
                  pl.BlockSpec((1, D, tf), lambda i, f, be, nv: (be[i], 0, nf + f)),
                  pl.BlockSpec((1, tf, D), lambda i, f, be, nv: (be[i], f, 0))],
        out_specs=pl.BlockSpec((tm, D), lambda i, f, be, nv: (i, 0)),
        scratch_shapes=[pltpu.VMEM((tm, D), F32)],
    )
    return pl.pallas_call(
        _swiglu_kernel,
        grid_spec=grid_spec,
        out_shape=jax.ShapeDtypeStruct((rows, D), out_dtype),
        compiler_params=_params(2),
    )(block_e, n_valid, x, w_gu, w_gu, w_d)


def _gla_in_kernel(x1_ref, f_ref, g2_ref, ng3_ref, ng0_ref, sc_ref, sh_ref, wqkv_ref, wg1_ref, wg2_ref, bg_ref,
                   wr_ref, br_ref, x2_ref, qkv_ref, g_ref, r_ref):
    bb, tl, _ = x1_ref.shape
    tm = bb * tl
    x2 = x1_ref[...] + g2_ref[...] * _rms(f_ref[...], ng3_ref[...])
    x2_ref[...] = x2
    h = (_rms(x2, ng0_ref[...]) * (1.0 + sc_ref[...]) + sh_ref[...]).reshape(tm, D).astype(BF)
    for c in range(4):
        cols = slice(c * KD, (c + 1) * KD)
        part = _dot(h, wqkv_ref[:, cols])
        if c == 0:
            part = part * (DK ** -0.5)
        qkv_ref[:, :, cols] = part.astype(BF).reshape(bb, tl, KD)
    low = _dot(h, wg1_ref[...]).astype(BF)
    a = _dot(low, wg2_ref[...]) + bg_ref[...]
    log_sig = jnp.minimum(a, 0.0) - jnp.log(1.0 + jnp.exp(-jnp.abs(a)))
    g_ref[...] = (log_sig / GLA_GATE_NORM).reshape(bb, tl, 2 * KD)
    r_ref[...] = _silu(_dot(h, wr_ref[...]) + br_ref[...]).astype(BF).reshape(bb, tl, D)


def _gla_in(x1, f, g2, ng3, ng0, sc, sh, qkv_w, gk_w1, gk_w2, gk_b, r_w, r_b):
    B, L, _ = x1.shape
    r = _Rows(B, L)
    w1 = jnp.zeros((D, LANES), F32).at[:, :GLA_RANK].set(gk_w1[0]).at[:, GLA_RANK:2 * GLA_RANK].set(gk_w1[1])
    w2 = jnp.zeros((LANES, 2 * KD), F32).at[:GLA_RANK, :KD].set(gk_w2[0]).at[GLA_RANK:2 * GLA_RANK, KD:].set(gk_w2[1])
    return pl.pallas_call(
        _gla_in_kernel,
        grid=(r.n,),
        in_specs=[r.act(D), r.act(D), r.mod(g2), _const((1, 1, D)), _const((1, 1, D)), r.mod(sc), r.mod(sh),
                  _const((D, 2 * KD + D)), _const((D, LANES)), _const((LANES, 2 * KD)), _const((1, 2 * KD)),
                  _const((D, D)), _const((1, D))],
        out_specs=[r.act(D), r.act(2 * KD + D), r.act(2 * KD), r.act(D)],
        out_shape=[jax.ShapeDtypeStruct((B, L, D), F32), jax.ShapeDtypeStruct((B, L, 2 * KD + D), BF),
                   jax.ShapeDtypeStruct((B, L, 2 * KD), F32), jax.ShapeDtypeStruct((B, L, D), BF)],
        compiler_params=_params(1),
    )(x1, f.reshape(B, L, D), g2, ng3.reshape(1, 1, D), ng0.reshape(1, 1, D), sc, sh, qkv_w.astype(BF),
      w1.astype(BF), w2.astype(BF), gk_b.reshape(1, 2 * KD), r_w.astype(BF), r_b.reshape(1, D))


def _gla_scan_kernel(*refs, zero_init):
    if zero_init:
        qf_ref, qb_ref, gf_ref, gb_ref, of_ref, ob_ref, sn_ref, st_ref = refs
    else:
        qf_ref, qb_ref, gf_ref, gb_ref, s0_ref, of_ref, ob_ref, sn_ref, st_ref = refs
    j = pl.program_id(1)
    tl = qf_ref.shape[1]
    n_chunk = tl // CHUNK

    @pl.when(j == 0)
    def _():
        for d in range(2):
            for h in range(H):
                if zero_init:
                    st_ref[d, h] = jnp.zeros((DV, DK), F32)
                else:
                    st_ref[d, h] = s0_ref[0, 0, d, h].T

    ri = lax.broadcasted_iota(jnp.int32, (CHUNK, CHUNK), 0)
    ci = lax.broadcasted_iota(jnp.int32, (CHUNK, CHUNK), 1)
    masks = (ci <= ri, ci >= ri)

    def decays(d, g_ref, rows):
        tri = jnp.where(masks[d], 1.0, 0.0).astype(BF)
        g = g_ref[0, rows, :]
        g_hi = g.astype(BF)
        g_lo = (g - g_hi.astype(F32)).astype(BF)
        b = _dot(tri, g_hi) + _dot(tri, g_lo)
        b_last = b[CHUNK - 1:CHUNK, :] if d == 0 else b[0:1, :]
        return jnp.exp(b), jnp.exp(-b), jnp.exp(b_last - b), jnp.exp(b_last)

    def chunk_rows(c):
        return (pl.ds(pl.multiple_of(c * CHUNK, CHUNK), CHUNK),
                pl.ds(pl.multiple_of((n_chunk - 1 - c) * CHUNK, CHUNK), CHUNK))

    def body(c, carry):
        rows = chunk_rows(c)
        q_refs, o_refs = (qf_ref, qb_ref), (of_ref, ob_ref)
        fac = (decays(0, gf_ref, rows[0]), decays(1, gb_ref, rows[1]))
        chains = [(d, h) for h in range(H) for d in range(2)]
        qd, ksc, att, v = {}, {}, {}, {}
        for d, h in chains:
            e_pos, e_neg, e_rem, _ = fac[d]
            ks = slice(h * DK, (h + 1) * DK)
            q = q_refs[d][0, rows[d], h * DK:(h + 1) * DK].astype(F32)
            k = q_refs[d][0, rows[d], KD + h * DK:KD + (h + 1) * DK].astype(F32)
            qd[d, h] = (q * e_pos[:, ks]).astype(BF)
            ksc[d, h] = (k * e_rem[:, ks]).astype(BF)
            att[d, h] = _dot_nt(qd[d, h], (k * e_neg[:, ks]).astype(BF))
        for d, h in chains:
            v[d, h] = q_refs[d][0, rows[d], 2 * KD + h * DV:2 * KD + (h + 1) * DV]
            a = jnp.where(masks[d], att[d, h], 0.0).astype(BF)
            o = _dot(a, v[d, h]) + _dot_nt(qd[d, h], st_ref[d, h].astype(BF))
            o_refs[d][0, rows[d], h * DV:(h + 1) * DV] = o.astype(o_refs[d].dtype)
        for d, h in chains:
            ks = slice(h * DK, (h + 1) * DK)
            st_ref[d, h] = st_ref[d, h] * fac[d][3][:, ks] + _dot_tn(v[d, h], ksc[d, h])
        return carry

    lax.fori_loop(0, n_chunk, body, 0)

    @pl.when(j == pl.num_programs(1) - 1)
    def _():
        for d in range(2):
            for h in range(H):
                sn_ref[0, 0, d, h] = st_ref[d, h].T


def _gla_scan(qkv, g, s0):
    B, L, _ = qkv.shape
    tl = min(L, ROW_TILE)
    nl = L // tl
    zero_init = s0 is None
    wq = 2 * KD + D
    in_specs = [pl.BlockSpec((1, tl, wq), lambda b, j: (b, j, 0)),
                pl.BlockSpec((1, tl, wq), lambda b, j: (b, nl - 1 - j, 0)),
                pl.BlockSpec((1, tl, KD), lambda b, j: (b, j, 0)),
                pl.BlockSpec((1, tl, KD), lambda b, j: (b, nl - 1 - j, 1))]
    args = [qkv, qkv, g, g]
    st_spec = pl.BlockSpec((1, 1, 2, H, DK, DV), lambda b, j: (b, 0, 0, 0, 0, 0))
    if not zero_init:
        in_specs.append(st_spec)
        args.append(s0)
    return pl.pallas_call(
        functools.partial(_gla_scan_kernel, zero_init=zero_init),
        grid=(B, nl),
        in_specs=in_specs,
        out_specs=[pl.BlockSpec((1, tl, D), lambda b, j: (b, j, 0)),
                   pl.BlockSpec((1, tl, D), lambda b, j: (b, nl - 1 - j, 0)), st_spec],
        out_shape=[jax.ShapeDtypeStruct((B, L, D), BF), jax.ShapeDtypeStruct((B, L, D), BF),
                   jax.ShapeDtypeStruct((B, 1, 2, H, DK, DV), F32)],
        scratch_shapes=[pltpu.VMEM((2, H, DV, DK), F32)],
        compiler_params=_params(2),
    )(*args)


def _gla_out_kernel(of_ref, ob_ref, r_ref, on_ref, w_ref, x_ref, g1_ref, ng1_ref, ng2_ref, sc_ref, sh_ref, rw_ref,
                    x3_ref, h_ref, route_ref, rt_ref, cnt_ref):
    bb, tl, _ = x_ref.shape
    tm = bb * tl
    o = of_ref[...].astype(F32) + ob_ref[...].astype(F32)
    heads = []
    for h in range(H):
        oh = o[:, :, h * DV:(h + 1) * DV]
        heads.append(oh * lax.rsqrt(jnp.mean(oh * oh, axis=-1, keepdims=True) + RMS_EPS))
    o = jnp.concatenate(heads, -1) * on_ref[...] * r_ref[...].astype(F32)
    m = _dot(o.reshape(tm, D).astype(BF), w_ref[...])
    x3 = x_ref[...] + g1_ref[...] * _rms(m, ng1_ref[0]).reshape(bb, tl, D)
    x3_ref[...] = x3
    h2 = _rms(x3, ng2_ref[...]) * (1.0 + sc_ref[...]) + sh_ref[...]
    h_hi = h2.astype(BF)
    h_ref[...] = h_hi
    h_lo = (h2 - h_hi.astype(F32)).reshape(tm, D).astype(BF)
    both = _dot(h_hi.reshape(tm, D), rw_ref[...])
    logits = both[:, :LANES] + both[:, LANES:] + _dot(h_lo, rw_ref[:, :LANES])
    lane_i = lax.broadcasted_iota(jnp.int32, (tm, LANES), 1)
    lane = lane_i.astype(F32)
    logits = jnp.where(lane_i < N_EXPERTS, logits, -jnp.inf)
    m0 = jnp.max(logits, axis=-1, keepdims=True)
    i0 = jnp.min(jnp.where(logits == m0, lane, float(LANES)), axis=-1, keepdims=True)
    rest = jnp.where(lane == i0, -jnp.inf, logits)
    m1 = jnp.max(rest, axis=-1, keepdims=True)
    i1 = jnp.min(jnp.where(rest == m1, lane, float(LANES)), axis=-1, keepdims=True)
    e = jnp.exp(m1 - m0)
    w0 = 1.0 / (1.0 + e)
    w1 = e * w0
    onehot = jnp.where((lane == i0) | (lane == i1), 1.0, 0.0)
    ri = lax.broadcasted_iota(jnp.int32, (tm, tm), 0)
    ci = lax.broadcasted_iota(jnp.int32, (tm, tm), 1)
    same_tile = (ri // MOE_TILE) == (ci // MOE_TILE)
    before = jnp.where(same_tile, jnp.where(ci < ri, 1.0, 0.0), 0.0).astype(BF)
    prior = _dot(before, onehot.astype(BF))
    rank0 = jnp.sum(jnp.where(lane == i0, prior, 0.0), axis=-1, keepdims=True)
    rank1 = jnp.sum(jnp.where(lane == i1, prior, 0.0), axis=-1, keepdims=True)
    fields = (w0, w1, i0, i1, rank0, rank1)
    route = jnp.zeros((tm, LANES), F32)
    for n, val in enumerate(fields):
        route = jnp.where(lane_i == n, val, route)
    route_ref[...] = route
    route_t = route.T
    for s in range(tm // MOE_TILE):
        rows = slice(s * MOE_TILE, (s + 1) * MOE_TILE)
        cnt_ref[s] = jnp.broadcast_to(jnp.sum(onehot[rows], axis=0, keepdims=True), cnt_ref.shape[1:])
        rt_ref[s] = route_t[0:SUBLANES, rows]


def _gla_out(o_f, o_b, r_gate, onorm_g, w, x, g1, ng1, ng2, sc, sh, router_w):
    B, L, _ = x.shape
    r = _Rows(B, L)
    rw = jnp.pad(router_w, ((0, 0), (0, LANES - N_EXPERTS)))
    rw_hi = rw.astype(BF)
    rw = jnp.concatenate([rw_hi, (rw - rw_hi.astype(F32)).astype(BF)], axis=1)
    tm = r.tm
    n_sub = tm // MOE_TILE
    return pl.pallas_call(
        _gla_out_kernel,
        grid=(r.n,),
        in_specs=[r.act(D), r.act(D), r.act(D), _const((1, 1, D)), _const((D, D)), r.act(D), r.mod(g1),
                  _const((1, 1, D)), _const((1, 1, D)), r.mod(sc), r.mod(sh), _const((D, 2 * LANES))],
        out_specs=[r.act(D), r.act(D), pl.BlockSpec((tm, LANES), lambda i: (i, 0)),
                   pl.BlockSpec((n_sub, SUBLANES, MOE_TILE), lambda i: (i, 0, 0)),
                   pl.BlockSpec((n_sub, SUBLANES, LANES), lambda i: (i, 0, 0))],
        out_shape=[jax.ShapeDtypeStruct((B, L, D), F32), jax.ShapeDtypeStruct((B, L, D), BF),
                   jax.ShapeDtypeStruct((B * L, LANES), F32),
                   jax.ShapeDtypeStruct((r.n * n_sub, SUBLANES, MOE_TILE), F32),
                   jax.ShapeDtypeStruct((r.n * n_sub, SUBLANES, LANES), F32)],
        compiler_params=_params(1),
    )(o_f, o_b, r_gate, jnp.tile(onorm_g, H).reshape(1, 1, D), w.astype(BF), x, g1, ng1.reshape(1, 1, D),
      ng2.reshape(1, 1, D), sc, sh, rw)


SEG = 128
MAIN = 2


def _seg_start(seg_ref, step, e):
    return pl.multiple_of(seg_ref[step * N_EXPERTS + e], PACKED_ROWS)


def _dispatch_kernel(seg_ref, cnt_ref, h_ref, rt_ref, xs_in_hbm, xs_hbm, stage, extra, sem, esem):
    del xs_in_hbm
    i = pl.program_id(0)
    slot = i % 2
    tm = h_ref.shape[0]
    n_piece = tm // SEG
    h = h_ref[...]
    assert n_piece == 2 * MAIN
    ex0, ex1, lr0, lr1 = (jnp.broadcast_to(rt_ref[0, n:n + 1, :], (MAIN * SEG, tm)) for n in range(2, 6))
    rank_row = lax.broadcasted_iota(jnp.int32, (MAIN * SEG, tm), 0).astype(F32)

    def row0(step, e, k):
        return _seg_start(seg_ref, step, e) + k * SEG

    def copy(step, buf, e, k):
        return pltpu.make_async_copy(stage.at[buf, e, pl.ds(k * SEG, SEG)], xs_hbm.at[pl.ds(row0(step, e, k), SEG)],
                                     sem.at[buf, e * MAIN + k])

    def live(step, e, k):
        return k * SEG < cnt_ref[step * N_EXPERTS + e]

    def each_main(step, buf, fn):
        for e in range(N_EXPERTS):
            for k in range(MAIN):
                @pl.when(live(step, e, k))
                def _(e=e, k=k):
                    fn(copy(step, buf, e, k), k)

    ranks = []
    for e in range(N_EXPERTS):
        ef = float(e)
        ranks.append(jnp.where(ex0 == ef, lr0, jnp.where(ex1 == ef, lr1, -1.0)))
        pick = jnp.where(ranks[e] == rank_row, 1.0, 0.0).astype(BF)
        stage[slot, e] = _dot(pick, h).astype(BF)

    @pl.when(i > 0)
    def _():
        each_main(i - 1, 1 - slot, lambda cp, k: cp.wait())

    each_main(i, slot, lambda cp, k: cp.start(priority=k % 2))
    for e in range(N_EXPERTS):
        @pl.when(live(i, e, MAIN))
        def _(e=e):
            more = jnp.where(ranks[e] - float(MAIN * SEG) == rank_row, 1.0, 0.0).astype(BF)
            extra[...] = _dot(more, h).astype(BF)
            for k in range(MAIN, n_piece):
                @pl.when(live(i, e, k))
                def _(k=k):
                    cp = pltpu.make_async_copy(extra.at[pl.ds((k - MAIN) * SEG, SEG)],
                                               xs_hbm.at[pl.ds(row0(i, e, k), SEG)], esem)
                    cp.start()
                    cp.wait()

    @pl.when(i == pl.num_programs(0) - 1)
    def _():
        each_main(i, slot, lambda cp, k: cp.wait())


def _dispatch(h2, route_t, seg, cnt, xs, tm=MOE_TILE):
    T = h2.shape[0]
    grid_spec = pltpu.PrefetchScalarGridSpec(
        num_scalar_prefetch=2,
        grid=(T // tm,),
        in_specs=[pl.BlockSpec((tm, D), lambda i, s, c: (i, 0)),
                  pl.BlockSpec((1, SUBLANES, tm), lambda i, s, c: (i, 0, 0)),
                  pl.BlockSpec(memory_space=pl.ANY)],
        out_specs=pl.BlockSpec(memory_space=pl.ANY),
        scratch_shapes=[pltpu.VMEM((2, N_EXPERTS, MAIN * SEG, D), BF), pltpu.VMEM((MAIN * SEG, D), BF),
                        pltpu.SemaphoreType.DMA((2, N_EXPERTS * MAIN)), pltpu.SemaphoreType.DMA(())],
    )
    return pl.pallas_call(
        _dispatch_kernel,
        grid_spec=grid_spec,
        out_shape=jax.ShapeDtypeStruct(xs.shape, xs.dtype),
        input_output_aliases={4: 0},
        compiler_params=_params(1),
    )(seg, cnt, h2, route_t, xs)


def _combine_kernel(seg_ref, cnt_ref, ys_hbm, route_ref, x_ref, g2_ref, ng_ref, o_ref, gbuf, extra, acc_ref, sem,
                    esem):
    i = pl.program_id(0)
    bb, tl, _ = x_ref.shape
    tm = bb * tl
    n_piece = tm // SEG

    def main_copy(step, slot, e, k):
        src = ys_hbm.at[pl.ds(_seg_start(seg_ref, step, e) + k * SEG, SEG)]
        return pltpu.make_async_copy(src, gbuf.at[slot, pl.ds((e * MAIN + k) * SEG, SEG)], sem.at[slot, e * MAIN + k])

    def fetch(step, slot):
        for e in range(N_EXPERTS):
            for k in range(MAIN):
                main_copy(step, slot, e, k).start(priority=k % 2)

    @pl.when(i == 0)
    def _():
        fetch(0, 0)

    @pl.when(i + 1 < pl.num_programs(0))
    def _():
        fetch(i + 1, (i + 1) % 2)

    slot = i % 2
    route = route_ref[...]

    def spread_cols(e, r0, width):
        ef = float(e)
        gate = jnp.where(route[:, 2:3] == ef, route[:, 0:1], jnp.where(route[:, 3:4] == ef, route[:, 1:2], 0.0))
        rank = jnp.where(route[:, 2:3] == ef, route[:, 4:5], jnp.where(route[:, 3:4] == ef, route[:, 5:6], -1.0))
        col = lax.broadcasted_iota(jnp.int32, (tm, width), 1).astype(F32) + float(r0)
        return (jnp.where(rank == col, 1.0, 0.0) * gate).astype(BF)

    spread = jnp.concatenate([spread_cols(e, 0, MAIN * SEG) for e in range(N_EXPERTS)], axis=1)
    for e in range(N_EXPERTS):
        for k in range(MAIN):
            main_copy(i, slot, e, k).wait()
    acc_ref[...] = _dot(spread, gbuf[slot].astype(BF))
    for e in range(N_EXPERTS):
        for k in range(MAIN, n_piece):
            @pl.when(k * SEG < cnt_ref[i * N_EXPERTS + e])
            def _(e=e, k=k):
                src = ys_hbm.at[pl.ds(_seg_start(seg_ref, i, e) + k * SEG, SEG)]
                cp = pltpu.make_async_copy(src, extra, esem)
                cp.start()
                more = spread_cols(e, k * SEG, SEG)
                cp.wait()
                acc_ref[...] += _dot(more, extra[...].astype(BF))

    o_ref[...] = x_ref[...] + g2_ref[...] * _rms(acc_ref[...], ng_ref[0]).reshape(bb, tl, D)


def _combine(ys, route, seg, cnt, x, g2, ng):
    B, L, _ = x.shape
    r = _Rows(B, L, MOE_TILE)
    tm = r.tm
    nl = r.nl
    act = pl.BlockSpec((r.bb, r.tl, D), lambda i, s, c: (i // nl, i % nl, 0))
    g2_spec = (pl.BlockSpec((1, 1, D), lambda i, s, c: (0, 0, 0)) if g2.shape[0] == 1 else
               pl.BlockSpec((r.bb, 1, D), lambda i, s, c: (i // nl, 0, 0)))
    grid_spec = pltpu.PrefetchScalarGridSpec(
        num_scalar_prefetch=2,
        grid=(r.n,),
        in_specs=[pl.BlockSpec(memory_space=pl.ANY), pl.BlockSpec((tm, LANES), lambda i, s, c: (i, 0)), act, g2_spec,
                  pl.BlockSpec((1, 1, D), lambda i, s, c: (0, 0, 0))],
        out_specs=act,
        scratch_shapes=[pltpu.VMEM((2, N_EXPERTS * MAIN * SEG, D), BF), pltpu.VMEM((SEG, D), BF),
                        pltpu.VMEM((tm, D), F32), pltpu.SemaphoreType.DMA((2, N_EXPERTS * MAIN)),
                        pltpu.SemaphoreType.DMA(())],
    )
    return pl.pallas_call(
        _combine_kernel,
        grid_spec=grid_spec,
        out_shape=jax.ShapeDtypeStruct((B, L, D), F32),
        compiler_params=_params(1),
    )(seg, cnt, ys, route, x, g2, ng.reshape(1, 1, D))


def _moe(passes, ng, w_gu, w_d, tmoe=1024):
    cntb = jnp.concatenate([ps['blk_cnt'][:, 0, :N_EXPERTS] for ps in passes], axis=0).astype(jnp.int32)
    n_tok_blocks = cntb.shape[0]
    n_tok = sum(ps['h2'].shape[0] * ps['h2'].shape[1] for ps in passes)
    held = (cntb + PACKED_ROWS - 1) // PACKED_ROWS * PACKED_ROWS
    before = jnp.cumsum(held, axis=0) - held
    cnt = jnp.sum(held, axis=0)
    p_cnt = (cnt + MAIN * SEG + tmoe - 1) // tmoe * tmoe
    p_end = jnp.cumsum(p_cnt)
    p_start = p_end - p_cnt
    seg = p_start[None, :] + before
    n_rows = 2 * n_tok + N_EXPERTS * (n_tok_blocks * (PACKED_ROWS - 1) + tmoe + MAIN * SEG)
    n_rows = (n_rows + tmoe - 1) // tmoe * tmoe
    n_blocks = n_rows // tmoe
    starts = jnp.arange(n_blocks, dtype=jnp.int32) * tmoe
    block_e = jnp.minimum(jnp.sum(starts[:, None] >= p_end[None, :], axis=-1), N_EXPERTS - 1).astype(jnp.int32)
    n_valid = (p_end[-1:] // tmoe).astype(jnp.int32)
    xs = jnp.zeros((n_rows, D), BF)
    first = 0
    for ps in passes:
        B, L, _ = ps['h2'].shape
        nb = ps['blk_cnt'].shape[0]
        ps['seg'] = seg[first:first + nb].reshape(-1)
        ps['cnt'] = cntb[first:first + nb].reshape(-1)
        first += nb
        xs = _dispatch(ps['h2'].reshape(B * L, D), ps['route_t'], ps['seg'], ps['cnt'], xs)
    ys = _swiglu(xs, w_gu, w_d, block_e, n_valid, tmoe, D_FF_EXPERT // 2, BF)
    return [_combine(ys, ps['route'], ps['seg'], ps['cnt'], ps['x3'], ps['g2'], ng) for ps in passes]


def _trunk(x, mods, n_row, s0, p):
    B, L, _ = x.shape
    T = B * L
    ng = p['norm_g']
    sh1, sc1, g1, sh2, sc2, g2 = mods[0]
    v, x0 = _hy_in(x, ng[0, 0], sc1, sh1, p['hy_in_w'][0], p['hy_in_b'][0], p['hy_sc_w'][0], p['hy_sc_b'][0], n_row)
    kern = _hyena_kernel_taps(L, p['hy_f_w1'][0], p['hy_f_b1'][0], p['hy_f_w2'][0], p['hy_f_b2'][0],
                              p['hy_f_freq'][0], p['hy_f_w3'][0])
    if L <= 512:
        vx = _fftconv_short(v, x0, kern, p['hy_skip'][0])
    else:
        vx = _fftconv_long(v, x0, kern, p['hy_skip'][0])
    x1, h2 = _hy_out(vx, p['hy_out_w'][0], p['hy_out_b'][0], x, g1, ng[0, 1], ng[0, 2], sc2, sh2)
    tm = 1024
    ones = jnp.zeros((T // tm,), jnp.int32)
    f = _swiglu(h2.reshape(T, D), p['ffn_wgu'], p['ffn_wd'], ones, jnp.full((1,), T // tm, jnp.int32), tm, D_FF, F32)
    g2_0, ng3_0 = g2, ng[0, 3]
    sh1, sc1, g1, sh2, sc2, g2 = mods[1]
    x2, qkv, gate, r_gate = _gla_in(x1, f, g2_0, ng3_0, ng[1, 0], sc1, sh1, p['gla_qkv_w'][0], p['gla_gk_w1'][0],
                                    p['gla_gk_w2'][0], p['gla_gk_b'][0], p['gla_r_w'][0], p['gla_r_b'][0])
    o_f, o_b, s_new = _gla_scan(qkv, gate, s0)
    x3, h2, route, route_t, blk_cnt = _gla_out(o_f, o_b, r_gate, p['gla_onorm_g'][0], p['gla_out_w'][0], x2, g1,
                                               ng[1, 1], ng[1, 2], sc2, sh2, p['moe_router'][0])
    return dict(x3=x3, h2=h2, route=route, route_t=route_t, blk_cnt=blk_cnt, g2=g2), s_new


def kernel(x_prompt, x_sample, state_gla, c, c_ctx, ada_w, ada_b, norm_g, hy_in_w, hy_in_b, hy_sc_w, hy_sc_b, hy_f_w1, hy_f_b1, hy_f_w2, hy_f_b2, hy_f_freq, hy_f_w3, hy_skip, hy_out_w, hy_out_b, gla_qkv_w, gla_gk_w1, gla_gk_w2, gla_gk_b, gla_r_w, gla_r_b, gla_onorm_g, gla_out_w, ffn_wgu, ffn_wd, moe_router, moe_wgu, moe_wd):
    p = dict(norm_g=norm_g, hy_in_w=hy_in_w, hy_in_b=hy_in_b, hy_sc_w=hy_sc_w, hy_sc_b=hy_sc_b, hy_f_w1=hy_f_w1,
             hy_f_b1=hy_f_b1, hy_f_w2=hy_f_w2, hy_f_b2=hy_f_b2, hy_f_freq=hy_f_freq, hy_f_w3=hy_f_w3,
             hy_skip=hy_skip, hy_out_w=hy_out_w, hy_out_b=hy_out_b, gla_qkv_w=gla_qkv_w, gla_gk_w1=gla_gk_w1,
             gla_gk_w2=gla_gk_w2, gla_gk_b=gla_gk_b, gla_r_w=gla_r_w, gla_r_b=gla_r_b, gla_onorm_g=gla_onorm_g,
             gla_out_w=gla_out_w, ffn_wgu=ffn_wgu.astype(BF), ffn_wd=ffn_wd.astype(BF),
             moe_router=moe_router, moe_wgu=moe_wgu[0].astype(BF), moe_wd=moe_wd[0].astype(BF))
    n_dec = c.shape[0]
    cond = jnp.concatenate([c_ctx[None, :], c, jnp.zeros((16 - 1 - n_dec, D), F32)], axis=0)
    mod = _ada(cond, ada_w, ada_b)
    mods_ctx = [[m[:, None, :] for m in jnp.split(mod[l, 0:1], 6, axis=-1)] for l in range(DEPTH)]
    mods_dec = [[m[:, None, :] for m in jnp.split(mod[l, 1:1 + n_dec], 6, axis=-1)] for l in range(DEPTH)]
    ctx, state_new = _trunk(x_prompt, mods_ctx, x_prompt.shape[1], None, p)
    grid_w = 64
    dec, _ = _trunk(x_sample, mods_dec, grid_w, state_gla, p)
    y_prompt, y_sample = _moe([ctx, dec], norm_g[1, 3], p['moe_wgu'], p['moe_wd'])
    return y_prompt, y_sample, state_new
```
